```python
import math
import jax, jax.numpy as jnp
from jax import lax
import numpy as np


D_MODEL = 1024
BATCH = 16
SEQ = 2048
DEPTH = 2

N_MIXERS = 2
HEAD_DIM = 64
N_Q_HEADS = D_MODEL // HEAD_DIM
N_KV_HEADS = N_Q_HEADS // 4
GQA_GROUP = N_Q_HEADS // N_KV_HEADS
WINDOW = 128
BLOCK = 128
REL_BUCKETS = 32
REL_MAX_DIST = 128
QKV_DIM = (N_Q_HEADS + 2 * N_KV_HEADS) * HEAD_DIM
CHUNK = 128
D_FFN_SG = 6 * D_MODEL
D_GATE = D_FFN_SG // 2
N_SG_GROUPS = 8
SG_GROUP_DIM = D_GATE // N_SG_GROUPS
N_KEYS = 128
N_EXPERTS = N_KEYS * N_KEYS
PEER_HEADS = 8
PEER_TOPK = 16
D_KEY = 256
D_HALF = D_KEY // 2
TOKEN_CHUNK = 128

NORM_EPS = 1e-6
LN_EPS = 1e-5
N_ATTN_LAYERS = (DEPTH + N_MIXERS - 1) // N_MIXERS
N_SG_LAYERS = DEPTH // N_MIXERS

kernel_name = "hybrid_swa_gmlp_peer_encoder"


def rms_norm(x, g):
    xf = x.astype(jnp.float32)
    y = xf * lax.rsqrt(jnp.mean(xf * xf, axis=-1, keepdims=True) + NORM_EPS)
    return (y * g.astype(jnp.float32)).astype(x.dtype)


def t5_bucket(rel):
    nb = REL_BUCKETS // 2
    max_exact = nb // 2
    ret = jnp.where(rel > 0, nb, 0)
    n = jnp.abs(rel)
    nf = jnp.maximum(n, 1).astype(jnp.float32)
    large = max_exact + (jnp.log(nf / max_exact) / math.log(REL_MAX_DIST / max_exact)
                         * (nb - max_exact)).astype(jnp.int32)
    large = jnp.minimum(large, nb - 1)
    return ret + jnp.where(n < max_exact, n, large)


def windowed_gqa_attention(h, w_in, w_out, sink, rel_table):
    B, S, _ = h.shape
    nb = S // BLOCK
    qkv = h @ w_in
    q, k, v = jnp.split(qkv, [N_Q_HEADS * HEAD_DIM, (N_Q_HEADS + N_KV_HEADS) * HEAD_DIM], axis=-1)
    q = q.reshape(B, nb, BLOCK, N_KV_HEADS, GQA_GROUP, HEAD_DIM) * (HEAD_DIM ** -0.5)
    k = k.reshape(B, S, N_KV_HEADS, HEAD_DIM)
    v = v.reshape(B, S, N_KV_HEADS, HEAD_DIM)
    pad = ((0, 0), (BLOCK, BLOCK), (0, 0), (0, 0))
    kp = jnp.pad(k, pad).reshape(B, nb + 2, BLOCK, N_KV_HEADS, HEAD_DIM)
    vp = jnp.pad(v, pad).reshape(B, nb + 2, BLOCK, N_KV_HEADS, HEAD_DIM)

    def band(t):
        return jnp.concatenate([t[:, :-2], t[:, 1:-1], t[:, 2:]], axis=2)

    kb, vb = band(kp), band(vp)
    qi = jnp.arange(BLOCK)[:, None]
    kj = jnp.arange(3 * BLOCK)[None, :]
    rel = kj - BLOCK - qi
    bias = rel_table[t5_bucket(rel)]
    bias = bias.astype(jnp.float32).transpose(2, 0, 1).reshape(N_KV_HEADS, GQA_GROUP, BLOCK, 3 * BLOCK)
    key_pos = (jnp.arange(nb)[:, None] - 1) * BLOCK + jnp.arange(3 * BLOCK)[None, :]
    valid = (jnp.abs(rel) <= WINDOW)[None] & ((key_pos >= 0) & (key_pos < S))[:, None, :]
    sink_f = sink.astype(jnp.float32).reshape(N_KV_HEADS, GQA_GROUP, 1, 1)

    def attend(args):
        qb, kb_, vb_, m = args
        s = jnp.einsum('bqhgd,bkhd->bhgqk', qb, kb_).astype(jnp.float32) + bias
        s = jnp.where(m, s, -jnp.inf)
        mx = jnp.maximum(jnp.max(s, axis=-1, keepdims=True), sink_f)
        p = jnp.exp(s - mx)
        denom = jnp.sum(p, axis=-1, keepdims=True) + jnp.exp(sink_f - mx)
        p = (p / denom).astype(vb_.dtype)
        return jnp.einsum('bhgqk,bkhd->bqhgd', p, vb_)

    out = lax.map(attend, (q.transpose(1, 0, 2, 3, 4, 5), kb.transpose(1, 0, 2, 3, 4),
                           vb.transpose(1, 0, 2, 3, 4), valid))
    out = out.transpose(1, 0, 2, 3, 4, 5).reshape(B, S, N_Q_HEADS * HEAD_DIM)
    return out @ w_out


def chunked_spatial_gating(h, w_in, ln_g, ln_b, w_sp, b_sp, w_out):
    B, S, _ = h.shape
    nc = S // CHUNK
    z = jax.nn.gelu(h @ w_in)
    u, v = jnp.split(z, 2, axis=-1)
    vf = v.astype(jnp.float32)
    mu = jnp.mean(vf, axis=-1, keepdims=True)
    var = jnp.mean(jnp.square(vf - mu), axis=-1, keepdims=True)
    v = ((vf - mu) * lax.rsqrt(var + LN_EPS) * ln_g.astype(jnp.float32)
         + ln_b.astype(jnp.float32)).astype(h.dtype)
    v = v.reshape(B, nc, CHUNK, N_SG_GROUPS, SG_GROUP_DIM)
    mixed = jnp.einsum('gpq,bcqgd->bcpgd', w_sp, v) + b_sp.T[None, None, :, :, None]
    gated = u * mixed.reshape(B, S, D_GATE)
    return gated @ w_out


def peer(h, w_query, sub_keys, expert_down, expert_up):
    B, S, D = h.shape
    T = B * S
    x = h.reshape(T, D)
    q = (x @ w_query).reshape(T, PEER_HEADS, 2, D_HALF)
    s = jnp.einsum('thcd,hckd->thck', q, sub_keys).astype(jnp.float32)
    s1, i1 = lax.top_k(s[:, :, 0], PEER_TOPK)
    s2, i2 = lax.top_k(s[:, :, 1], PEER_TOPK)
    cand = (s1[..., :, None] + s2[..., None, :]).reshape(T, PEER_HEADS, PEER_TOPK * PEER_TOPK)
    cidx = (i1[..., :, None] * N_KEYS + i2[..., None, :]).reshape(T, PEER_HEADS, PEER_TOPK * PEER_TOPK)
    top_s, pos = lax.top_k(cand, PEER_TOPK)
    idx = jnp.take_along_axis(cidx, pos, axis=-1)
    gate = jax.nn.softmax(top_s, axis=-1).astype(h.dtype)
    nch = T // TOKEN_CHUNK

    def experts(args):
        xc, ic, gc = args
        a = jax.nn.gelu(jnp.einsum('cd,chkd->chk', xc, expert_down[ic]))
        return jnp.einsum('chk,chkd->cd', a * gc, expert_up[ic])

    y = lax.map(experts, (x.reshape(nch, TOKEN_CHUNK, D),
                          idx.reshape(nch, TOKEN_CHUNK, PEER_HEADS, PEER_TOPK),
                          gate.reshape(nch, TOKEN_CHUNK, PEER_HEADS, PEER_TOPK)))
    return y.reshape(B, S, D)


def setup_inputs(seed: int = 0) -> dict:
    key = jax.random.key(seed)
    ks = jax.random.split(key, 20)
    nrm = jax.random.normal
    f32 = jnp.float32
    return {
        "x": nrm(ks[0], (BATCH, SEQ, D_MODEL), f32),
        "rel_table": 0.5 * nrm(ks[1], (REL_BUCKETS, N_Q_HEADS), f32),
        "mix_norm_g": 1.0 + 0.05 * nrm(ks[2], (DEPTH, D_MODEL), f32),
        "attn_w_in": nrm(ks[3], (N_ATTN_LAYERS, D_MODEL, QKV_DIM), f32) * D_MODEL ** -0.5,
        "attn_w_out": nrm(ks[4], (N_ATTN_LAYERS, N_Q_HEADS * HEAD_DIM, D_MODEL), f32) * (N_Q_HEADS * HEAD_DIM) ** -0.5,
        "attn_sink": 0.5 * nrm(ks[5], (N_ATTN_LAYERS, N_Q_HEADS), f32),
        "sg_w_in": nrm(ks[6], (N_SG_LAYERS, D_MODEL, D_FFN_SG), f32) * D_MODEL ** -0.5,
        "sg_ln_g": 1.0 + 0.05 * nrm(ks[7], (N_SG_LAYERS, D_GATE), f32),
        "sg_ln_b": 0.02 * nrm(ks[8], (N_SG_LAYERS, D_GATE), f32),
        "sg_w_spatial": nrm(ks[9], (N_SG_LAYERS, N_SG_GROUPS, CHUNK, CHUNK), f32) * CHUNK ** -0.5,
        "sg_b_spatial": 1.0 + 0.02 * nrm(ks[10], (N_SG_LAYERS, N_SG_GROUPS, CHUNK), f32),
        "sg_w_out": nrm(ks[11], (N_SG_LAYERS, D_GATE, D_MODEL), f32) * D_GATE ** -0.5,
        "ffn_norm_g": 1.0 + 0.05 * nrm(ks[12], (DEPTH, D_MODEL), f32),
        "peer_w_query": nrm(ks[13], (DEPTH, D_MODEL, PEER_HEADS * D_KEY), f32) * D_MODEL ** -0.5,
        "peer_sub_keys": nrm(ks[14], (DEPTH, PEER_HEADS, 2, N_KEYS, D_HALF), f32) * D_HALF ** -0.5,
        "peer_down": nrm(ks[15], (DEPTH, N_EXPERTS, D_MODEL), f32) * D_MODEL ** -0.5,
        "peer_up": nrm(ks[16], (DEPTH, N_EXPERTS, D_MODEL), f32) * PEER_HEADS ** -0.5,
        "final_norm_g": 1.0 + 0.05 * nrm(ks[17], (D_MODEL,), f32),
    }


def reference(x, rel_table, mix_norm_g, attn_w_in, attn_w_out, attn_sink, sg_w_in, sg_ln_g,
              sg_ln_b, sg_w_spatial, sg_b_spatial, sg_w_out, ffn_norm_g, peer_w_query,
              peer_sub_keys, peer_down, peer_up, final_norm_g):
    h = x
    for i in range(DEPTH):
        j = i // N_MIXERS
        hn = rms_norm(h, mix_norm_g[i])
        if i % N_MIXERS == 0:
            h = h + windowed_gqa_attention(hn, attn_w_in[j], attn_w_out[j], attn_sink[j], rel_table)
        else:
            h = h + chunked_spatial_gating(hn, sg_w_in[j], sg_ln_g[j], sg_ln_b[j],
                                           sg_w_spatial[j], sg_b_spatial[j], sg_w_out[j])
        h = h + peer(rms_norm(h, ffn_norm_g[i]), peer_w_query[i], peer_sub_keys[i],
                     peer_down[i], peer_up[i])
    return rms_norm(h, final_norm_g)
```

```python
import functools
import math

import numpy as np
import jax
import jax.numpy as jnp
from jax import lax
from jax.experimental import pallas as pl
from jax.experimental.pallas import tpu as pltpu

D_MODEL = 1024
HEAD_DIM = 64
N_Q_HEADS = 16
N_KV_HEADS = 4
GQA_GROUP = 4
WINDOW = 128
BLOCK = 128
REL_BUCKETS = 32
REL_MAX_DIST = 128
Q_DIM = N_Q_HEADS * HEAD_DIM
KV_DIM = N_KV_HEADS * HEAD_DIM
QKV_DIM = Q_DIM + 2 * KV_DIM
CHUNK = 128
D_GATE = 3072
N_SG_GROUPS = 8
SG_GROUP_DIM = D_GATE // N_SG_GROUPS
N_KEYS = 128
PEER_HEADS = 8
PEER_TOPK = 16
D_HALF = 128
N_SLOTS = PEER_HEADS * PEER_TOPK
NORM_EPS = 1e-6
LN_EPS = 1e-5

LANES = 128
SUBLANES = 8
ROW_WORDS = D_MODEL // 2
ROW_SUB = ROW_WORDS // LANES
PROD_STRIDE = 136
VMEM_LIMIT = 56 * 1024 * 1024

_GELU_C = math.sqrt(2.0 / math.pi)


def _gelu(x):
    return 0.5 * x * (1.0 + jnp.tanh(_GELU_C * (x + 0.044715 * (x * x * x))))


def _cparams(sem):
    return pltpu.CompilerParams(dimension_semantics=sem, vmem_limit_bytes=VMEM_LIMIT)


def _norm_matmul_kernel(x_ref, g_ref, w_ref, *rest, act, emit_xn, precision):
    if emit_xn:
        o_ref, xn_ref, xs_ref = rest
    else:
        o_ref, xs_ref = rest

    @pl.when(pl.program_id(1) == 0)
    def _():
        x = x_ref[...]
        y = x * lax.rsqrt(jnp.mean(x * x, axis=-1, keepdims=True) + NORM_EPS) * g_ref[...]
        xs_ref[...] = y.astype(xs_ref.dtype)
        if emit_xn:
            xn_ref[...] = y

    acc = jnp.dot(xs_ref[...], w_ref[...], preferred_element_type=jnp.float32,
                  precision=precision)
    if act:
        acc = _gelu(acc)
    o_ref[...] = acc.astype(o_ref.dtype)


def norm_matmul(x, g, w, *, out_dtype, act=False, emit_xn=False, tm=512, tn=None,
                precision=None):
    T, D = x.shape
    N = w.shape[1]
    tn = tn or N
    tm = min(tm, T)
    out_shape = [jax.ShapeDtypeStruct((T, N), out_dtype)]
    out_specs = [pl.BlockSpec((tm, tn), lambda i, j: (i, j))]
    if emit_xn:
        out_shape.append(jax.ShapeDtypeStruct((T, D), jnp.float32))
        out_specs.append(pl.BlockSpec((tm, D), lambda i, j: (i, 0)))
    res = pl.pallas_call(
        functools.partial(_norm_matmul_kernel, act=act, emit_xn=emit_xn, precision=precision),
        grid=(T // tm, N // tn),
        in_specs=[pl.BlockSpec((tm, D), lambda i, j: (i, 0)),
                  pl.BlockSpec((1, D), lambda i, j: (0, 0)),
                  pl.BlockSpec((D, tn), lambda i, j: (0, j))],
        out_specs=out_specs,
        out_shape=out_shape,
        scratch_shapes=[pltpu.VMEM((tm, D), w.dtype)],
        compiler_params=_cparams(("parallel", "arbitrary")),
        name="norm_matmul",
    )(x, g.reshape(1, D), w)
    return res if emit_xn else res[0]


def _matmul_res_kernel(a_ref, w_ref, h_ref, o_ref):
    o_ref[...] = h_ref[...] + jnp.dot(a_ref[...], w_ref[...],
                                      preferred_element_type=jnp.float32)


def matmul_residual(a, w, h, *, tm=512):
    T, K = a.shape
    N = w.shape[1]
    tm = min(tm, T)
    return pl.pallas_call(
        _matmul_res_kernel,
        grid=(T // tm,),
        in_specs=[pl.BlockSpec((tm, K), lambda i: (i, 0)),
                  pl.BlockSpec((K, N), lambda i: (0, 0)),
                  pl.BlockSpec((tm, N), lambda i: (i, 0))],
        out_specs=pl.BlockSpec((tm, N), lambda i: (i, 0)),
        out_shape=jax.ShapeDtypeStruct((T, N), jnp.float32),
        compiler_params=_cparams(("parallel",)),
        name="matmul_residual",
    )(a, w, h)


def _t5_bucket(rel):
    nb = REL_BUCKETS // 2
    max_exact = nb // 2
    ret = jnp.where(rel > 0, nb, 0)
    n = jnp.abs(rel)
    nf = jnp.maximum(n, 1).astype(jnp.float32)
    large = max_exact + (jnp.log(nf / max_exact) / math.log(REL_MAX_DIST / max_exact)
                         * (nb - max_exact)).astype(jnp.int32)
    large = jnp.minimum(large, nb - 1)
    return (ret + jnp.where(n < max_exact, n, large)).astype(jnp.int32)


def _bias_kernel(bucket_ref, window_ref, table_ref, o_ref):
    bucket = bucket_ref[...]
    in_window = window_ref[...] > 0
    for hq in range(N_Q_HEADS):
        acc = jnp.zeros(bucket.shape, jnp.float32)
        for b in range(REL_BUCKETS):
            acc = jnp.where(bucket == b, table_ref[b, hq], acc)
        o_ref[hq] = jnp.where(in_window, acc, -jnp.inf)


def attn_bias(rel_table):
    qi = jnp.arange(BLOCK)[:, None]
    kj = jnp.arange(3 * BLOCK)[None, :]
    rel = kj - BLOCK - qi
    bucket = _t5_bucket(rel)
    window = (jnp.abs(rel) <= WINDOW).astype(jnp.int32)
    return pl.pallas_call(
        _bias_kernel,
        in_specs=[pl.BlockSpec(memory_space=pltpu.VMEM),
                  pl.BlockSpec(memory_space=pltpu.VMEM),
                  pl.BlockSpec(memory_space=pltpu.SMEM)],
        out_specs=pl.BlockSpec(memory_space=pltpu.VMEM),
        out_shape=jax.ShapeDtypeStruct((N_Q_HEADS, BLOCK, 3 * BLOCK), jnp.float32),
        name="attn_bias",
    )(bucket, window, rel_table)


def _attn_kernel(cur_ref, prev_ref, next_ref, bias_ref, sink_ref, o_ref):
    i = pl.program_id(1)
    nb = pl.num_programs(1)
    q = cur_ref[0, :, 0:Q_DIM]
    kband = jnp.concatenate([prev_ref[0, :, 0:KV_DIM], cur_ref[0, :, Q_DIM:Q_DIM + KV_DIM],
                             next_ref[0, :, 0:KV_DIM]], axis=0)
    vband = jnp.concatenate([prev_ref[0, :, KV_DIM:2 * KV_DIM], cur_ref[0, :, Q_DIM + KV_DIM:QKV_DIM],
                             next_ref[0, :, KV_DIM:2 * KV_DIM]], axis=0)
    col = lax.broadcasted_iota(jnp.int32, (1, 3 * BLOCK), 1)
    valid = jnp.logical_and(jnp.logical_or(col >= BLOCK, i > 0),
                            jnp.logical_or(col < 2 * BLOCK, i < nb - 1))
    outs = []
    for hk in range(N_KV_HEADS):
        kh = kband[:, hk * HEAD_DIM:(hk + 1) * HEAD_DIM]
        vh = vband[:, hk * HEAD_DIM:(hk + 1) * HEAD_DIM]
        for g in range(GQA_GROUP):
            hq = hk * GQA_GROUP + g
            qh = q[:, hq * HEAD_DIM:(hq + 1) * HEAD_DIM]
            s = lax.dot_general(qh, kh, (((1,), (1,)), ((), ())),
                                preferred_element_type=jnp.float32)
            s = s * (HEAD_DIM ** -0.5) + bias_ref[hq]
            s = jnp.where(valid, s, -jnp.inf)
            sk = sink_ref[hq]
            mx = jnp.maximum(jnp.max(s, axis=-1, keepdims=True), sk)
            p = jnp.exp(s - mx)
            denom = jnp.sum(p, axis=-1, keepdims=True) + jnp.exp(sk - mx)
            p = (p / denom).astype(vh.dtype)
            outs.append(jnp.dot(p, vh, preferred_element_type=jnp.float32))
    o_ref[0] = jnp.concatenate(outs, axis=-1).astype(o_ref.dtype)


def window_attention(qkv, bias, sink):
    B, S, _ = qkv.shape
    nb = S // BLOCK
    kv_col = Q_DIM // (2 * KV_DIM)
    return pl.pallas_call(
        _attn_kernel,
        grid=(B, nb),
        in_specs=[pl.BlockSpec((1, BLOCK, QKV_DIM), lambda b, i: (b, i, 0)),
                  pl.BlockSpec((1, BLOCK, 2 * KV_DIM),
                               lambda b, i: (b, jnp.maximum(i - 1, 0), kv_col)),
                  pl.BlockSpec((1, BLOCK, 2 * KV_DIM),
                               lambda b, i: (b, jnp.minimum(i + 1, nb - 1), kv_col)),
                  pl.BlockSpec((N_Q_HEADS, BLOCK, 3 * BLOCK), lambda b, i: (0, 0, 0)),
                  pl.BlockSpec(memory_space=pltpu.SMEM)],
        out_specs=pl.BlockSpec((1, BLOCK, Q_DIM), lambda b, i: (b, i, 0)),
        out_shape=jax.ShapeDtypeStruct((B, S, Q_DIM), jnp.bfloat16),
        compiler_params=_cparams(("parallel", "arbitrary")),
        name="window_attention",
    )(qkv, qkv, qkv, bias, sink)


def _spatial_gate_kernel(z_ref, g_ref, b_ref, wsp_ref, bsp_ref, o_ref):
    v = z_ref[:, D_GATE:2 * D_GATE].astype(jnp.float32)
    mu = jnp.mean(v, axis=-1, keepdims=True)
    vc = v - mu
    var = jnp.mean(vc * vc, axis=-1, keepdims=True)
    vn = (vc * lax.rsqrt(var + LN_EPS) * g_ref[...] + b_ref[...]).astype(jnp.bfloat16)
    for grp in range(N_SG_GROUPS):
        lo, hi = grp * SG_GROUP_DIM, (grp + 1) * SG_GROUP_DIM
        mixed = jnp.dot(wsp_ref[grp], vn[:, lo:hi], preferred_element_type=jnp.float32)
        mixed = mixed + bsp_ref[:, grp:grp + 1]
        u = z_ref[:, lo:hi].astype(jnp.float32)
        o_ref[:, lo:hi] = (u * mixed).astype(o_ref.dtype)


def spatial_gate(z, ln_g, ln_b, w_sp, b_sp):
    T = z.shape[0]
    return pl.pallas_call(
        _spatial_gate_kernel,
        grid=(T // CHUNK,),
        in_specs=[pl.BlockSpec((CHUNK, 2 * D_GATE), lambda i: (i, 0)),
                  pl.BlockSpec((1, D_GATE), lambda i: (0, 0)),
                  pl.BlockSpec((1, D_GATE), lambda i: (0, 0)),
                  pl.BlockSpec((N_SG_GROUPS, CHUNK, CHUNK), lambda i: (0, 0, 0)),
                  pl.BlockSpec((CHUNK, N_SG_GROUPS), lambda i: (0, 0))],
        out_specs=pl.BlockSpec((CHUNK, D_GATE), lambda i: (i, 0)),
        out_shape=jax.ShapeDtypeStruct((T, D_GATE), jnp.bfloat16),
        compiler_params=_cparams(("parallel",)),
        name="spatial_gate",
    )(z, ln_g.reshape(1, D_GATE), ln_b.reshape(1, D_GATE), w_sp.astype(jnp.bfloat16), b_sp.T)


def _take_top(s, ids, n):
    rows = lax.broadcasted_iota(jnp.int32, s.shape, 0)
    vals, picks = [], []
    for _ in range(n):
        m = jnp.max(s, axis=0, keepdims=True)
        am = jnp.min(jnp.where(s == m, rows, s.shape[0]), axis=0, keepdims=True)
        hit = rows == am
        vals.append(m)
        if ids is None:
            picks.append(am)
        else:
            picks.append(jnp.max(jnp.where(hit, ids, -1), axis=0, keepdims=True))
        s = jnp.where(hit, -jnp.inf, s)
    return jnp.concatenate(vals, axis=0), jnp.concatenate(picks, axis=0)


def _route_kernel(q_ref, keys_ref, idx_ref, gate_ref, *, precision):
    nt = (((1,), (1,)), ((), ()))
    q = q_ref[...]
    s1 = lax.dot_general(keys_ref[0, 0], q[:, 0:D_HALF], nt,
                         preferred_element_type=jnp.float32, precision=precision)
    s2 = lax.dot_general(keys_ref[0, 1], q[:, D_HALF:2 * D_HALF], nt,
                         preferred_element_type=jnp.float32, precision=precision)
    v1, i1 = _take_top(s1, None, PEER_TOPK)
    v2, i2 = _take_top(s2, None, PEER_TOPK)
    cand = jnp.concatenate([v1[a:a + 1] + v2 for a in range(PEER_TOPK)], axis=0)
    cidx = jnp.concatenate([i1[a:a + 1] * N_KEYS + i2 for a in range(PEER_TOPK)], axis=0)
    top_s, top_i = _take_top(cand, cidx, PEER_TOPK)
    e = jnp.exp(top_s - top_s[0:1])
    gate_ref[...] = e / jnp.sum(e, axis=0, keepdims=True)
    idx_ref[...] = top_i * ROW_SUB


def peer_route(q, sub_keys, *, tb=256, precision=None):
    T = q.shape[0]
    tb = min(tb, T)
    return pl.pallas_call(
        functools.partial(_route_kernel, precision=precision),
        grid=(T // tb, PEER_HEADS),
        in_specs=[pl.BlockSpec((tb, 2 * D_HALF), lambda i, h: (i, h)),
                  pl.BlockSpec((1, 2, N_KEYS, D_HALF), lambda i, h: (h, 0, 0, 0))],
        out_specs=[pl.BlockSpec((PEER_TOPK, tb), lambda i, h: (h, i)),
                   pl.BlockSpec((PEER_TOPK, tb), lambda i, h: (h, i))],
        out_shape=[jax.ShapeDtypeStruct((N_SLOTS, T), jnp.int32),
                   jax.ShapeDtypeStruct((N_SLOTS, T), jnp.float32)],
        compiler_params=_cparams(("parallel", "arbitrary")),
        name="peer_route",
    )(q, sub_keys)


def pack_table(tbl):
    E, D = tbl.shape
    bits = lax.bitcast_convert_type(tbl.astype(jnp.bfloat16), jnp.uint16).astype(jnp.uint32)
    packed = bits[:, :D // 2] | (bits[:, D // 2:] << 16)
    return packed.reshape(E * ROW_SUB, LANES)


def _unpack(slab):
    lo = pltpu.bitcast(slab << 16, jnp.float32)
    hi = pltpu.bitcast(slab & jnp.uint32(0xFFFF0000), jnp.float32)
    return lo, hi


def _peer_down_kernel(idx_ref, x_ref, gate_ref, tbl_ref, o_ref, prod_ref):
    tb = x_ref.shape[0]
    lane = lax.broadcasted_iota(jnp.int32, (N_SLOTS, tb), 1)

    def token(t, acc):
        xv = x_ref[t]
        x_lo, x_hi = xv[0:ROW_SUB], xv[ROW_SUB:2 * ROW_SUB]
        for k in range(N_SLOTS):
            off = pl.multiple_of(idx_ref[k, t], ROW_SUB)
            lo, hi = _unpack(tbl_ref[pl.ds(off, ROW_SUB), :])
            prod_ref[pl.ds(k, ROW_SUB, stride=PROD_STRIDE), :] = lo * x_lo + hi * x_hi
        plane = prod_ref[0:N_SLOTS, :]
        for c in range(1, ROW_SUB):
            plane = plane + prod_ref[c * PROD_STRIDE:c * PROD_STRIDE + N_SLOTS, :]
        col = jnp.sum(plane, axis=1, keepdims=True)
        return jnp.where(lane == t, col, acc)

    a = lax.fori_loop(0, tb, token, jnp.zeros((N_SLOTS, tb), jnp.float32))
    o_ref[...] = _gelu(a) * gate_ref[...]


def peer_down(idx_t, xn3, gate_t, tbl, *, tb=128):
    T = xn3.shape[0]
    tb = min(tb, T)
    return pl.pallas_call(
        _peer_down_kernel,
        grid=(T // tb,),
        in_specs=[pl.BlockSpec((N_SLOTS, tb), lambda i: (0, i), memory_space=pltpu.SMEM),
                  pl.BlockSpec((tb, SUBLANES, LANES), lambda i: (i, 0, 0)),
                  pl.BlockSpec((N_SLOTS, tb), lambda i: (0, i)),
                  pl.BlockSpec(tbl.shape, lambda i: (0, 0), pipeline_mode=pl.Buffered(1))],
        out_specs=pl.BlockSpec((N_SLOTS, tb), lambda i: (0, i)),
        out_shape=jax.ShapeDtypeStruct((N_SLOTS, T), jnp.float32),
        scratch_shapes=[pltpu.VMEM((ROW_SUB * PROD_STRIDE, LANES), jnp.float32)],
        compiler_params=_cparams(("arbitrary",)),
        name="peer_down",
    )(idx_t, xn3, gate_t, tbl)


def _peer_up_kernel(idx_ref, w_ref, h_ref, tbl_ref, o_ref):
    tb = h_ref.shape[0]
    n_acc = 4

    def token(t, carry):
        acc_lo = [jnp.zeros((ROW_SUB, LANES), jnp.float32) for _ in range(n_acc)]
        acc_hi = [jnp.zeros((ROW_SUB, LANES), jnp.float32) for _ in range(n_acc)]
        for k in range(N_SLOTS):
            off = pl.multiple_of(idx_ref[k, t], ROW_SUB)
            lo, hi = _unpack(tbl_ref[pl.ds(off, ROW_SUB), :])
            w = w_ref[k, t]
            acc_lo[k % n_acc] = acc_lo[k % n_acc] + lo * w
            acc_hi[k % n_acc] = acc_hi[k % n_acc] + hi * w
        y_lo = (acc_lo[0] + acc_lo[1]) + (acc_lo[2] + acc_lo[3])
        y_hi = (acc_hi[0] + acc_hi[1]) + (acc_hi[2] + acc_hi[3])
        o_ref[t] = h_ref[t] + jnp.concatenate([y_lo, y_hi], axis=0)
        return carry

    lax.fori_loop(0, tb, token, 0)


def peer_up(idx_t, w_t, h3, tbl, *, tb=128):
    T = h3.shape[0]
    tb = min(tb, T)
    return pl.pallas_call(
        _peer_up_kernel,
        grid=(T // tb,),
        in_specs=[pl.BlockSpec((N_SLOTS, tb), lambda i: (0, i), memory_space=pltpu.SMEM),
                  pl.BlockSpec((N_SLOTS, tb), lambda i: (0, i), memory_space=pltpu.SMEM),
                  pl.BlockSpec((tb, SUBLANES, LANES), lambda i: (i, 0, 0)),
                  pl.BlockSpec(tbl.shape, lambda i: (0, 0), pipeline_mode=pl.Buffered(1))],
        out_specs=pl.BlockSpec((tb, SUBLANES, LANES), lambda i: (i, 0, 0)),
        out_shape=jax.ShapeDtypeStruct(h3.shape, jnp.float32),
        compiler_params=_cparams(("arbitrary",)),
        name="peer_up",
    )(idx_t, w_t, h3, tbl)


def peer_layer(h, g, w_query, sub_keys, down, up):
    T = h.shape[0]
    hi = lax.Precision.HIGHEST
    q, xn = norm_matmul(h, g, w_query, out_dtype=jnp.float32, emit_xn=True, tn=512, precision=hi)
    idx_t, gate_t = peer_route(q, sub_keys, precision=hi)
    w_t = peer_down(idx_t, xn.reshape(T, SUBLANES, LANES), gate_t, pack_table(down))
    return peer_up(idx_t, w_t, h.reshape(T, SUBLANES, LANES), pack_table(up)).reshape(T, D_MODEL)


def _rms_kernel(x_ref, g_ref, o_ref):
    x = x_ref[...]
    o_ref[...] = x * lax.rsqrt(jnp.mean(x * x, axis=-1, keepdims=True) + NORM_EPS) * g_ref[...]


def rms_norm(x, g, *, tm=512):
    T, D = x.shape
    tm = min(tm, T)
    return pl.pallas_call(
        _rms_kernel,
        grid=(T // tm,),
        in_specs=[pl.BlockSpec((tm, D), lambda i: (i, 0)), pl.BlockSpec((1, D), lambda i: (0, 0))],
        out_specs=pl.BlockSpec((tm, D), lambda i: (i, 0)),
        out_shape=jax.ShapeDtypeStruct((T, D), jnp.float32),
        compiler_params=_cparams(("parallel",)),
        name="final_norm",
    )(x, g.reshape(1, D))


def kernel(x, rel_table, mix_norm_g, attn_w_in, attn_w_out, attn_sink, sg_w_in, sg_ln_g, sg_ln_b, sg_w_spatial, sg_b_spatial, sg_w_out, ffn_norm_g, peer_w_query, peer_sub_keys, peer_down, peer_up, final_norm_g):
    B, S, D = x.shape
    T = B * S
    bf16 = jnp.bfloat16
    h = x.reshape(T, D)

    qkv = norm_matmul(h, mix_norm_g[0], attn_w_in[0].astype(bf16), out_dtype=bf16)
    att = window_attention(qkv.reshape(B, S, QKV_DIM), attn_bias(rel_table), attn_sink[0])
    h = matmul_residual(att.reshape(T, Q_DIM), attn_w_out[0].astype(bf16), h)
    h = peer_layer(h, ffn_norm_g[0], peer_w_query[0], peer_sub_keys[0], peer_down[0], peer_up[0])

    z = norm_matmul(h, mix_norm_g[1], sg_w_in[0].astype(bf16), out_dtype=bf16, act=True, tn=1536)
    gated = spatial_gate(z, sg_ln_g[0], sg_ln_b[0], sg_w_spatial[0], sg_b_spatial[0])
    h = matmul_residual(gated, sg_w_out[0].astype(bf16), h)
    h = peer_layer(h, ffn_norm_g[1], peer_w_query[1], peer_sub_keys[1], peer_down[1], peer_up[1])

    return rms_norm(h, final_norm_g).reshape(B, S, D)
```

```python
import functools
import math

import numpy as np
import jax
import jax.numpy as jnp
from jax import lax
from jax.experimental import pallas as pl
from jax.experimental.pallas import tpu as pltpu

D_MODEL = 1024
HEAD_DIM = 64
N_Q_HEADS = 16
N_KV_HEADS = 4
GQA_GROUP = 4
WINDOW = 128
BLOCK = 128
REL_BUCKETS = 32
REL_MAX_DIST = 128
Q_DIM = N_Q_HEADS * HEAD_DIM
KV_DIM = N_KV_HEADS * HEAD_DIM
QKV_DIM = Q_DIM + 2 * KV_DIM
CHUNK = 128
D_GATE = 3072
N_SG_GROUPS = 8
SG_GROUP_DIM = D_GATE // N_SG_GROUPS
N_KEYS = 128
PEER_HEADS = 8
PEER_TOPK = 16
D_HALF = 128
N_SLOTS = PEER_HEADS * PEER_TOPK
NORM_EPS = 1e-6
LN_EPS = 1e-5

LANES = 128
SUBLANES = 8
ROW_WORDS = D_MODEL // 2
ROW_SUB = ROW_WORDS // LANES
PLANE_STRIDE = 136
VMEM_LIMIT = 56 * 1024 * 1024

_GELU_C = math.sqrt(2.0 / math.pi)


def _gelu(x):
    return 0.5 * x * (1.0 + jnp.tanh(_GELU_C * (x + 0.044715 * (x * x * x))))


def _cparams(sem):
    return pltpu.CompilerParams(dimension_semantics=sem, vmem_limit_bytes=VMEM_LIMIT)


def _norm_matmul_kernel(x_ref, g_ref, w_ref, *rest, act, emit_xn, precision):
    if emit_xn:
        o_ref, xn_ref, xs_ref = rest
    else:
        o_ref, xs_ref = rest

    @pl.when(pl.program_id(1) == 0)
    def _():
        x = x_ref[...]
        y = x * lax.rsqrt(jnp.mean(x * x, axis=-1, keepdims=True) + NORM_EPS) * g_ref[...]
        xs_ref[...] = y.astype(xs_ref.dtype)
        if emit_xn:
            xn_ref[...] = y

    acc = jnp.dot(xs_ref[...], w_ref[...], preferred_element_type=jnp.float32,
                  precision=precision)
    if act:
        acc = _gelu(acc)
    o_ref[...] = acc.astype(o_ref.dtype)


def norm_matmul(x, g, w, *, out_dtype, act=False, emit_xn=False, tm=512, tn=None,
                precision=None):
    T, D = x.shape
    N = w.shape[1]
    tn = tn or N
    tm = min(tm, T)
    out_shape = [jax.ShapeDtypeStruct((T, N), out_dtype)]
    out_specs = [pl.BlockSpec((tm, tn), lambda i, j: (i, j))]
    if emit_xn:
        out_shape.append(jax.ShapeDtypeStruct((T, D), jnp.float32))
        out_specs.append(pl.BlockSpec((tm, D), lambda i, j: (i, 0)))
    res = pl.pallas_call(
        functools.partial(_norm_matmul_kernel, act=act, emit_xn=emit_xn, precision=precision),
        grid=(T // tm, N // tn),
        in_specs=[pl.BlockSpec((tm, D), lambda i, j: (i, 0)),
                  pl.BlockSpec((1, D), lambda i, j: (0, 0)),
                  pl.BlockSpec((D, tn), lambda i, j: (0, j))],
        out_specs=out_specs,
        out_shape=out_shape,
        scratch_shapes=[pltpu.VMEM((tm, D), w.dtype)],
        compiler_params=_cparams(("parallel", "arbitrary")),
        name="norm_matmul",
    )(x, g.reshape(1, D), w)
    return res if emit_xn else res[0]


def _matmul_res_kernel(a_ref, w_ref, h_ref, o_ref):
    o_ref[...] = h_ref[...] + jnp.dot(a_ref[...], w_ref[...],
                                      preferred_element_type=jnp.float32)


def matmul_residual(a, w, h, *, tm=512):
    T, K = a.shape
    N = w.shape[1]
    tm = min(tm, T)
    return pl.pallas_call(
        _matmul_res_kernel,
        grid=(T // tm,),
        in_specs=[pl.BlockSpec((tm, K), lambda i: (i, 0)),
                  pl.BlockSpec((K, N), lambda i: (0, 0)),
                  pl.BlockSpec((tm, N), lambda i: (i, 0))],
        out_specs=pl.BlockSpec((tm, N), lambda i: (i, 0)),
        out_shape=jax.ShapeDtypeStruct((T, N), jnp.float32),
        compiler_params=_cparams(("parallel",)),
        name="matmul_residual",
    )(a, w, h)


def _t5_bucket(rel):
    nb = REL_BUCKETS // 2
    max_exact = nb // 2
    ret = jnp.where(rel > 0, nb, 0)
    n = jnp.abs(rel)
    nf = jnp.maximum(n, 1).astype(jnp.float32)
    large = max_exact + (jnp.log(nf / max_exact) / math.log(REL_MAX_DIST / max_exact)
                         * (nb - max_exact)).astype(jnp.int32)
    large = jnp.minimum(large, nb - 1)
    return (ret + jnp.where(n < max_exact, n, large)).astype(jnp.int32)


def _bias_kernel(bucket_ref, window_ref, table_ref, o_ref):
    bucket = bucket_ref[...]
    in_window = window_ref[...] > 0
    for hq in range(N_Q_HEADS):
        acc = jnp.zeros(bucket.shape, jnp.float32)
        for b in range(REL_BUCKETS):
            acc = jnp.where(bucket == b, table_ref[b, hq], acc)
        o_ref[hq] = jnp.where(in_window, acc, -jnp.inf)


def attn_bias(rel_table):
    qi = jnp.arange(BLOCK)[:, None]
    kj = jnp.arange(3 * BLOCK)[None, :]
    rel = kj - BLOCK - qi
    bucket = _t5_bucket(rel)
    window = (jnp.abs(rel) <= WINDOW).astype(jnp.int32)
    return pl.pallas_call(
        _bias_kernel,
        in_specs=[pl.BlockSpec(memory_space=pltpu.VMEM),
                  pl.BlockSpec(memory_space=pltpu.VMEM),
                  pl.BlockSpec(memory_space=pltpu.SMEM)],
        out_specs=pl.BlockSpec(memory_space=pltpu.VMEM),
        out_shape=jax.ShapeDtypeStruct((N_Q_HEADS, BLOCK, 3 * BLOCK), jnp.float32),
        name="attn_bias",
    )(bucket, window, rel_table)


def _attn_kernel(cur_ref, prev_ref, next_ref, bias_ref, sink_ref, o_ref):
    i = pl.program_id(1)
    nb = pl.num_programs(1)
    q = cur_ref[0, :, 0:Q_DIM]
    kband = jnp.concatenate([prev_ref[0, :, 0:KV_DIM], cur_ref[0, :, Q_DIM:Q_DIM + KV_DIM],
                             next_ref[0, :, 0:KV_DIM]], axis=0)
    vband = jnp.concatenate([prev_ref[0, :, KV_DIM:2 * KV_DIM], cur_ref[0, :, Q_DIM + KV_DIM:QKV_DIM],
                             next_ref[0, :, KV_DIM:2 * KV_DIM]], axis=0)
    col = lax.broadcasted_iota(jnp.int32, (1, 3 * BLOCK), 1)
    valid = jnp.logical_and(jnp.logical_or(col >= BLOCK, i > 0),
                            jnp.logical_or(col < 2 * BLOCK, i < nb - 1))
    outs = []
    for hk in range(N_KV_HEADS):
        kh = kband[:, hk * HEAD_DIM:(hk + 1) * HEAD_DIM]
        vh = vband[:, hk * HEAD_DIM:(hk + 1) * HEAD_DIM]
        for g in range(GQA_GROUP):
            hq = hk * GQA_GROUP + g
            qh = q[:, hq * HEAD_DIM:(hq + 1) * HEAD_DIM]
            s = lax.dot_general(qh, kh, (((1,), (1,)), ((), ())),
                                preferred_element_type=jnp.float32)
            s = s * (HEAD_DIM ** -0.5) + bias_ref[hq]
            s = jnp.where(valid, s, -jnp.inf)
            sk = sink_ref[hq]
            mx = jnp.maximum(jnp.max(s, axis=-1, keepdims=True), sk)
            p = jnp.exp(s - mx)
            denom = jnp.sum(p, axis=-1, keepdims=True) + jnp.exp(sk - mx)
            p = (p / denom).astype(vh.dtype)
            outs.append(jnp.dot(p, vh, preferred_element_type=jnp.float32))
    o_ref[0] = jnp.concatenate(outs, axis=-1).astype(o_ref.dtype)


def window_attention(qkv, bias, sink):
    B, S, _ = qkv.shape
    nb = S // BLOCK
    kv_col = Q_DIM // (2 * KV_DIM)
    return pl.pallas_call(
        _attn_kernel,
        grid=(B, nb),
        in_specs=[pl.BlockSpec((1, BLOCK, QKV_DIM), lambda b, i: (b, i, 0)),
                  pl.BlockSpec((1, BLOCK, 2 * KV_DIM),
                               lambda b, i: (b, jnp.maximum(i - 1, 0), kv_col)),
                  pl.BlockSpec((1, BLOCK, 2 * KV_DIM),
                               lambda b, i: (b, jnp.minimum(i + 1, nb - 1), kv_col)),
                  pl.BlockSpec((N_Q_HEADS, BLOCK, 3 * BLOCK), lambda b, i: (0, 0, 0)),
                  pl.BlockSpec(memory_space=pltpu.SMEM)],
        out_specs=pl.BlockSpec((1, BLOCK, Q_DIM), lambda b, i: (b, i, 0)),
        out_shape=jax.ShapeDtypeStruct((B, S, Q_DIM), jnp.bfloat16),
        compiler_params=_cparams(("parallel", "arbitrary")),
        name="window_attention",
    )(qkv, qkv, qkv, bias, sink)


def _spatial_gate_kernel(z_ref, g_ref, b_ref, wsp_ref, bsp_ref, o_ref):
    v = z_ref[:, D_GATE:2 * D_GATE].astype(jnp.float32)
    mu = jnp.mean(v, axis=-1, keepdims=True)
    vc = v - mu
    var = jnp.mean(vc * vc, axis=-1, keepdims=True)
    vn = (vc * lax.rsqrt(var + LN_EPS) * g_ref[...] + b_ref[...]).astype(jnp.bfloat16)
    for grp in range(N_SG_GROUPS):
        lo, hi = grp * SG_GROUP_DIM, (grp + 1) * SG_GROUP_DIM
        mixed = jnp.dot(wsp_ref[grp], vn[:, lo:hi], preferred_element_type=jnp.float32)
        mixed = mixed + bsp_ref[:, grp:grp + 1]
        u = z_ref[:, lo:hi].astype(jnp.float32)
        o_ref[:, lo:hi] = (u * mixed).astype(o_ref.dtype)


def spatial_gate(z, ln_g, ln_b, w_sp, b_sp):
    T = z.shape[0]
    return pl.pallas_call(
        _spatial_gate_kernel,
        grid=(T // CHUNK,),
        in_specs=[pl.BlockSpec((CHUNK, 2 * D_GATE), lambda i: (i, 0)),
                  pl.BlockSpec((1, D_GATE), lambda i: (0, 0)),
                  pl.BlockSpec((1, D_GATE), lambda i: (0, 0)),
                  pl.BlockSpec((N_SG_GROUPS, CHUNK, CHUNK), lambda i: (0, 0, 0)),
                  pl.BlockSpec((CHUNK, N_SG_GROUPS), lambda i: (0, 0))],
        out_specs=pl.BlockSpec((CHUNK, D_GATE), lambda i: (i, 0)),
        out_shape=jax.ShapeDtypeStruct((T, D_GATE), jnp.bfloat16),
        compiler_params=_cparams(("parallel",)),
        name="spatial_gate",
    )(z, ln_g.reshape(1, D_GATE), ln_b.reshape(1, D_GATE), w_sp.astype(jnp.bfloat16), b_sp.T)


def _take_top(s, ids, n):
    rows = lax.broadcasted_iota(jnp.int32, s.shape, 0)
    vals, picks = [], []
    for _ in range(n):
        m = jnp.max(s, axis=0, keepdims=True)
        am = jnp.min(jnp.where(s == m, rows, s.shape[0]), axis=0, keepdims=True)
        hit = rows == am
        vals.append(m)
        if ids is None:
            picks.append(am)
        else:
            picks.append(jnp.max(jnp.where(hit, ids, -1), axis=0, keepdims=True))
        s = jnp.where(hit, -jnp.inf, s)
    return jnp.concatenate(vals, axis=0), jnp.concatenate(picks, axis=0)


def _route_kernel(q_ref, keys_ref, idx_ref, gate_ref, *, precision):
    nt = (((1,), (1,)), ((), ()))
    q = q_ref[...]
    s1 = lax.dot_general(keys_ref[0, 0], q[:, 0:D_HALF], nt,
                         preferred_element_type=jnp.float32, precision=precision)
    s2 = lax.dot_general(keys_ref[0, 1], q[:, D_HALF:2 * D_HALF], nt,
                         preferred_element_type=jnp.float32, precision=precision)
    v1, i1 = _take_top(s1, None, PEER_TOPK)
    v2, i2 = _take_top(s2, None, PEER_TOPK)
    cand = jnp.concatenate([v1[a:a + 1] + v2 for a in range(PEER_TOPK)], axis=0)
    cidx = jnp.concatenate([i1[a:a + 1] * N_KEYS + i2 for a in range(PEER_TOPK)], axis=0)
    top_s, top_i = _take_top(cand, cidx, PEER_TOPK)
    e = jnp.exp(top_s - top_s[0:1])
    gate_ref[...] = e / jnp.sum(e, axis=0, keepdims=True)
    idx_ref[...] = top_i * ROW_SUB


def peer_route(q, sub_keys, *, tb=256, precision=None):
    T = q.shape[0]
    tb = min(tb, T)
    return pl.pallas_call(
        functools.partial(_route_kernel, precision=precision),
        grid=(T // tb, PEER_HEADS),
        in_specs=[pl.BlockSpec((tb, 2 * D_HALF), lambda i, h: (i, h)),
                  pl.BlockSpec((1, 2, N_KEYS, D_HALF), lambda i, h: (h, 0, 0, 0))],
        out_specs=[pl.BlockSpec((PEER_TOPK, tb), lambda i, h: (h, i)),
                   pl.BlockSpec((PEER_TOPK, tb), lambda i, h: (h, i))],
        out_shape=[jax.ShapeDtypeStruct((N_SLOTS, T), jnp.int32),
                   jax.ShapeDtypeStruct((N_SLOTS, T), jnp.float32)],
        compiler_params=_cparams(("parallel", "arbitrary")),
        name="peer_route",
    )(q, sub_keys)


def pack_table(tbl):
    E, D = tbl.shape
    bits = lax.bitcast_convert_type(tbl.astype(jnp.bfloat16), jnp.uint16).astype(jnp.uint32)
    packed = bits[:, :D // 2] | (bits[:, D // 2:] << 16)
    return packed.reshape(E * ROW_SUB, LANES)


def _unpack(words):
    lo = pltpu.bitcast(words << 16, jnp.float32)
    hi = pltpu.bitcast(words & jnp.uint32(0xFFFF0000), jnp.float32)
    return lo, hi


def _gather_rows(idx_ref, tbl_ref, t, planes_ref):
    for k in range(N_SLOTS):
        off = pl.multiple_of(idx_ref[t, k], ROW_SUB)
        planes_ref[pl.ds(k, ROW_SUB, stride=PLANE_STRIDE), :] = tbl_ref[pl.ds(off, ROW_SUB), :]


def _plane(planes_ref, c):
    return _unpack(planes_ref[c * PLANE_STRIDE:c * PLANE_STRIDE + N_SLOTS, :])


def _two_stage_tokens(tb, gather, compute, planes_a, planes_b, before_pair=None):
    gather(0, planes_a)

    def pair(i, carry):
        t0 = 2 * i
        if before_pair is not None:
            before_pair(t0)
        gather(t0 + 1, planes_b)
        compute(t0, planes_a, 0)
        gather(jnp.minimum(t0 + 2, tb - 1), planes_a)
        compute(t0 + 1, planes_b, 1)
        return carry

    lax.fori_loop(0, tb // 2, pair, 0)


def _peer_down_kernel(idx_ref, x_ref, gate_ref, tbl_ref, o_ref, planes_a, planes_b, part_ref):
    tb = x_ref.shape[0]
    lane = lax.broadcasted_iota(jnp.int32, (N_SLOTS, tb), 1)

    def compute(t, planes_ref, slot):
        acc = jnp.zeros((N_SLOTS, LANES), jnp.float32)
        for c in range(ROW_SUB):
            lo, hi = _plane(planes_ref, c)
            acc = acc + lo * x_ref[t, c:c + 1, :] + hi * x_ref[t, ROW_SUB + c:ROW_SUB + c + 1, :]
        part_ref[slot] = acc

    def place_pair(t0):
        col_a = jnp.sum(part_ref[0], axis=1, keepdims=True)
        col_b = jnp.sum(part_ref[1], axis=1, keepdims=True)
        o_ref[...] = jnp.where(lane == t0 - 2, col_a, jnp.where(lane == t0 - 1, col_b, o_ref[...]))

    o_ref[...] = jnp.zeros(o_ref.shape, jnp.float32)
    part_ref[...] = jnp.zeros(part_ref.shape, jnp.float32)
    _two_stage_tokens(tb, functools.partial(_gather_rows, idx_ref, tbl_ref), compute, planes_a, planes_b,
                      before_pair=place_pair)
    place_pair(tb)
    o_ref[...] = _gelu(o_ref[...]) * gate_ref[...]


def _expert_call(kernel_fn, T, tb, in_specs, out_spec, out_shape, name, extra_scratch=()):
    planes = pltpu.VMEM((ROW_SUB * PLANE_STRIDE, LANES), jnp.uint32)
    return pl.pallas_call(
        kernel_fn,
        grid=(T // tb,),
        in_specs=in_specs,
        out_specs=out_spec,
        out_shape=out_shape,
        scratch_shapes=[planes, planes, *extra_scratch],
        compiler_params=_cparams(("arbitrary",)),
        name=name,
    )


def peer_down(idx, xn3, gate_t, tbl, *, tb=128):
    T = xn3.shape[0]
    tb = min(tb, T)
    return _expert_call(
        _peer_down_kernel, T, tb,
        [pl.BlockSpec((tb, N_SLOTS), lambda i: (i, 0), memory_space=pltpu.SMEM),
         pl.BlockSpec((tb, SUBLANES, LANES), lambda i: (i, 0, 0)),
         pl.BlockSpec((N_SLOTS, tb), lambda i: (0, i)),
         pl.BlockSpec(tbl.shape, lambda i: (0, 0), pipeline_mode=pl.Buffered(1))],
        pl.BlockSpec((N_SLOTS, tb), lambda i: (0, i)),
        jax.ShapeDtypeStruct((N_SLOTS, T), jnp.float32),
        "peer_down",
        extra_scratch=[pltpu.VMEM((2, N_SLOTS, LANES), jnp.float32)],
    )(idx, xn3, gate_t, tbl)


def _peer_up_kernel(idx_ref, w_ref, h_ref, tbl_ref, o_ref, planes_a, planes_b):
    tb = h_ref.shape[0]
    lane = lax.broadcasted_iota(jnp.int32, (N_SLOTS, tb), 1)

    def compute(t, planes_ref, slot):
        col = jnp.sum(jnp.where(lane == t, w_ref[...], 0.0), axis=1, keepdims=True)
        wb = jnp.broadcast_to(col, (N_SLOTS, LANES))
        y_lo, y_hi = [], []
        for c in range(ROW_SUB):
            lo, hi = _plane(planes_ref, c)
            y_lo.append(jnp.sum(lo * wb, axis=0, keepdims=True))
            y_hi.append(jnp.sum(hi * wb, axis=0, keepdims=True))
        o_ref[t] = h_ref[t] + jnp.concatenate(y_lo + y_hi, axis=0)

    _two_stage_tokens(tb, functools.partial(_gather_rows, idx_ref, tbl_ref), compute, planes_a, planes_b)


def peer_up(idx, w_t, h3, tbl, *, tb=128):
    T = h3.shape[0]
    tb = min(tb, T)
    return _expert_call(
        _peer_up_kernel, T, tb,
        [pl.BlockSpec((tb, N_SLOTS), lambda i: (i, 0), memory_space=pltpu.SMEM),
         pl.BlockSpec((N_SLOTS, tb), lambda i: (0, i)),
         pl.BlockSpec((tb, SUBLANES, LANES), lambda i: (i, 0, 0)),
         pl.BlockSpec(tbl.shape, lambda i: (0, 0), pipeline_mode=pl.Buffered(1))],
        pl.BlockSpec((tb, SUBLANES, LANES), lambda i: (i, 0, 0)),
        jax.ShapeDtypeStruct(h3.shape, jnp.float32),
        "peer_up",
    )(idx, w_t, h3, tbl)


def peer_layer(h, g, w_query, sub_keys, down, up):
    T = h.shape[0]
    hi = lax.Precision.HIGHEST
    q, xn = norm_matmul(h, g, w_query, out_dtype=jnp.float32, emit_xn=True, tn=512, precision=hi)
    idx_t, gate_t = peer_route(q, sub_keys, precision=hi)
    idx = idx_t.T
    w_t = peer_down(idx, xn.reshape(T, SUBLANES, LANES), gate_t, pack_table(down))
    return peer_up(idx, w_t, h.reshape(T, SUBLANES, LANES), pack_table(up)).reshape(T, D_MODEL)


def _rms_kernel(x_ref, g_ref, o_ref):
    x = x_ref[...]
    o_ref[...] = x * lax.rsqrt(jnp.mean(x * x, axis=-1, keepdims=True) + NORM_EPS) * g_ref[...]


def rms_norm(x, g, *, tm=512):
    T, D = x.shape
    tm = min(tm, T)
    return pl.pallas_call(
        _rms_kernel,
        grid=(T // tm,),
        in_specs=[pl.BlockSpec((tm, D), lambda i: (i, 0)), pl.BlockSpec((1, D), lambda i: (0, 0))],
        out_specs=pl.BlockSpec((tm, D), lambda i: (i, 0)),
        out_shape=jax.ShapeDtypeStruct((T, D), jnp.float32),
        compiler_params=_cparams(("parallel",)),
        name="final_norm",
    )(x, g.reshape(1, D))


def kernel(x, rel_table, mix_norm_g, attn_w_in, attn_w_out, attn_sink, sg_w_in, sg_ln_g, sg_ln_b, sg_w_spatial, sg_b_spatial, sg_w_out, ffn_norm_g, peer_w_query, peer_sub_keys, peer_down, peer_up, final_norm_g):
    B, S, D = x.shape
    T = B * S
    bf16 = jnp.bfloat16
    h = x.reshape(T, D)

    qkv = norm_matmul(h, mix_norm_g[0], attn_w_in[0].astype(bf16), out_dtype=bf16)
    att = window_attention(qkv.reshape(B, S, QKV_DIM), attn_bias(rel_table), attn_sink[0])
    h = matmul_residual(att.reshape(T, Q_DIM), attn_w_out[0].astype(bf16), h)
    h = peer_layer(h, ffn_norm_g[0], peer_w_query[0], peer_sub_keys[0], peer_down[0], peer_up[0])

    z = norm_matmul(h, mix_norm_g[1], sg_w_in[0].astype(bf16), out_dtype=bf16, act=True, tn=1536)
    gated = spatial_gate(z, sg_ln_g[0], sg_ln_b[0], sg_w_spatial[0], sg_b_spatial[0])
    h = matmul_residual(gated, sg_w_out[0].astype(bf16), h)
    h = peer_layer(h, ffn_norm_g[1], peer_w_query[1], peer_sub_keys[1], peer_down[1], peer_up[1])

    return rms_norm(h, final_norm_g).reshape(B, S, D)
```

```python
import functools
import math

import numpy as np
import jax
import jax.numpy as jnp
from jax import lax
from jax.experimental import pallas as pl
from jax.experimental.pallas import tpu as pltpu

D_MODEL = 1024
HEAD_DIM = 64
N_Q_HEADS = 16
N_KV_HEADS = 4
GQA_GROUP = 4
WINDOW = 128
BLOCK = 128
REL_BUCKETS = 32
REL_MAX_DIST = 128
Q_DIM = N_Q_HEADS * HEAD_DIM
KV_DIM = N_KV_HEADS * HEAD_DIM
QKV_DIM = Q_DIM + 2 * KV_DIM
CHUNK = 128
D_GATE = 3072
N_SG_GROUPS = 8
SG_GROUP_DIM = D_GATE // N_SG_GROUPS
N_KEYS = 128
PEER_HEADS = 8
PEER_TOPK = 16
D_HALF = 128
N_SLOTS = PEER_HEADS * PEER_TOPK
NORM_EPS = 1e-6
LN_EPS = 1e-5

LANES = 128
SUBLANES = 8
ROW_WORDS = D_MODEL // 2
ROW_SUB = ROW_WORDS // LANES
PLANE_STRIDE = 136
VMEM_LIMIT = 56 * 1024 * 1024

_GELU_C = math.sqrt(2.0 / math.pi)


def _gelu(x):
    return 0.5 * x * (1.0 + jnp.tanh(_GELU_C * (x + 0.044715 * (x * x * x))))


def _cparams(sem):
    return pltpu.CompilerParams(dimension_semantics=sem, vmem_limit_bytes=VMEM_LIMIT)


def _norm_matmul_kernel(x_ref, g_ref, w_ref, *rest, act, emit_xn, precision):
    if emit_xn:
        o_ref, xn_ref, xs_ref = rest
    else:
        o_ref, xs_ref = rest

    @pl.when(pl.program_id(1) == 0)
    def _():
        x = x_ref[...]
        y = x * lax.rsqrt(jnp.mean(x * x, axis=-1, keepdims=True) + NORM_EPS) * g_ref[...]
        xs_ref[...] = y.astype(xs_ref.dtype)
        if emit_xn:
            xn_ref[...] = y

    acc = jnp.dot(xs_ref[...], w_ref[...], preferred_element_type=jnp.float32,
                  precision=precision)
    if act:
        acc = _gelu(acc)
    o_ref[...] = acc.astype(o_ref.dtype)


def norm_matmul(x, g, w, *, out_dtype, act=False, emit_xn=False, tm=512, tn=None,
                precision=None):
    T, D = x.shape
    N = w.shape[1]
    tn = tn or N
    tm = min(tm, T)
    out_shape = [jax.ShapeDtypeStruct((T, N), out_dtype)]
    out_specs = [pl.BlockSpec((tm, tn), lambda i, j: (i, j))]
    if emit_xn:
        out_shape.append(jax.ShapeDtypeStruct((T, D), jnp.float32))
        out_specs.append(pl.BlockSpec((tm, D), lambda i, j: (i, 0)))
    res = pl.pallas_call(
        functools.partial(_norm_matmul_kernel, act=act, emit_xn=emit_xn, precision=precision),
        grid=(T // tm, N // tn),
        in_specs=[pl.BlockSpec((tm, D), lambda i, j: (i, 0)),
                  pl.BlockSpec((1, D), lambda i, j: (0, 0)),
                  pl.BlockSpec((D, tn), lambda i, j: (0, j))],
        out_specs=out_specs,
        out_shape=out_shape,
        scratch_shapes=[pltpu.VMEM((tm, D), w.dtype)],
        compiler_params=_cparams(("parallel", "arbitrary")),
        name="norm_matmul",
    )(x, g.reshape(1, D), w)
    return res if emit_xn else res[0]


def _matmul_res_kernel(a_ref, w_ref, h_ref, o_ref):
    o_ref[...] = h_ref[...] + jnp.dot(a_ref[...], w_ref[...],
                                      preferred_element_type=jnp.float32)


def matmul_residual(a, w, h, *, tm=512):
    T, K = a.shape
    N = w.shape[1]
    tm = min(tm, T)
    return pl.pallas_call(
        _matmul_res_kernel,
        grid=(T // tm,),
        in_specs=[pl.BlockSpec((tm, K), lambda i: (i, 0)),
                  pl.BlockSpec((K, N), lambda i: (0, 0)),
                  pl.BlockSpec((tm, N), lambda i: (i, 0))],
        out_specs=pl.BlockSpec((tm, N), lambda i: (i, 0)),
        out_shape=jax.ShapeDtypeStruct((T, N), jnp.float32),
        compiler_params=_cparams(("parallel",)),
        name="matmul_residual",
    )(a, w, h)


def _t5_bucket(rel):
    nb = REL_BUCKETS // 2
    max_exact = nb // 2
    ret = jnp.where(rel > 0, nb, 0)
    n = jnp.abs(rel)
    nf = jnp.maximum(n, 1).astype(jnp.float32)
    large = max_exact + (jnp.log(nf / max_exact) / math.log(REL_MAX_DIST / max_exact)
                         * (nb - max_exact)).astype(jnp.int32)
    large = jnp.minimum(large, nb - 1)
    return (ret + jnp.where(n < max_exact, n, large)).astype(jnp.int32)


def _bias_kernel(bucket_ref, window_ref, table_ref, o_ref):
    bucket = bucket_ref[...]
    in_window = window_ref[...] > 0
    for hq in range(N_Q_HEADS):
        acc = jnp.zeros(bucket.shape, jnp.float32)
        for b in range(REL_BUCKETS):
            acc = jnp.where(bucket == b, table_ref[b, hq], acc)
        o_ref[hq] = jnp.where(in_window, acc, -jnp.inf)


def attn_bias(rel_table):
    qi = jnp.arange(BLOCK)[:, None]
    kj = jnp.arange(3 * BLOCK)[None, :]
    rel = kj - BLOCK - qi
    bucket = _t5_bucket(rel)
    window = (jnp.abs(rel) <= WINDOW).astype(jnp.int32)
    return pl.pallas_call(
        _bias_kernel,
        in_specs=[pl.BlockSpec(memory_space=pltpu.VMEM),
                  pl.BlockSpec(memory_space=pltpu.VMEM),
                  pl.BlockSpec(memory_space=pltpu.SMEM)],
        out_specs=pl.BlockSpec(memory_space=pltpu.VMEM),
        out_shape=jax.ShapeDtypeStruct((N_Q_HEADS, BLOCK, 3 * BLOCK), jnp.float32),
        name="attn_bias",
    )(bucket, window, rel_table)


def _attn_kernel(cur_ref, prev_ref, next_ref, bias_ref, sink_ref, o_ref):
    i = pl.program_id(1)
    nb = pl.num_programs(1)
    q = cur_ref[0, :, 0:Q_DIM]
    kband = jnp.concatenate([prev_ref[0, :, 0:KV_DIM], cur_ref[0, :, Q_DIM:Q_DIM + KV_DIM],
                             next_ref[0, :, 0:KV_DIM]], axis=0)
    vband = jnp.concatenate([prev_ref[0, :, KV_DIM:2 * KV_DIM], cur_ref[0, :, Q_DIM + KV_DIM:QKV_DIM],
                             next_ref[0, :, KV_DIM:2 * KV_DIM]], axis=0)
    col = lax.broadcasted_iota(jnp.int32, (1, 3 * BLOCK), 1)
    valid = jnp.logical_and(jnp.logical_or(col >= BLOCK, i > 0),
                            jnp.logical_or(col < 2 * BLOCK, i < nb - 1))
    outs = []
    for hk in range(N_KV_HEADS):
        kh = kband[:, hk * HEAD_DIM:(hk + 1) * HEAD_DIM]
        vh = vband[:, hk * HEAD_DIM:(hk + 1) * HEAD_DIM]
        for g in range(GQA_GROUP):
            hq = hk * GQA_GROUP + g
            qh = q[:, hq * HEAD_DIM:(hq + 1) * HEAD_DIM]
            s = lax.dot_general(qh, kh, (((1,), (1,)), ((), ())),
                                preferred_element_type=jnp.float32)
            s = s * (HEAD_DIM ** -0.5) + bias_ref[hq]
            s = jnp.where(valid, s, -jnp.inf)
            sk = sink_ref[hq]
            mx = jnp.maximum(jnp.max(s, axis=-1, keepdims=True), sk)
            p = jnp.exp(s - mx)
            denom = jnp.sum(p, axis=-1, keepdims=True) + jnp.exp(sk - mx)
            p = (p / denom).astype(vh.dtype)
            outs.append(jnp.dot(p, vh, preferred_element_type=jnp.float32))
    o_ref[0] = jnp.concatenate(outs, axis=-1).astype(o_ref.dtype)


def window_attention(qkv, bias, sink):
    B, S, _ = qkv.shape
    nb = S // BLOCK
    kv_col = Q_DIM // (2 * KV_DIM)
    return pl.pallas_call(
        _attn_kernel,
        grid=(B, nb),
        in_specs=[pl.BlockSpec((1, BLOCK, QKV_DIM), lambda b, i: (b, i, 0)),
                  pl.BlockSpec((1, BLOCK, 2 * KV_DIM),
                               lambda b, i: (b, jnp.maximum(i - 1, 0), kv_col)),
                  pl.BlockSpec((1, BLOCK, 2 * KV_DIM),
                               lambda b, i: (b, jnp.minimum(i + 1, nb - 1), kv_col)),
                  pl.BlockSpec((N_Q_HEADS, BLOCK, 3 * BLOCK), lambda b, i: (0, 0, 0)),
                  pl.BlockSpec(memory_space=pltpu.SMEM)],
        out_specs=pl.BlockSpec((1, BLOCK, Q_DIM), lambda b, i: (b, i, 0)),
        out_shape=jax.ShapeDtypeStruct((B, S, Q_DIM), jnp.bfloat16),
        compiler_params=_cparams(("parallel", "arbitrary")),
        name="window_attention",
    )(qkv, qkv, qkv, bias, sink)


def _spatial_gate_kernel(z_ref, g_ref, b_ref, wsp_ref, bsp_ref, o_ref):
    v = z_ref[:, D_GATE:2 * D_GATE].astype(jnp.float32)
    mu = jnp.mean(v, axis=-1, keepdims=True)
    vc = v - mu
    var = jnp.mean(vc * vc, axis=-1, keepdims=True)
    vn = (vc * lax.rsqrt(var + LN_EPS) * g_ref[...] + b_ref[...]).astype(jnp.bfloat16)
    for grp in range(N_SG_GROUPS):
        lo, hi = grp * SG_GROUP_DIM, (grp + 1) * SG_GROUP_DIM
        mixed = jnp.dot(wsp_ref[grp], vn[:, lo:hi], preferred_element_type=jnp.float32)
        mixed = mixed + bsp_ref[:, grp:grp + 1]
        u = z_ref[:, lo:hi].astype(jnp.float32)
        o_ref[:, lo:hi] = (u * mixed).astype(o_ref.dtype)


def spatial_gate(z, ln_g, ln_b, w_sp, b_sp):
    T = z.shape[0]
    return pl.pallas_call(
        _spatial_gate_kernel,
        grid=(T // CHUNK,),
        in_specs=[pl.BlockSpec((CHUNK, 2 * D_GATE), lambda i: (i, 0)),
                  pl.BlockSpec((1, D_GATE), lambda i: (0, 0)),
                  pl.BlockSpec((1, D_GATE), lambda i: (0, 0)),
                  pl.BlockSpec((N_SG_GROUPS, CHUNK, CHUNK), lambda i: (0, 0, 0)),
                  pl.BlockSpec((CHUNK, N_SG_GROUPS), lambda i: (0, 0))],
        out_specs=pl.BlockSpec((CHUNK, D_GATE), lambda i: (i, 0)),
        out_shape=jax.ShapeDtypeStruct((T, D_GATE), jnp.bfloat16),
        compiler_params=_cparams(("parallel",)),
        name="spatial_gate",
    )(z, ln_g.reshape(1, D_GATE), ln_b.reshape(1, D_GATE), w_sp.astype(jnp.bfloat16), b_sp.T)


def _oddeven_merge_sort_pairs(n):
    pairs = []
    p = 1
    while p < n:
        k = p
        while k >= 1:
            for j in range(k % p, n - k, 2 * k):
                for i in range(min(k, n - j - k)):
                    if (i + j) // (2 * p) == (i + j + k) // (2 * p):
                        pairs.append((i + j, i + j + k))
            k //= 2
        p *= 2
    return pairs


_SORT16 = _oddeven_merge_sort_pairs(N_KEYS // SUBLANES)


def _top16_of_keys(s):
    nv = N_KEYS // SUBLANES
    L = s.shape[1]
    sub = lax.broadcasted_iota(jnp.int32, (SUBLANES, L), 0)
    v = [s[j * SUBLANES:(j + 1) * SUBLANES] for j in range(nv)]
    ids = [sub + j * SUBLANES for j in range(nv)]
    for i, j in _SORT16:
        swap = jnp.logical_or(v[j] > v[i], jnp.logical_and(v[j] == v[i], ids[j] < ids[i]))
        v[i], v[j] = jnp.where(swap, v[j], v[i]), jnp.where(swap, v[i], v[j])
        ids[i], ids[j] = jnp.where(swap, ids[j], ids[i]), jnp.where(swap, ids[i], ids[j])
    vals, picks = [], []
    for it in range(PEER_TOPK):
        m = jnp.max(v[0], axis=0, keepdims=True)
        am = jnp.min(jnp.where(v[0] == m, ids[0], N_KEYS), axis=0, keepdims=True)
        hit = ids[0] == am
        vals.append(m)
        picks.append(am)
        last = PEER_TOPK - 1 - it
        for j in range(last):
            v[j] = jnp.where(hit, v[j + 1], v[j])
            ids[j] = jnp.where(hit, ids[j + 1], ids[j])
        v[last] = jnp.where(hit, -jnp.inf, v[last])
    return jnp.concatenate(vals, axis=0), jnp.concatenate(picks, axis=0)


_CAND_ROWS = (
    [(a, 0) for a in range(16)]
    + [None] + [(0, b) for b in range(1, 16)]
    + [None] + [(a, 1) for a in range(1, 8)]
    + [None, None] + [(1, b) for b in range(2, 8)]
    + [(2, 2), (3, 2), (4, 2), (2, 3), (2, 4), (3, 3), None, None]
)
assert sorted(x for x in _CAND_ROWS if x) == sorted(
    (a, b) for a in range(16) for b in range(16) if (a + 1) * (b + 1) <= 16)


def _rows(x, picks):
    pieces, i = [], 0
    while i < len(picks):
        j = i
        while j + 1 < len(picks) and picks[j + 1] == picks[j] + 1:
            j += 1
        pieces.append(x[picks[i]:picks[j] + 1])
        i = j + 1
    return pieces[0] if len(pieces) == 1 else jnp.concatenate(pieces, axis=0)


def _cand_positions(L):
    row = lax.broadcasted_iota(jnp.int32, (len(_CAND_ROWS), L), 0)
    pos = jnp.full(row.shape, _PAD_POS, jnp.int32)
    for r, c in enumerate(_CAND_ROWS):
        if c is not None:
            pos = jnp.where(row == r, c[0] * PEER_TOPK + c[1], pos)
    return pos


_PAD_POS = 1 << 20


def _joint_top16(v1, i1, v2, i2, pos):
    a_of = [c[0] if c else 0 for c in _CAND_ROWS]
    b_of = [c[1] if c else 0 for c in _CAND_ROWS]
    cand = jnp.where(pos < _PAD_POS, _rows(v1, a_of) + _rows(v2, b_of), -jnp.inf)
    cidx = _rows(i1, a_of) * N_KEYS + _rows(i2, b_of)
    vals, picks = [], []
    for _ in range(PEER_TOPK):
        m = jnp.max(cand, axis=0, keepdims=True)
        pm = jnp.min(jnp.where(cand == m, pos, 1 << 21), axis=0, keepdims=True)
        hit = pos == pm
        vals.append(m)
        picks.append(jnp.max(jnp.where(hit, cidx, -1), axis=0, keepdims=True))
        cand = jnp.where(hit, -jnp.inf, cand)
    return jnp.concatenate(vals, axis=0), jnp.concatenate(picks, axis=0)


def _route_kernel(q_ref, keys_ref, idx_ref, gate_ref, *, precision):
    nt = (((1,), (1,)), ((), ()))
    pos = _cand_positions(LANES)
    for j in range(q_ref.shape[0] // LANES):
        q = q_ref[j * LANES:(j + 1) * LANES, :]
        s1 = lax.dot_general(keys_ref[0, 0], q[:, 0:D_HALF], nt,
                             preferred_element_type=jnp.float32, precision=precision)
        s2 = lax.dot_general(keys_ref[0, 1], q[:, D_HALF:2 * D_HALF], nt,
                             preferred_element_type=jnp.float32, precision=precision)
        v1, i1 = _top16_of_keys(s1)
        v2, i2 = _top16_of_keys(s2)
        top_s, top_i = _joint_top16(v1, i1, v2, i2, pos)
        e = jnp.exp(top_s - top_s[0:1])
        gate_ref[:, j * LANES:(j + 1) * LANES] = e / jnp.sum(e, axis=0, keepdims=True)
        idx_ref[:, j * LANES:(j + 1) * LANES] = top_i * ROW_SUB


def peer_route(q, sub_keys, *, tb=512, precision=None):
    T = q.shape[0]
    tb = min(tb, T)
    return pl.pallas_call(
        functools.partial(_route_kernel, precision=precision),
        grid=(T // tb, PEER_HEADS),
        in_specs=[pl.BlockSpec((tb, 2 * D_HALF), lambda i, h: (i, h)),
                  pl.BlockSpec((1, 2, N_KEYS, D_HALF), lambda i, h: (h, 0, 0, 0))],
        out_specs=[pl.BlockSpec((PEER_TOPK, tb), lambda i, h: (h, i)),
                   pl.BlockSpec((PEER_TOPK, tb), lambda i, h: (h, i))],
        out_shape=[jax.ShapeDtypeStruct((N_SLOTS, T), jnp.int32),
                   jax.ShapeDtypeStruct((N_SLOTS, T), jnp.float32)],
        compiler_params=_cparams(("parallel", "arbitrary")),
        name="peer_route",
    )(q, sub_keys)


def pack_table(tbl):
    E, D = tbl.shape
    bits = lax.bitcast_convert_type(tbl.astype(jnp.bfloat16), jnp.uint16).astype(jnp.uint32)
    packed = bits[:, :D // 2] | (bits[:, D // 2:] << 16)
    return packed.reshape(E * ROW_SUB, LANES)


def _unpack(words):
    lo = pltpu.bitcast(words << 16, jnp.float32)
    hi = pltpu.bitcast(words & jnp.uint32(0xFFFF0000), jnp.float32)
    return lo, hi


def _gather_rows(idx_ref, tbl_ref, t, planes_ref):
    for k in range(N_SLOTS):
        off = pl.multiple_of(idx_ref[t, k], ROW_SUB)
        planes_ref[pl.ds(k, ROW_SUB, stride=PLANE_STRIDE), :] = tbl_ref[pl.ds(off, ROW_SUB), :]


def _plane(planes_ref, c):
    return _unpack(planes_ref[c * PLANE_STRIDE:c * PLANE_STRIDE + N_SLOTS, :])


def _two_stage_tokens(tb, gather, compute, planes_a, planes_b, before_pair=None):
    gather(0, planes_a)

    def pair(i, carry):
        t0 = 2 * i
        if before_pair is not None:
            before_pair(t0)
        gather(t0 + 1, planes_b)
        compute(t0, planes_a, 0)
        gather(jnp.minimum(t0 + 2, tb - 1), planes_a)
        compute(t0 + 1, planes_b, 1)
        return carry

    lax.fori_loop(0, tb // 2, pair, 0)


def _peer_down_kernel(idx_ref, x_ref, gate_ref, tbl_ref, o_ref, planes_a, planes_b, part_ref):
    tb = x_ref.shape[0]
    lane = lax.broadcasted_iota(jnp.int32, (N_SLOTS, tb), 1)

    def compute(t, planes_ref, slot):
        acc = jnp.zeros((N_SLOTS, LANES), jnp.float32)
        for c in range(ROW_SUB):
            lo, hi = _plane(planes_ref, c)
            acc = acc + lo * x_ref[t, c:c + 1, :] + hi * x_ref[t, ROW_SUB + c:ROW_SUB + c + 1, :]
        part_ref[slot] = acc

    def place_pair(t0):
        col_a = jnp.sum(part_ref[0], axis=1, keepdims=True)
        col_b = jnp.sum(part_ref[1], axis=1, keepdims=True)
        o_ref[...] = jnp.where(lane == t0 - 2, col_a, jnp.where(lane == t0 - 1, col_b, o_ref[...]))

    o_ref[...] = jnp.zeros(o_ref.shape, jnp.float32)
    part_ref[...] = jnp.zeros(part_ref.shape, jnp.float32)
    _two_stage_tokens(tb, functools.partial(_gather_rows, idx_ref, tbl_ref), compute, planes_a, planes_b,
                      before_pair=place_pair)
    place_pair(tb)
    o_ref[...] = _gelu(o_ref[...]) * gate_ref[...]


def _expert_call(kernel_fn, T, tb, in_specs, out_spec, out_shape, name, extra_scratch=()):
    planes = pltpu.VMEM((ROW_SUB * PLANE_STRIDE, LANES), jnp.uint32)
    return pl.pallas_call(
        kernel_fn,
        grid=(T // tb,),
        in_specs=in_specs,
        out_specs=out_spec,
        out_shape=out_shape,
        scratch_shapes=[planes, planes, *extra_scratch],
        compiler_params=_cparams(("arbitrary",)),
        name=name,
    )


def peer_down(idx, xn3, gate_t, tbl, *, tb=128):
    T = xn3.shape[0]
    tb = min(tb, T)
    return _expert_call(
        _peer_down_kernel, T, tb,
        [pl.BlockSpec((tb, N_SLOTS), lambda i: (i, 0), memory_space=pltpu.SMEM),
         pl.BlockSpec((tb, SUBLANES, LANES), lambda i: (i, 0, 0)),
         pl.BlockSpec((N_SLOTS, tb), lambda i: (0, i)),
         pl.BlockSpec(tbl.shape, lambda i: (0, 0), pipeline_mode=pl.Buffered(1))],
        pl.BlockSpec((N_SLOTS, tb), lambda i: (0, i)),
        jax.ShapeDtypeStruct((N_SLOTS, T), jnp.float32),
        "peer_down",
        extra_scratch=[pltpu.VMEM((2, N_SLOTS, LANES), jnp.float32)],
    )(idx, xn3, gate_t, tbl)


def _peer_up_kernel(idx_ref, w_ref, h_ref, tbl_ref, o_ref, planes_a, planes_b):
    tb = h_ref.shape[0]
    lane = lax.broadcasted_iota(jnp.int32, (N_SLOTS, tb), 1)

    def compute(t, planes_ref, slot):
        col = jnp.sum(jnp.where(lane == t, w_ref[...], 0.0), axis=1, keepdims=True)
        wb = jnp.broadcast_to(col, (N_SLOTS, LANES))
        y_lo, y_hi = [], []
        for c in range(ROW_SUB):
            lo, hi = _plane(planes_ref, c)
            y_lo.append(jnp.sum(lo * wb, axis=0, keepdims=True))
            y_hi.append(jnp.sum(hi * wb, axis=0, keepdims=True))
        o_ref[t] = h_ref[t] + jnp.concatenate(y_lo + y_hi, axis=0)

    _two_stage_tokens(tb, functools.partial(_gather_rows, idx_ref, tbl_ref), compute, planes_a, planes_b)


def peer_up(idx, w_t, h3, tbl, *, tb=128):
    T = h3.shape[0]
    tb = min(tb, T)
    return _expert_call(
        _peer_up_kernel, T, tb,
        [pl.BlockSpec((tb, N_SLOTS), lambda i: (i, 0), memory_space=pltpu.SMEM),
         pl.BlockSpec((N_SLOTS, tb), lambda i: (0, i)),
         pl.BlockSpec((tb, SUBLANES, LANES), lambda i: (i, 0, 0)),
         pl.BlockSpec(tbl.shape, lambda i: (0, 0), pipeline_mode=pl.Buffered(1))],
        pl.BlockSpec((tb, SUBLANES, LANES), lambda i: (i, 0, 0)),
        jax.ShapeDtypeStruct(h3.shape, jnp.float32),
        "peer_up",
    )(idx, w_t, h3, tbl)


def peer_layer(h, g, w_query, sub_keys, down, up):
    T = h.shape[0]
    hi = lax.Precision.HIGHEST
    q, xn = norm_matmul(h, g, w_query, out_dtype=jnp.float32, emit_xn=True, tn=512, precision=hi)
    idx_t, gate_t = peer_route(q, sub_keys, precision=hi)
    idx = idx_t.T
    w_t = peer_down(idx, xn.reshape(T, SUBLANES, LANES), gate_t, pack_table(down))
    return peer_up(idx, w_t, h.reshape(T, SUBLANES, LANES), pack_table(up)).reshape(T, D_MODEL)


def _rms_kernel(x_ref, g_ref, o_ref):
    x = x_ref[...]
    o_ref[...] = x * lax.rsqrt(jnp.mean(x * x, axis=-1, keepdims=True) + NORM_EPS) * g_ref[...]


def rms_norm(x, g, *, tm=512):
    T, D = x.shape
    tm = min(tm, T)
    return pl.pallas_call(
        _rms_kernel,
        grid=(T // tm,),
        in_specs=[pl.BlockSpec((tm, D), lambda i: (i, 0)), pl.BlockSpec((1, D), lambda i: (0, 0))],
        out_specs=pl.BlockSpec((tm, D), lambda i: (i, 0)),
        out_shape=jax.ShapeDtypeStruct((T, D), jnp.float32),
        compiler_params=_cparams(("parallel",)),
        name="final_norm",
    )(x, g.reshape(1, D))


def kernel(x, rel_table, mix_norm_g, attn_w_in, attn_w_out, attn_sink, sg_w_in, sg_ln_g, sg_ln_b, sg_w_spatial, sg_b_spatial, sg_w_out, ffn_norm_g, peer_w_query, peer_sub_keys, peer_down, peer_up, final_norm_g):
    B, S, D = x.shape
    T = B * S
    bf16 = jnp.bfloat16
    h = x.reshape(T, D)

    qkv = norm_matmul(h, mix_norm_g[0], attn_w_in[0].astype(bf16), out_dtype=bf16)
    att = window_attention(qkv.reshape(B, S, QKV_DIM), attn_bias(rel_table), attn_sink[0])
    h = matmul_residual(att.reshape(T, Q_DIM), attn_w_out[0].astype(bf16), h)
    h = peer_layer(h, ffn_norm_g[0], peer_w_query[0], peer_sub_keys[0], peer_down[0], peer_up[0])

    z = norm_matmul(h, mix_norm_g[1], sg_w_in[0].astype(bf16), out_dtype=bf16, act=True, tn=1536)
    gated = spatial_gate(z, sg_ln_g[0], sg_ln_b[0], sg_w_spatial[0], sg_b_spatial[0])
    h = matmul_residual(gated, sg_w_out[0].astype(bf16), h)
    h = peer_layer(h, ffn_norm_g[1], peer_w_query[1], peer_sub_keys[1], peer_down[1], peer_up[1])

    return rms_norm(h, final_norm_g).reshape(B, S, D)
```

```python
import functools
import math

import numpy as np
import jax
import jax.numpy as jnp
from jax import lax
from jax.experimental import pallas as pl
from jax.experimental.pallas import tpu as pltpu

D_MODEL = 1024
HEAD_DIM = 64
N_Q_HEADS = 16
N_KV_HEADS = 4
GQA_GROUP = 4
WINDOW = 128
BLOCK = 128
REL_BUCKETS = 32
REL_MAX_DIST = 128
Q_DIM = N_Q_HEADS * HEAD_DIM
KV_DIM = N_KV_HEADS * HEAD_DIM
QKV_DIM = Q_DIM + 2 * KV_DIM
CHUNK = 128
D_GATE = 3072
N_SG_GROUPS = 8
SG_GROUP_DIM = D_GATE // N_SG_GROUPS
N_KEYS = 128
PEER_HEADS = 8
PEER_TOPK = 16
D_HALF = 128
N_SLOTS = PEER_HEADS * PEER_TOPK
NORM_EPS = 1e-6
LN_EPS = 1e-5

LANES = 128
SUBLANES = 8
ROW_WORDS = D_MODEL // 2
ROW_SUB = ROW_WORDS // LANES
PLANE_STRIDE = 136
VMEM_LIMIT = 56 * 1024 * 1024

_GELU_C = math.sqrt(2.0 / math.pi)


def _gelu(x):
    return 0.5 * x * (1.0 + jnp.tanh(_GELU_C * (x + 0.044715 * (x * x * x))))


def _cparams(sem):
    return pltpu.CompilerParams(dimension_semantics=sem, vmem_limit_bytes=VMEM_LIMIT)


def _norm_matmul_kernel(x_ref, g_ref, w_ref, *rest, act, emit_xn, precision):
    if emit_xn:
        o_ref, xn_ref, xs_ref = rest
    else:
        o_ref, xs_ref = rest

    @pl.when(pl.program_id(1) == 0)
    def _():
        x = x_ref[...]
        y = x * lax.rsqrt(jnp.mean(x * x, axis=-1, keepdims=True) + NORM_EPS) * g_ref[...]
        xs_ref[...] = y.astype(xs_ref.dtype)
        if emit_xn:
            xn_ref[...] = y

    acc = jnp.dot(xs_ref[...], w_ref[...], preferred_element_type=jnp.float32,
                  precision=precision)
    if act:
        acc = _gelu(acc)
    o_ref[...] = acc.astype(o_ref.dtype)


def norm_matmul(x, g, w, *, out_dtype, act=False, emit_xn=False, tm=512, tn=None,
                precision=None):
    T, D = x.shape
    N = w.shape[1]
    tn = tn or N
    tm = min(tm, T)
    out_shape = [jax.ShapeDtypeStruct((T, N), out_dtype)]
    out_specs = [pl.BlockSpec((tm, tn), lambda i, j: (i, j))]
    if emit_xn:
        out_shape.append(jax.ShapeDtypeStruct((T, D), jnp.float32))
        out_specs.append(pl.BlockSpec((tm, D), lambda i, j: (i, 0)))
    res = pl.pallas_call(
        functools.partial(_norm_matmul_kernel, act=act, emit_xn=emit_xn, precision=precision),
        grid=(T // tm, N // tn),
        in_specs=[pl.BlockSpec((tm, D), lambda i, j: (i, 0)),
                  pl.BlockSpec((1, D), lambda i, j: (0, 0)),
                  pl.BlockSpec((D, tn), lambda i, j: (0, j))],
        out_specs=out_specs,
        out_shape=out_shape,
        scratch_shapes=[pltpu.VMEM((tm, D), w.dtype)],
        compiler_params=_cparams(("parallel", "arbitrary")),
        name="norm_matmul",
    )(x, g.reshape(1, D), w)
    return res if emit_xn else res[0]


def _matmul_res_kernel(a_ref, w_ref, h_ref, o_ref):
    o_ref[...] = h_ref[...] + jnp.dot(a_ref[...], w_ref[...],
                                      preferred_element_type=jnp.float32)


def matmul_residual(a, w, h, *, tm=512):
    T, K = a.shape
    N = w.shape[1]
    tm = min(tm, T)
    return pl.pallas_call(
        _matmul_res_kernel,
        grid=(T // tm,),
        in_specs=[pl.BlockSpec((tm, K), lambda i: (i, 0)),
                  pl.BlockSpec((K, N), lambda i: (0, 0)),
                  pl.BlockSpec((tm, N), lambda i: (i, 0))],
        out_specs=pl.BlockSpec((tm, N), lambda i: (i, 0)),
        out_shape=jax.ShapeDtypeStruct((T, N), jnp.float32),
        compiler_params=_cparams(("parallel",)),
        name="matmul_residual",
    )(a, w, h)


def _t5_bucket(rel):
    nb = REL_BUCKETS // 2
    max_exact = nb // 2
    ret = jnp.where(rel > 0, nb, 0)
    n = jnp.abs(rel)
    nf = jnp.maximum(n, 1).astype(jnp.float32)
    large = max_exact + (jnp.log(nf / max_exact) / math.log(REL_MAX_DIST / max_exact)
                         * (nb - max_exact)).astype(jnp.int32)
    large = jnp.minimum(large, nb - 1)
    return (ret + jnp.where(n < max_exact, n, large)).astype(jnp.int32)


def _bias_kernel(bucket_ref, window_ref, table_ref, o_ref):
    bucket = bucket_ref[...]
    in_window = window_ref[...] > 0
    for hq in range(N_Q_HEADS):
        acc = jnp.zeros(bucket.shape, jnp.float32)
        for b in range(REL_BUCKETS):
            acc = jnp.where(bucket == b, table_ref[b, hq], acc)
        o_ref[hq] = jnp.where(in_window, acc, -jnp.inf)


def attn_bias(rel_table):
    qi = jnp.arange(BLOCK)[:, None]
    kj = jnp.arange(3 * BLOCK)[None, :]
    rel = kj - BLOCK - qi
    bucket = _t5_bucket(rel)
    window = (jnp.abs(rel) <= WINDOW).astype(jnp.int32)
    return pl.pallas_call(
        _bias_kernel,
        in_specs=[pl.BlockSpec(memory_space=pltpu.VMEM),
                  pl.BlockSpec(memory_space=pltpu.VMEM),
                  pl.BlockSpec(memory_space=pltpu.SMEM)],
        out_specs=pl.BlockSpec(memory_space=pltpu.VMEM),
        out_shape=jax.ShapeDtypeStruct((N_Q_HEADS, BLOCK, 3 * BLOCK), jnp.float32),
        name="attn_bias",
    )(bucket, window, rel_table)


def _attn_kernel(cur_ref, prev_ref, next_ref, bias_ref, sink_ref, o_ref):
    i = pl.program_id(1)
    nb = pl.num_programs(1)
    q = cur_ref[0, :, 0:Q_DIM]
    kband = jnp.concatenate([prev_ref[0, :, 0:KV_DIM], cur_ref[0, :, Q_DIM:Q_DIM + KV_DIM],
                             next_ref[0, :, 0:KV_DIM]], axis=0)
    vband = jnp.concatenate([prev_ref[0, :, KV_DIM:2 * KV_DIM], cur_ref[0, :, Q_DIM + KV_DIM:QKV_DIM],
                             next_ref[0, :, KV_DIM:2 * KV_DIM]], axis=0)
    col = lax.broadcasted_iota(jnp.int32, (1, 3 * BLOCK), 1)
    valid = jnp.logical_and(jnp.logical_or(col >= BLOCK, i > 0),
                            jnp.logical_or(col < 2 * BLOCK, i < nb - 1))
    outs = []
    for hk in range(N_KV_HEADS):
        kh = kband[:, hk * HEAD_DIM:(hk + 1) * HEAD_DIM]
        vh = vband[:, hk * HEAD_DIM:(hk + 1) * HEAD_DIM]
        for g in range(GQA_GROUP):
            hq = hk * GQA_GROUP + g
            qh = q[:, hq * HEAD_DIM:(hq + 1) * HEAD_DIM]
            s = lax.dot_general(qh, kh, (((1,), (1,)), ((), ())),
                                preferred_element_type=jnp.float32)
            s = s * (HEAD_DIM ** -0.5) + bias_ref[hq]
            s = jnp.where(valid, s, -jnp.inf)
            sk = sink_ref[hq]
            mx = jnp.maximum(jnp.max(s, axis=-1, keepdims=True), sk)
            p = jnp.exp(s - mx)
            denom = jnp.sum(p, axis=-1, keepdims=True) + jnp.exp(sk - mx)
            p = (p / denom).astype(vh.dtype)
            outs.append(jnp.dot(p, vh, preferred_element_type=jnp.float32))
    o_ref[0] = jnp.concatenate(outs, axis=-1).astype(o_ref.dtype)


def window_attention(qkv, bias, sink):
    B, S, _ = qkv.shape
    nb = S // BLOCK
    kv_col = Q_DIM // (2 * KV_DIM)
    return pl.pallas_call(
        _attn_kernel,
        grid=(B, nb),
        in_specs=[pl.BlockSpec((1, BLOCK, QKV_DIM), lambda b, i: (b, i, 0)),
                  pl.BlockSpec((1, BLOCK, 2 * KV_DIM),
                               lambda b, i: (b, jnp.maximum(i - 1, 0), kv_col)),
                  pl.BlockSpec((1, BLOCK, 2 * KV_DIM),
                               lambda b, i: (b, jnp.minimum(i + 1, nb - 1), kv_col)),
                  pl.BlockSpec((N_Q_HEADS, BLOCK, 3 * BLOCK), lambda b, i: (0, 0, 0)),
                  pl.BlockSpec(memory_space=pltpu.SMEM)],
        out_specs=pl.BlockSpec((1, BLOCK, Q_DIM), lambda b, i: (b, i, 0)),
        out_shape=jax.ShapeDtypeStruct((B, S, Q_DIM), jnp.bfloat16),
        compiler_params=_cparams(("parallel", "arbitrary")),
        name="window_attention",
    )(qkv, qkv, qkv, bias, sink)


def _spatial_gate_kernel(z_ref, g_ref, b_ref, wsp_ref, bsp_ref, o_ref):
    v = z_ref[:, D_GATE:2 * D_GATE].astype(jnp.float32)
    mu = jnp.mean(v, axis=-1, keepdims=True)
    vc = v - mu
    var = jnp.mean(vc * vc, axis=-1, keepdims=True)
    vn = (vc * lax.rsqrt(var + LN_EPS) * g_ref[...] + b_ref[...]).astype(jnp.bfloat16)
    for grp in range(N_SG_GROUPS):
        lo, hi = grp * SG_GROUP_DIM, (grp + 1) * SG_GROUP_DIM
        mixed = jnp.dot(wsp_ref[grp], vn[:, lo:hi], preferred_element_type=jnp.float32)
        mixed = mixed + bsp_ref[:, grp:grp + 1]
        u = z_ref[:, lo:hi].astype(jnp.float32)
        o_ref[:, lo:hi] = (u * mixed).astype(o_ref.dtype)


def spatial_gate(z, ln_g, ln_b, w_sp, b_sp):
    T = z.shape[0]
    return pl.pallas_call(
        _spatial_gate_kernel,
        grid=(T // CHUNK,),
        in_specs=[pl.BlockSpec((CHUNK, 2 * D_GATE), lambda i: (i, 0)),
                  pl.BlockSpec((1, D_GATE), lambda i: (0, 0)),
                  pl.BlockSpec((1, D_GATE), lambda i: (0, 0)),
                  pl.BlockSpec((N_SG_GROUPS, CHUNK, CHUNK), lambda i: (0, 0, 0)),
                  pl.BlockSpec((CHUNK, N_SG_GROUPS), lambda i: (0, 0))],
        out_specs=pl.BlockSpec((CHUNK, D_GATE), lambda i: (i, 0)),
        out_shape=jax.ShapeDtypeStruct((T, D_GATE), jnp.bfloat16),
        compiler_params=_cparams(("parallel",)),
        name="spatial_gate",
    )(z, ln_g.reshape(1, D_GATE), ln_b.reshape(1, D_GATE), w_sp.astype(jnp.bfloat16), b_sp.T)


def _oddeven_merge_sort_pairs(n):
    pairs = []
    p = 1
    while p < n:
        k = p
        while k >= 1:
            for j in range(k % p, n - k, 2 * k):
                for i in range(min(k, n - j - k)):
                    if (i + j) // (2 * p) == (i + j + k) // (2 * p):
                        pairs.append((i + j, i + j + k))
            k //= 2
        p *= 2
    return pairs


_SORT16 = _oddeven_merge_sort_pairs(N_KEYS // SUBLANES)


def _top16_of_keys(s):
    nv = N_KEYS // SUBLANES
    L = s.shape[1]
    sub = lax.broadcasted_iota(jnp.int32, (SUBLANES, L), 0)
    v = [s[j * SUBLANES:(j + 1) * SUBLANES] for j in range(nv)]
    ids = [sub + j * SUBLANES for j in range(nv)]
    for i, j in _SORT16:
        swap = jnp.logical_or(v[j] > v[i], jnp.logical_and(v[j] == v[i], ids[j] < ids[i]))
        v[i], v[j] = jnp.where(swap, v[j], v[i]), jnp.where(swap, v[i], v[j])
        ids[i], ids[j] = jnp.where(swap, ids[j], ids[i]), jnp.where(swap, ids[i], ids[j])
    vals, picks = [], []
    for it in range(PEER_TOPK):
        m = jnp.max(v[0], axis=0, keepdims=True)
        am = jnp.min(jnp.where(v[0] == m, ids[0], N_KEYS), axis=0, keepdims=True)
        hit = ids[0] == am
        vals.append(m)
        picks.append(am)
        last = PEER_TOPK - 1 - it
        for j in range(last):
            v[j] = jnp.where(hit, v[j + 1], v[j])
            ids[j] = jnp.where(hit, ids[j + 1], ids[j])
        v[last] = jnp.where(hit, -jnp.inf, v[last])
    return jnp.concatenate(vals, axis=0), jnp.concatenate(picks, axis=0)


_CAND_ROWS = (
    [(a, 0) for a in range(16)]
    + [None] + [(0, b) for b in range(1, 16)]
    + [None] + [(a, 1) for a in range(1, 8)]
    + [None, None] + [(1, b) for b in range(2, 8)]
    + [(2, 2), (3, 2), (4, 2), (2, 3), (2, 4), (3, 3), None, None]
)
assert sorted(x for x in _CAND_ROWS if x) == sorted(
    (a, b) for a in range(16) for b in range(16) if (a + 1) * (b + 1) <= 16)


def _rows(x, picks):
    pieces, i = [], 0
    while i < len(picks):
        j = i
        while j + 1 < len(picks) and picks[j + 1] == picks[j] + 1:
            j += 1
        pieces.append(x[picks[i]:picks[j] + 1])
        i = j + 1
    return pieces[0] if len(pieces) == 1 else jnp.concatenate(pieces, axis=0)


def _cand_positions(L):
    row = lax.broadcasted_iota(jnp.int32, (len(_CAND_ROWS), L), 0)
    pos = jnp.full(row.shape, _PAD_POS, jnp.int32)
    for r, c in enumerate(_CAND_ROWS):
        if c is not None:
            pos = jnp.where(row == r, c[0] * PEER_TOPK + c[1], pos)
    return pos


_PAD_POS = 1 << 20


def _joint_top16(v1, i1, v2, i2, pos):
    a_of = [c[0] if c else 0 for c in _CAND_ROWS]
    b_of = [c[1] if c else 0 for c in _CAND_ROWS]
    cand = jnp.where(pos < _PAD_POS, _rows(v1, a_of) + _rows(v2, b_of), -jnp.inf)
    cidx = _rows(i1, a_of) * N_KEYS + _rows(i2, b_of)
    vals, picks = [], []
    for _ in range(PEER_TOPK):
        m = jnp.max(cand, axis=0, keepdims=True)
        pm = jnp.min(jnp.where(cand == m, pos, 1 << 21), axis=0, keepdims=True)
        hit = pos == pm
        vals.append(m)
        picks.append(jnp.max(jnp.where(hit, cidx, -1), axis=0, keepdims=True))
        cand = jnp.where(hit, -jnp.inf, cand)
    return jnp.concatenate(vals, axis=0), jnp.concatenate(picks, axis=0)


def _route_kernel(q_ref, keys_ref, idx_ref, gate_ref, *, precision):
    nt = (((1,), (1,)), ((), ()))
    pos = _cand_positions(LANES)
    for j in range(q_ref.shape[0] // LANES):
        q = q_ref[j * LANES:(j + 1) * LANES, :]
        s1 = lax.dot_general(keys_ref[0, 0], q[:, 0:D_HALF], nt,
                             preferred_element_type=jnp.float32, precision=precision)
        s2 = lax.dot_general(keys_ref[0, 1], q[:, D_HALF:2 * D_HALF], nt,
                             preferred_element_type=jnp.float32, precision=precision)
        v1, i1 = _top16_of_keys(s1)
        v2, i2 = _top16_of_keys(s2)
        top_s, top_i = _joint_top16(v1, i1, v2, i2, pos)
        e = jnp.exp(top_s - top_s[0:1])
        gate_ref[:, j * LANES:(j + 1) * LANES] = e / jnp.sum(e, axis=0, keepdims=True)
        idx_ref[:, j * LANES:(j + 1) * LANES] = top_i * ROW_SUB


def peer_route(q, sub_keys, *, tb=512, precision=None):
    T = q.shape[0]
    tb = min(tb, T)
    return pl.pallas_call(
        functools.partial(_route_kernel, precision=precision),
        grid=(T // tb, PEER_HEADS),
        in_specs=[pl.BlockSpec((tb, 2 * D_HALF), lambda i, h: (i, h)),
                  pl.BlockSpec((1, 2, N_KEYS, D_HALF), lambda i, h: (h, 0, 0, 0))],
        out_specs=[pl.BlockSpec((PEER_TOPK, tb), lambda i, h: (h, i)),
                   pl.BlockSpec((PEER_TOPK, tb), lambda i, h: (h, i))],
        out_shape=[jax.ShapeDtypeStruct((N_SLOTS, T), jnp.int32),
                   jax.ShapeDtypeStruct((N_SLOTS, T), jnp.float32)],
        compiler_params=_cparams(("parallel", "arbitrary")),
        name="peer_route",
    )(q, sub_keys)


def pack_table(tbl):
    E, D = tbl.shape
    bits = lax.bitcast_convert_type(tbl.astype(jnp.bfloat16), jnp.uint16).astype(jnp.uint32)
    packed = bits[:, :D // 2] | (bits[:, D // 2:] << 16)
    return packed.reshape(E * ROW_SUB, LANES)


def _unpack(words):
    lo = pltpu.bitcast(words << 16, jnp.float32)
    hi = pltpu.bitcast(words & jnp.uint32(0xFFFF0000), jnp.float32)
    return lo, hi


def _gather_rows(idx_ref, tbl_ref, t, planes_ref):
    for k in range(N_SLOTS):
        off = pl.multiple_of(idx_ref[t, k], ROW_SUB)
        planes_ref[pl.ds(k, ROW_SUB, stride=PLANE_STRIDE), :] = tbl_ref[pl.ds(off, ROW_SUB), :]


def _plane(planes_ref, c):
    return _unpack(planes_ref[c * PLANE_STRIDE:c * PLANE_STRIDE + N_SLOTS, :])


def _two_stage_tokens(tb, gather, compute, planes_a, planes_b, before_pair=None):
    gather(0, planes_a)

    def pair(i, carry):
        t0 = 2 * i
        if before_pair is not None:
            before_pair(t0)
        gather(t0 + 1, planes_b)
        compute(t0, planes_a, 0)
        gather(jnp.minimum(t0 + 2, tb - 1), planes_a)
        compute(t0 + 1, planes_b, 1)
        return carry

    lax.fori_loop(0, tb // 2, pair, 0)


def _peer_down_kernel(idx_ref, x_ref, gate_ref, tbl_ref, o_ref, planes_a, planes_b, part_ref):
    tb = x_ref.shape[0]
    lane = lax.broadcasted_iota(jnp.int32, (N_SLOTS, tb), 1)

    def compute(t, planes_ref, slot):
        acc = jnp.zeros((N_SLOTS, LANES), jnp.float32)
        for c in range(ROW_SUB):
            lo, hi = _plane(planes_ref, c)
            acc = acc + lo * x_ref[t, c:c + 1, :] + hi * x_ref[t, ROW_SUB + c:ROW_SUB + c + 1, :]
        part_ref[slot] = acc

    def place_pair(t0):
        col_a = jnp.sum(part_ref[0], axis=1, keepdims=True)
        col_b = jnp.sum(part_ref[1], axis=1, keepdims=True)
        o_ref[...] = jnp.where(lane == t0 - 2, col_a, jnp.where(lane == t0 - 1, col_b, o_ref[...]))

    o_ref[...] = jnp.zeros(o_ref.shape, jnp.float32)
    part_ref[...] = jnp.zeros(part_ref.shape, jnp.float32)
    _two_stage_tokens(tb, functools.partial(_gather_rows, idx_ref, tbl_ref), compute, planes_a, planes_b,
                      before_pair=place_pair)
    place_pair(tb)
    o_ref[...] = _gelu(o_ref[...]) * gate_ref[...]


def _expert_call(kernel_fn, T, tb, in_specs, out_spec, out_shape, name, extra_scratch=()):
    planes = pltpu.VMEM((ROW_SUB * PLANE_STRIDE, LANES), jnp.uint32)
    return pl.pallas_call(
        kernel_fn,
        grid=(T // tb,),
        in_specs=in_specs,
        out_specs=out_spec,
        out_shape=out_shape,
        scratch_shapes=[planes, planes, *extra_scratch],
        compiler_params=_cparams(("arbitrary",)),
        name=name,
    )


def peer_down(idx, xn3, gate_t, tbl, *, tb=128):
    T = xn3.shape[0]
    tb = min(tb, T)
    return _expert_call(
        _peer_down_kernel, T, tb,
        [pl.BlockSpec((tb, N_SLOTS), lambda i: (i, 0), memory_space=pltpu.SMEM),
         pl.BlockSpec((tb, SUBLANES, LANES), lambda i: (i, 0, 0)),
         pl.BlockSpec((N_SLOTS, tb), lambda i: (0, i)),
         pl.BlockSpec(tbl.shape, lambda i: (0, 0), pipeline_mode=pl.Buffered(1))],
        pl.BlockSpec((N_SLOTS, tb), lambda i: (0, i)),
        jax.ShapeDtypeStruct((N_SLOTS, T), jnp.float32),
        "peer_down",
        extra_scratch=[pltpu.VMEM((2, N_SLOTS, LANES), jnp.float32)],
    )(idx, xn3, gate_t, tbl)


def _peer_up_kernel(idx_ref, w_ref, h_ref, tbl_ref, o_ref, planes_a, planes_b):
    tb = h_ref.shape[0]
    lane = lax.broadcasted_iota(jnp.int32, (N_SLOTS, tb), 1)

    def compute(t, planes_ref, slot):
        col = jnp.sum(jnp.where(lane == t, w_ref[...], 0.0), axis=1, keepdims=True)
        wb = jnp.broadcast_to(col, (N_SLOTS, LANES))
        y_lo, y_hi = [], []
        for c in range(ROW_SUB):
            lo, hi = _plane(planes_ref, c)
            y_lo.append(jnp.sum(lo * wb, axis=0, keepdims=True))
            y_hi.append(jnp.sum(hi * wb, axis=0, keepdims=True))
        o_ref[t] = h_ref[t] + jnp.concatenate(y_lo + y_hi, axis=0)

    _two_stage_tokens(tb, functools.partial(_gather_rows, idx_ref, tbl_ref), compute, planes_a, planes_b)


def peer_up(idx, w_t, h3, tbl, *, tb=128):
    T = h3.shape[0]
    tb = min(tb, T)
    return _expert_call(
        _peer_up_kernel, T, tb,
        [pl.BlockSpec((tb, N_SLOTS), lambda i: (i, 0), memory_space=pltpu.SMEM),
         pl.BlockSpec((N_SLOTS, tb), lambda i: (0, i)),
         pl.BlockSpec((tb, SUBLANES, LANES), lambda i: (i, 0, 0)),
         pl.BlockSpec(tbl.shape, lambda i: (0, 0), pipeline_mode=pl.Buffered(1))],
        pl.BlockSpec((tb, SUBLANES, LANES), lambda i: (i, 0, 0)),
        jax.ShapeDtypeStruct(h3.shape, jnp.float32),
        "peer_up",
    )(idx, w_t, h3, tbl)


def peer_layer(h, g, w_query, sub_keys, down, up):
    T = h.shape[0]
    hi = lax.Precision.HIGHEST
    q, xn = norm_matmul(h, g, w_query.astype(jnp.bfloat16), out_dtype=jnp.float32, emit_xn=True)
    idx_t, gate_t = peer_route(q, sub_keys, precision=hi)
    idx = idx_t.T
    w_t = peer_down(idx, xn.reshape(T, SUBLANES, LANES), gate_t, pack_table(down))
    return peer_up(idx, w_t, h.reshape(T, SUBLANES, LANES), pack_table(up)).reshape(T, D_MODEL)


def _rms_kernel(x_ref, g_ref, o_ref):
    x = x_ref[...]
    o_ref[...] = x * lax.rsqrt(jnp.mean(x * x, axis=-1, keepdims=True) + NORM_EPS) * g_ref[...]


def rms_norm(x, g, *, tm=512):
    T, D = x.shape
    tm = min(tm, T)
    return pl.pallas_call(
        _rms_kernel,
        grid=(T // tm,),
        in_specs=[pl.BlockSpec((tm, D), lambda i: (i, 0)), pl.BlockSpec((1, D), lambda i: (0, 0))],
        out_specs=pl.BlockSpec((tm, D), lambda i: (i, 0)),
        out_shape=jax.ShapeDtypeStruct((T, D), jnp.float32),
        compiler_params=_cparams(("parallel",)),
        name="final_norm",
    )(x, g.reshape(1, D))


def kernel(x, rel_table, mix_norm_g, attn_w_in, attn_w_out, attn_sink, sg_w_in, sg_ln_g, sg_ln_b, sg_w_spatial, sg_b_spatial, sg_w_out, ffn_norm_g, peer_w_query, peer_sub_keys, peer_down, peer_up, final_norm_g):
    B, S, D = x.shape
    T = B * S
    bf16 = jnp.bfloat16
    h = x.reshape(T, D)

    qkv = norm_matmul(h, mix_norm_g[0], attn_w_in[0].astype(bf16), out_dtype=bf16)
    att = window_attention(qkv.reshape(B, S, QKV_DIM), attn_bias(rel_table), attn_sink[0])
    h = matmul_residual(att.reshape(T, Q_DIM), attn_w_out[0].astype(bf16), h)
    h = peer_layer(h, ffn_norm_g[0], peer_w_query[0], peer_sub_keys[0], peer_down[0], peer_up[0])

    z = norm_matmul(h, mix_norm_g[1], sg_w_in[0].astype(bf16), out_dtype=bf16, act=True, tn=1536)
    gated = spatial_gate(z, sg_ln_g[0], sg_ln_b[0], sg_w_spatial[0], sg_b_spatial[0])
    h = matmul_residual(gated, sg_w_out[0].astype(bf16), h)
    h = peer_layer(h, ffn_norm_g[1], peer_w_query[1], peer_sub_keys[1], peer_down[1], peer_up[1])

    return rms_norm(h, final_norm_g).reshape(B, S, D)
```

```python
import dataclasses
import functools
import math

import jax
import jax.numpy as jnp
from jax import lax
from jax.experimental import pallas as pl
from jax.experimental.pallas import tpu as pltpu
from jax.experimental.pallas import tpu_sc as plsc

D_MODEL = 1024
HEAD_DIM = 64
N_Q_HEADS = 16
N_KV_HEADS = 4
GQA_GROUP = 4
WINDOW = 128
BLOCK = 128
REL_BUCKETS = 32
REL_MAX_DIST = 128
Q_DIM = N_Q_HEADS * HEAD_DIM
KV_DIM = N_KV_HEADS * HEAD_DIM
QKV_DIM = Q_DIM + 2 * KV_DIM
CHUNK = 128
D_GATE = 3072
N_SG_GROUPS = 8
SG_GROUP_DIM = D_GATE // N_SG_GROUPS
N_KEYS = 128
PEER_HEADS = 8
PEER_TOPK = 16
D_HALF = 128
N_SLOTS = PEER_HEADS * PEER_TOPK
NORM_EPS = 1e-6
LN_EPS = 1e-5

LANES = 128
SUBLANES = 8
ROW_WORDS = D_MODEL // 2
ROW_SUB = ROW_WORDS // LANES
PLANE_STRIDE = 136
VMEM_LIMIT = 56 * 1024 * 1024

_GELU_C = math.sqrt(2.0 / math.pi)


def _gelu(x):
    return 0.5 * x * (1.0 + jnp.tanh(_GELU_C * (x + 0.044715 * (x * x * x))))


def _cparams(sem):
    return pltpu.CompilerParams(dimension_semantics=sem, vmem_limit_bytes=VMEM_LIMIT)


def _norm_matmul_kernel(*refs, act, emit_xn, add_res, precision):
    refs = list(refs)
    x_ref = refs.pop(0)
    r_ref = refs.pop(0) if add_res else None
    g_ref, w_ref, o_ref = refs.pop(0), refs.pop(0), refs.pop(0)
    xn_ref = refs.pop(0) if emit_xn else None
    sum_ref = refs.pop(0) if add_res else None
    xs_ref = refs.pop(0)

    @pl.when(pl.program_id(1) == 0)
    def _():
        x = x_ref[...]
        if add_res:
            x = x + r_ref[...]
            sum_ref[...] = x
        y = x * lax.rsqrt(jnp.mean(x * x, axis=-1, keepdims=True) + NORM_EPS) * g_ref[...]
        xs_ref[...] = y.astype(xs_ref.dtype)
        if emit_xn:
            xn_ref[...] = y

    acc = jnp.dot(xs_ref[...], w_ref[...], preferred_element_type=jnp.float32,
                  precision=precision)
    if act:
        acc = _gelu(acc)
    o_ref[...] = acc.astype(o_ref.dtype)


def norm_matmul(x, g, w, *, out_dtype, res=None, act=False, emit_xn=False, tm=512, tn=None,
                precision=None):
    T, D = x.shape
    N = w.shape[1]
    tn = tn or N
    tm = min(tm, T)
    row = pl.BlockSpec((tm, D), lambda i, j: (i, 0))
    out_shape = [jax.ShapeDtypeStruct((T, N), out_dtype)]
    out_specs = [pl.BlockSpec((tm, tn), lambda i, j: (i, j))]
    for flag in (emit_xn, res is not None):
        if flag:
            out_shape.append(jax.ShapeDtypeStruct((T, D), jnp.float32))
            out_specs.append(row)
    args = [x] + ([res] if res is not None else []) + [g.reshape(1, D), w]
    in_specs = [row] * (len(args) - 2) + [pl.BlockSpec((1, D), lambda i, j: (0, 0)),
                                          pl.BlockSpec((D, tn), lambda i, j: (0, j))]
    return pl.pallas_call(
        functools.partial(_norm_matmul_kernel, act=act, emit_xn=emit_xn, add_res=res is not None,
                          precision=precision),
        grid=(T // tm, N // tn),
        in_specs=in_specs,
        out_specs=out_specs,
        out_shape=out_shape,
        scratch_shapes=[pltpu.VMEM((tm, D), w.dtype)],
        compiler_params=_cparams(("parallel", "arbitrary")),
        name="norm_matmul",
    )(*args)


def _matmul_res_kernel(a_ref, w_ref, h_ref, o_ref):
    o_ref[...] = h_ref[...] + jnp.dot(a_ref[...], w_ref[...],
                                      preferred_element_type=jnp.float32)


def matmul_residual(a, w, h, *, tm=512):
    T, K = a.shape
    N = w.shape[1]
    tm = min(tm, T)
    return pl.pallas_call(
        _matmul_res_kernel,
        grid=(T // tm,),
        in_specs=[pl.BlockSpec((tm, K), lambda i: (i, 0)),
                  pl.BlockSpec((K, N), lambda i: (0, 0)),
                  pl.BlockSpec((tm, N), lambda i: (i, 0))],
        out_specs=pl.BlockSpec((tm, N), lambda i: (i, 0)),
        out_shape=jax.ShapeDtypeStruct((T, N), jnp.float32),
        compiler_params=_cparams(("parallel",)),
        name="matmul_residual",
    )(a, w, h)


def _t5_bucket(rel):
    nb = REL_BUCKETS // 2
    max_exact = nb // 2
    ret = jnp.where(rel > 0, nb, 0)
    n = jnp.abs(rel)
    nf = jnp.maximum(n, 1).astype(jnp.float32)
    large = max_exact + (jnp.log(nf / max_exact) / math.log(REL_MAX_DIST / max_exact)
                         * (nb - max_exact)).astype(jnp.int32)
    large = jnp.minimum(large, nb - 1)
    return (ret + jnp.where(n < max_exact, n, large)).astype(jnp.int32)


def _bias_kernel(bucket_ref, window_ref, table_ref, o_ref):
    bucket = bucket_ref[...]
    in_window = window_ref[...] > 0
    for hq in range(N_Q_HEADS):
        acc = jnp.zeros(bucket.shape, jnp.float32)
        for b in range(REL_BUCKETS):
            acc = jnp.where(bucket == b, table_ref[b, hq], acc)
        o_ref[hq] = jnp.where(in_window, acc, -jnp.inf)


def attn_bias(rel_table):
    qi = jnp.arange(BLOCK)[:, None]
    kj = jnp.arange(3 * BLOCK)[None, :]
    rel = kj - BLOCK - qi
    bucket = _t5_bucket(rel)
    window = (jnp.abs(rel) <= WINDOW).astype(jnp.int32)
    return pl.pallas_call(
        _bias_kernel,
        in_specs=[pl.BlockSpec(memory_space=pltpu.VMEM),
                  pl.BlockSpec(memory_space=pltpu.VMEM),
                  pl.BlockSpec(memory_space=pltpu.SMEM)],
        out_specs=pl.BlockSpec(memory_space=pltpu.VMEM),
        out_shape=jax.ShapeDtypeStruct((N_Q_HEADS, BLOCK, 3 * BLOCK), jnp.float32),
        name="attn_bias",
    )(bucket, window, rel_table)


def _attn_kernel(cur_ref, prev_ref, next_ref, bias_ref, sink_ref, o_ref):
    i = pl.program_id(1)
    nb = pl.num_programs(1)
    q = cur_ref[0, :, 0:Q_DIM]
    kband = jnp.concatenate([prev_ref[0, :, 0:KV_DIM], cur_ref[0, :, Q_DIM:Q_DIM + KV_DIM],
                             next_ref[0, :, 0:KV_DIM]], axis=0)
    vband = jnp.concatenate([prev_ref[0, :, KV_DIM:2 * KV_DIM], cur_ref[0, :, Q_DIM + KV_DIM:QKV_DIM],
                             next_ref[0, :, KV_DIM:2 * KV_DIM]], axis=0)
    col = lax.broadcasted_iota(jnp.int32, (1, 3 * BLOCK), 1)
    valid = jnp.logical_and(jnp.logical_or(col >= BLOCK, i > 0),
                            jnp.logical_or(col < 2 * BLOCK, i < nb - 1))
    outs = []
    for hk in range(N_KV_HEADS):
        kh = kband[:, hk * HEAD_DIM:(hk + 1) * HEAD_DIM]
        vh = vband[:, hk * HEAD_DIM:(hk + 1) * HEAD_DIM]
        for g in range(GQA_GROUP):
            hq = hk * GQA_GROUP + g
            qh = q[:, hq * HEAD_DIM:(hq + 1) * HEAD_DIM]
            s = lax.dot_general(qh, kh, (((1,), (1,)), ((), ())),
                                preferred_element_type=jnp.float32)
            s = s * (HEAD_DIM ** -0.5) + bias_ref[hq]
            s = jnp.where(valid, s, -jnp.inf)
            sk = sink_ref[hq]
            mx = jnp.maximum(jnp.max(s, axis=-1, keepdims=True), sk)
            p = jnp.exp(s - mx)
            denom = jnp.sum(p, axis=-1, keepdims=True) + jnp.exp(sk - mx)
            p = (p / denom).astype(vh.dtype)
            outs.append(jnp.dot(p, vh, preferred_element_type=jnp.float32))
    o_ref[0] = jnp.concatenate(outs, axis=-1).astype(o_ref.dtype)


def window_attention(qkv, bias, sink):
    B, S, _ = qkv.shape
    nb = S // BLOCK
    kv_col = Q_DIM // (2 * KV_DIM)
    return pl.pallas_call(
        _attn_kernel,
        grid=(B, nb),
        in_specs=[pl.BlockSpec((1, BLOCK, QKV_DIM), lambda b, i: (b, i, 0)),
                  pl.BlockSpec((1, BLOCK, 2 * KV_DIM),
                               lambda b, i: (b, jnp.maximum(i - 1, 0), kv_col)),
                  pl.BlockSpec((1, BLOCK, 2 * KV_DIM),
                               lambda b, i: (b, jnp.minimum(i + 1, nb - 1), kv_col)),
                  pl.BlockSpec((N_Q_HEADS, BLOCK, 3 * BLOCK), lambda b, i: (0, 0, 0)),
                  pl.BlockSpec(memory_space=pltpu.SMEM)],
        out_specs=pl.BlockSpec((1, BLOCK, Q_DIM), lambda b, i: (b, i, 0)),
        out_shape=jax.ShapeDtypeStruct((B, S, Q_DIM), jnp.bfloat16),
        compiler_params=_cparams(("parallel", "arbitrary")),
        name="window_attention",
    )(qkv, qkv, qkv, bias, sink)


def _spatial_gate_kernel(z_ref, g_ref, b_ref, wsp_ref, bsp_ref, o_ref):
    v = z_ref[:, D_GATE:2 * D_GATE].astype(jnp.float32)
    mu = jnp.mean(v, axis=-1, keepdims=True)
    vc = v - mu
    var = jnp.mean(vc * vc, axis=-1, keepdims=True)
    vn = (vc * lax.rsqrt(var + LN_EPS) * g_ref[...] + b_ref[...]).astype(jnp.bfloat16)
    for grp in range(N_SG_GROUPS):
        lo, hi = grp * SG_GROUP_DIM, (grp + 1) * SG_GROUP_DIM
        mixed = jnp.dot(wsp_ref[grp], vn[:, lo:hi], preferred_element_type=jnp.float32)
        mixed = mixed + bsp_ref[:, grp:grp + 1]
        u = z_ref[:, lo:hi].astype(jnp.float32)
        o_ref[:, lo:hi] = (u * mixed).astype(o_ref.dtype)


def spatial_gate(z, ln_g, ln_b, w_sp, b_sp):
    T = z.shape[0]
    return pl.pallas_call(
        _spatial_gate_kernel,
        grid=(T // CHUNK,),
        in_specs=[pl.BlockSpec((CHUNK, 2 * D_GATE), lambda i: (i, 0)),
                  pl.BlockSpec((1, D_GATE), lambda i: (0, 0)),
                  pl.BlockSpec((1, D_GATE), lambda i: (0, 0)),
                  pl.BlockSpec((N_SG_GROUPS, CHUNK, CHUNK), lambda i: (0, 0, 0)),
                  pl.BlockSpec((CHUNK, N_SG_GROUPS), lambda i: (0, 0))],
        out_specs=pl.BlockSpec((CHUNK, D_GATE), lambda i: (i, 0)),
        out_shape=jax.ShapeDtypeStruct((T, D_GATE), jnp.bfloat16),
        compiler_params=_cparams(("parallel",)),
        name="spatial_gate",
    )(z, ln_g.reshape(1, D_GATE), ln_b.reshape(1, D_GATE), w_sp.astype(jnp.bfloat16), b_sp.T)


def _oddeven_merge_sort_pairs(n):
    pairs = []
    p = 1
    while p < n:
        k = p
        while k >= 1:
            for j in range(k % p, n - k, 2 * k):
                for i in range(min(k, n - j - k)):
                    if (i + j) // (2 * p) == (i + j + k) // (2 * p):
                        pairs.append((i + j, i + j + k))
            k //= 2
        p *= 2
    return pairs


_SORT16 = _oddeven_merge_sort_pairs(N_KEYS // SUBLANES)


def _top16_of_keys(s):
    nv = N_KEYS // SUBLANES
    L = s.shape[1]
    sub = lax.broadcasted_iota(jnp.int32, (SUBLANES, L), 0)
    v = [s[j * SUBLANES:(j + 1) * SUBLANES] for j in range(nv)]
    ids = [sub + j * SUBLANES for j in range(nv)]
    for i, j in _SORT16:
        swap = jnp.logical_or(v[j] > v[i], jnp.logical_and(v[j] == v[i], ids[j] < ids[i]))
        v[i], v[j] = jnp.where(swap, v[j], v[i]), jnp.where(swap, v[i], v[j])
        ids[i], ids[j] = jnp.where(swap, ids[j], ids[i]), jnp.where(swap, ids[i], ids[j])
    vals, picks = [], []
    for it in range(PEER_TOPK):
        m = jnp.max(v[0], axis=0, keepdims=True)
        am = jnp.min(jnp.where(v[0] == m, ids[0], N_KEYS), axis=0, keepdims=True)
        hit = ids[0] == am
        vals.append(m)
        picks.append(am)
        last = PEER_TOPK - 1 - it
        for j in range(last):
            v[j] = jnp.where(hit, v[j + 1], v[j])
            ids[j] = jnp.where(hit, ids[j + 1], ids[j])
        v[last] = jnp.where(hit, -jnp.inf, v[last])
    return jnp.concatenate(vals, axis=0), jnp.concatenate(picks, axis=0)


_CAND_ROWS = (
    [(a, 0) for a in range(16)]
    + [None] + [(0, b) for b in range(1, 16)]
    + [None] + [(a, 1) for a in range(1, 8)]
    + [None, None] + [(1, b) for b in range(2, 8)]
    + [(2, 2), (3, 2), (4, 2), (2, 3), (2, 4), (3, 3), None, None]
)
assert sorted(x for x in _CAND_ROWS if x) == sorted(
    (a, b) for a in range(16) for b in range(16) if (a + 1) * (b + 1) <= 16)


def _rows(x, picks):
    pieces, i = [], 0
    while i < len(picks):
        j = i
        while j + 1 < len(picks) and picks[j + 1] == picks[j] + 1:
            j += 1
        pieces.append(x[picks[i]:picks[j] + 1])
        i = j + 1
    return pieces[0] if len(pieces) == 1 else jnp.concatenate(pieces, axis=0)


def _cand_positions(L):
    row = lax.broadcasted_iota(jnp.int32, (len(_CAND_ROWS), L), 0)
    pos = jnp.full(row.shape, _PAD_POS, jnp.int32)
    for r, c in enumerate(_CAND_ROWS):
        if c is not None:
            pos = jnp.where(row == r, c[0] * PEER_TOPK + c[1], pos)
    return pos


_PAD_POS = 1 << 20


def _joint_top16(v1, i1, v2, i2, pos):
    a_of = [c[0] if c else 0 for c in _CAND_ROWS]
    b_of = [c[1] if c else 0 for c in _CAND_ROWS]
    cand = jnp.where(pos < _PAD_POS, _rows(v1, a_of) + _rows(v2, b_of), -jnp.inf)
    cidx = _rows(i1, a_of) * N_KEYS + _rows(i2, b_of)
    vals, picks = [], []
    for _ in range(PEER_TOPK):
        m = jnp.max(cand, axis=0, keepdims=True)
        pm = jnp.min(jnp.where(cand == m, pos, 1 << 21), axis=0, keepdims=True)
        hit = pos == pm
        vals.append(m)
        picks.append(jnp.max(jnp.where(hit, cidx, -1), axis=0, keepdims=True))
        cand = jnp.where(hit, -jnp.inf, cand)
    return jnp.concatenate(vals, axis=0), jnp.concatenate(picks, axis=0)


def _route_kernel(q_ref, keys_ref, idx_ref, gate_ref, *, precision):
    nt = (((1,), (1,)), ((), ()))
    pos = _cand_positions(LANES)
    for j in range(q_ref.shape[0] // LANES):
        q = q_ref[j * LANES:(j + 1) * LANES, :]
        s1 = lax.dot_general(keys_ref[0, 0], q[:, 0:D_HALF], nt,
                             preferred_element_type=jnp.float32, precision=precision)
        s2 = lax.dot_general(keys_ref[0, 1], q[:, D_HALF:2 * D_HALF], nt,
                             preferred_element_type=jnp.float32, precision=precision)
        v1, i1 = _top16_of_keys(s1)
        v2, i2 = _top16_of_keys(s2)
        top_s, top_i = _joint_top16(v1, i1, v2, i2, pos)
        e = jnp.exp(top_s - top_s[0:1])
        gate_ref[:, j * LANES:(j + 1) * LANES] = e / jnp.sum(e, axis=0, keepdims=True)
        idx_ref[:, j * LANES:(j + 1) * LANES] = top_i * ROW_SUB


def peer_route(q, sub_keys, *, tb=512, precision=None):
    T = q.shape[0]
    tb = min(tb, T)
    return pl.pallas_call(
        functools.partial(_route_kernel, precision=precision),
        grid=(T // tb, PEER_HEADS),
        in_specs=[pl.BlockSpec((tb, 2 * D_HALF), lambda i, h: (i, h)),
                  pl.BlockSpec((1, 2, N_KEYS, D_HALF), lambda i, h: (h, 0, 0, 0))],
        out_specs=[pl.BlockSpec((PEER_TOPK, tb), lambda i, h: (h, i)),
                   pl.BlockSpec((PEER_TOPK, tb), lambda i, h: (h, i))],
        out_shape=[jax.ShapeDtypeStruct((N_SLOTS, T), jnp.int32),
                   jax.ShapeDtypeStruct((N_SLOTS, T), jnp.float32)],
        compiler_params=_cparams(("parallel", "arbitrary")),
        name="peer_route",
    )(q, sub_keys)


def pack_table(tbl):
    D = tbl.shape[1]
    bits = lax.bitcast_convert_type(tbl.astype(jnp.bfloat16), jnp.uint16).astype(jnp.uint32)
    return bits[:, :D // 2] | (bits[:, D // 2:] << 16)


def _unpack(words):
    lo = pltpu.bitcast(words << 16, jnp.float32)
    hi = pltpu.bitcast(words & jnp.uint32(0xFFFF0000), jnp.float32)
    return lo, hi


def _gather_rows(idx_ref, tbl_ref, t, planes_ref):
    for k in range(N_SLOTS):
        off = pl.multiple_of(idx_ref[t, k], ROW_SUB)
        planes_ref[pl.ds(k, ROW_SUB, stride=PLANE_STRIDE), :] = tbl_ref[pl.ds(off, ROW_SUB), :]


def _plane(planes_ref, c):
    return _unpack(planes_ref[c * PLANE_STRIDE:c * PLANE_STRIDE + N_SLOTS, :])


def _two_stage_tokens(tb, gather, compute, planes_a, planes_b, before_pair=None):
    gather(0, planes_a)

    def pair(i, carry):
        t0 = 2 * i
        if before_pair is not None:
            before_pair(t0)
        gather(t0 + 1, planes_b)
        compute(t0, planes_a, 0)
        gather(jnp.minimum(t0 + 2, tb - 1), planes_a)
        compute(t0 + 1, planes_b, 1)
        return carry

    lax.fori_loop(0, tb // 2, pair, 0)


def _peer_down_kernel(idx_ref, x_ref, gate_ref, tbl_ref, o_ref, planes_a, planes_b, part_ref):
    tb = x_ref.shape[0]
    lane = lax.broadcasted_iota(jnp.int32, (N_SLOTS, tb), 1)

    def compute(t, planes_ref, slot):
        acc = jnp.zeros((N_SLOTS, LANES), jnp.float32)
        for c in range(ROW_SUB):
            lo, hi = _plane(planes_ref, c)
            acc = acc + lo * x_ref[t, c:c + 1, :] + hi * x_ref[t, ROW_SUB + c:ROW_SUB + c + 1, :]
        part_ref[slot] = acc

    def place_pair(t0):
        col_a = jnp.sum(part_ref[0], axis=1, keepdims=True)
        col_b = jnp.sum(part_ref[1], axis=1, keepdims=True)
        o_ref[...] = jnp.where(lane == t0 - 2, col_a, jnp.where(lane == t0 - 1, col_b, o_ref[...]))

    o_ref[...] = jnp.zeros(o_ref.shape, jnp.float32)
    part_ref[...] = jnp.zeros(part_ref.shape, jnp.float32)
    _two_stage_tokens(tb, functools.partial(_gather_rows, idx_ref, tbl_ref), compute, planes_a, planes_b,
                      before_pair=place_pair)
    place_pair(tb)
    o_ref[...] = _gelu(o_ref[...]) * gate_ref[...]


def _expert_call(kernel_fn, T, tb, in_specs, out_spec, out_shape, name, extra_scratch=()):
    planes = pltpu.VMEM((ROW_SUB * PLANE_STRIDE, LANES), jnp.uint32)
    return pl.pallas_call(
        kernel_fn,
        grid=(T // tb,),
        in_specs=in_specs,
        out_specs=out_spec,
        out_shape=out_shape,
        scratch_shapes=[planes, planes, *extra_scratch],
        compiler_params=_cparams(("arbitrary",)),
        name=name,
    )


def peer_down(idx, xn3, gate_t, tbl, *, tb=128):
    T = xn3.shape[0]
    tb = min(tb, T)
    return _expert_call(
        _peer_down_kernel, T, tb,
        [pl.BlockSpec((tb, N_SLOTS), lambda i: (i, 0), memory_space=pltpu.SMEM),
         pl.BlockSpec((tb, SUBLANES, LANES), lambda i: (i, 0, 0)),
         pl.BlockSpec((N_SLOTS, tb), lambda i: (0, i)),
         pl.BlockSpec(tbl.shape, lambda i: (0, 0), pipeline_mode=pl.Buffered(1))],
        pl.BlockSpec((N_SLOTS, tb), lambda i: (0, i)),
        jax.ShapeDtypeStruct((N_SLOTS, T), jnp.float32),
        "peer_down",
        extra_scratch=[pltpu.VMEM((2, N_SLOTS, LANES), jnp.float32)],
    )(idx, xn3, gate_t, tbl)


def _peer_up_kernel(idx_ref, w_ref, h_ref, tbl_ref, o_ref, planes_a, planes_b):
    tb = h_ref.shape[0]
    lane = lax.broadcasted_iota(jnp.int32, (N_SLOTS, tb), 1)

    def compute(t, planes_ref, slot):
        col = jnp.sum(jnp.where(lane == t, w_ref[...], 0.0), axis=1, keepdims=True)
        wb = jnp.broadcast_to(col, (N_SLOTS, LANES))
        y_lo, y_hi = [], []
        for c in range(ROW_SUB):
            lo, hi = _plane(planes_ref, c)
            y_lo.append(jnp.sum(lo * wb, axis=0, keepdims=True))
            y_hi.append(jnp.sum(hi * wb, axis=0, keepdims=True))
        o_ref[t] = h_ref[t] + jnp.concatenate(y_lo + y_hi, axis=0)

    _two_stage_tokens(tb, functools.partial(_gather_rows, idx_ref, tbl_ref), compute, planes_a, planes_b)


def peer_up(idx, w_t, h3, tbl, *, tb=128):
    T = h3.shape[0]
    tb = min(tb, T)
    return _expert_call(
        _peer_up_kernel, T, tb,
        [pl.BlockSpec((tb, N_SLOTS), lambda i: (i, 0), memory_space=pltpu.SMEM),
         pl.BlockSpec((N_SLOTS, tb), lambda i: (0, i)),
         pl.BlockSpec((tb, SUBLANES, LANES), lambda i: (i, 0, 0)),
         pl.BlockSpec(tbl.shape, lambda i: (0, 0), pipeline_mode=pl.Buffered(1))],
        pl.BlockSpec((tb, SUBLANES, LANES), lambda i: (i, 0, 0)),
        jax.ShapeDtypeStruct(h3.shape, jnp.float32),
        "peer_up",
    )(idx, w_t, h3, tbl)


SC_CORES = 2
SC_SUBCORES = 16
SC_WORKERS = SC_CORES * SC_SUBCORES
SC_LANES = 16
SC_GATHER_ROWS = N_SLOTS // 2
SC_ROW_BLOCK = 8
SC_TOKEN_BLOCK = 32
SC_COL_GROUP = 8


def _sc_params():
    cp = pltpu.CompilerParams()
    if "needs_layout_passes" in pltpu.CompilerParams.__dataclass_fields__:
        cp = dataclasses.replace(cp, needs_layout_passes=False)
    return cp


def peer_up_sc(ids, w, tbl):
    T = ids.shape[0]
    L, CH, RB, TBK = SC_LANES, SC_GATHER_ROWS, SC_ROW_BLOCK, SC_TOKEN_BLOCK
    assert T % (SC_WORKERS * TBK) == 0
    tpw = T // SC_WORKERS
    mesh = plsc.VectorSubcoreMesh(core_axis_name="c", subcore_axis_name="s")
    rows_buf = pltpu.VMEM((CH, ROW_WORDS), jnp.uint32)
    y_buf = pltpu.VMEM((D_MODEL,), jnp.float32)

    @functools.partial(
        pl.kernel, mesh=mesh, compiler_params=_sc_params(),
        out_type=jax.ShapeDtypeStruct((T, D_MODEL), jnp.float32),
        scratch_types=[pltpu.VMEM((TBK, N_SLOTS), jnp.int32), pltpu.VMEM((TBK, N_SLOTS), jnp.float32),
                       rows_buf, rows_buf, y_buf, y_buf] + [pltpu.SemaphoreType.DMA] * 4,
    )
    def up_kernel(ids_hbm, w_hbm, tbl_hbm, out_hbm, ids_v, w_v, rows0, rows1, y0, y1, g0, g1, o0, o1):
        base = (lax.axis_index("s") * SC_CORES + lax.axis_index("c")) * tpw
        rows, gsem, ys, osem = (rows0, rows1), (g0, g1), (y0, y1), (o0, o1)

        def gather(tl, half):
            return pltpu.make_async_copy(tbl_hbm.at[ids_v.at[tl, pl.ds(half * CH, CH)]], rows[half], gsem[half])

        def out_copy(tok, slot):
            return pltpu.make_async_copy(ys[slot], out_hbm.at[tok], osem[slot])

        def accumulate(tl, half, y_v):
            rows_v = rows[half]

            def block(b, carry):
                r0 = b * RB
                tlv = jnp.full((L,), tl, jnp.int32)
                wks = [plsc.load_gather(w_v, [tlv, jnp.full((L,), half * CH + r0 + r, jnp.int32)])
                       for r in range(RB)]
                for q in range(ROW_WORDS // (SC_COL_GROUP * L)):
                    acc = [None] * (2 * SC_COL_GROUP)
                    for r in range(RB):
                        for i in range(SC_COL_GROUP):
                            v = rows_v[r0 + r, pl.ds((q * SC_COL_GROUP + i) * L, L)]
                            lo = plsc.bitcast(v << 16, jnp.float32) * wks[r]
                            hi = plsc.bitcast(v & jnp.uint32(0xFFFF0000), jnp.float32) * wks[r]
                            acc[2 * i] = lo if r == 0 else acc[2 * i] + lo
                            acc[2 * i + 1] = hi if r == 0 else acc[2 * i + 1] + hi
                    for i in range(SC_COL_GROUP):
                        col = (q * SC_COL_GROUP + i) * L
                        plsc.addupdate(y_v.at[pl.ds(col, L)], acc[2 * i])
                        plsc.addupdate(y_v.at[pl.ds(ROW_WORDS + col, L)], acc[2 * i + 1])
                return carry

            lax.fori_loop(0, CH // RB, block, 0)

        def token_block(bi, carry):
            tok0 = base + bi * TBK
            pltpu.sync_copy(ids_hbm.at[pl.ds(tok0, TBK)], ids_v)
            pltpu.sync_copy(w_hbm.at[pl.ds(tok0, TBK)], w_v)
            gather(0, 0).start()

            def token_pair(pi, carry2):
                for slot in range(2):
                    tl = 2 * pi + slot
                    y_v = ys[slot]

                    @pl.when(pi > 0)
                    def _():
                        out_copy(tok0 + tl - 2, slot).wait()

                    zero = jnp.zeros((L,), jnp.float32)
                    for j in range(D_MODEL // L):
                        y_v[pl.ds(j * L, L)] = zero
                    gather(tl, 1).start()
                    gather(tl, 0).wait()
                    accumulate(tl, 0, y_v)

                    @pl.when(tl + 1 < TBK)
                    def _():
                        gather(tl + 1, 0).start()

                    gather(tl, 1).wait()
                    accumulate(tl, 1, y_v)
                    out_copy(tok0 + tl, slot).start()
                return carry2

            lax.fori_loop(0, TBK // 2, token_pair, 0)
            out_copy(tok0 + TBK - 2, 0).wait()
            out_copy(tok0 + TBK - 1, 1).wait()
            return carry

        lax.fori_loop(0, tpw // TBK, token_block, 0)

    return up_kernel(ids, w, tbl)


def peer_layer(h, g, w_query, sub_keys, down_rows, up_rows):
    T = h.shape[0]
    q, xn = norm_matmul(h, g, w_query, out_dtype=jnp.float32, emit_xn=True)
    idx_t, gate_t = peer_route(q, sub_keys, precision=lax.Precision.HIGHEST)
    idx = idx_t.T
    w_t = peer_down(idx, xn.reshape(T, SUBLANES, LANES), gate_t, down_rows)
    return peer_up_sc(idx // ROW_SUB, w_t.T, up_rows)


def _rms_kernel(x_ref, r_ref, g_ref, o_ref):
    x = x_ref[...] + r_ref[...]
    o_ref[...] = x * lax.rsqrt(jnp.mean(x * x, axis=-1, keepdims=True) + NORM_EPS) * g_ref[...]


def rms_norm_sum(x, r, g, *, tm=512):
    T, D = x.shape
    tm = min(tm, T)
    row = pl.BlockSpec((tm, D), lambda i: (i, 0))
    return pl.pallas_call(
        _rms_kernel,
        grid=(T // tm,),
        in_specs=[row, row, pl.BlockSpec((1, D), lambda i: (0, 0))],
        out_specs=row,
        out_shape=jax.ShapeDtypeStruct((T, D), jnp.float32),
        compiler_params=_cparams(("parallel",)),
        name="final_norm",
    )(x, r, g.reshape(1, D))


N_PIECES = 4


def kernel(x, rel_table, mix_norm_g, attn_w_in, attn_w_out, attn_sink, sg_w_in, sg_ln_g, sg_ln_b, sg_w_spatial, sg_b_spatial, sg_w_out, ffn_norm_g, peer_w_query, peer_sub_keys, peer_down, peer_up, final_norm_g):
    B, S, D = x.shape
    bf16 = jnp.bfloat16
    n_pieces = N_PIECES if B % N_PIECES == 0 else 1
    Bp = B // n_pieces
    Tp = Bp * S

    bias = attn_bias(rel_table)
    attn_in, attn_out = attn_w_in[0].astype(bf16), attn_w_out[0].astype(bf16)
    sg_in, sg_out = sg_w_in[0].astype(bf16), sg_w_out[0].astype(bf16)
    w_query = peer_w_query.astype(bf16)
    down_rows = [pack_table(peer_down[i]).reshape(-1, LANES) for i in range(2)]
    up_rows = [pack_table(peer_up[i]) for i in range(2)]

    mid = []
    for p in range(n_pieces):
        h = x[p * Bp:(p + 1) * Bp].reshape(Tp, D)
        qkv, = norm_matmul(h, mix_norm_g[0], attn_in, out_dtype=bf16)
        att = window_attention(qkv.reshape(Bp, S, QKV_DIM), bias, attn_sink[0])
        h = matmul_residual(att.reshape(Tp, Q_DIM), attn_out, h)
        mid.append((h, peer_layer(h, ffn_norm_g[0], w_query[0], peer_sub_keys[0], down_rows[0], up_rows[0])))

    outs = []
    for h, y in mid:
        z, h = norm_matmul(h, mix_norm_g[1], sg_in, res=y, out_dtype=bf16, act=True, tn=1536)
        gated = spatial_gate(z, sg_ln_g[0], sg_ln_b[0], sg_w_spatial[0], sg_b_spatial[0])
        h = matmul_residual(gated, sg_out, h)
        y = peer_layer(h, ffn_norm_g[1], w_query[1], peer_sub_keys[1], down_rows[1], up_rows[1])
        outs.append(rms_norm_sum(h, y, final_norm_g).reshape(Bp, S, D))
    return jnp.concatenate(outs, axis=0)
```

```python
import dataclasses
import functools
import math

import jax
import jax.numpy as jnp
from jax import lax
from jax.experimental import pallas as pl
from jax.experimental.pallas import tpu as pltpu
from jax.experimental.pallas import tpu_sc as plsc

D_MODEL = 1024
HEAD_DIM = 64
N_Q_HEADS = 16
N_KV_HEADS = 4
GQA_GROUP = 4
WINDOW = 128
BLOCK = 128
REL_BUCKETS = 32
REL_MAX_DIST = 128
Q_DIM = N_Q_HEADS * HEAD_DIM
KV_DIM = N_KV_HEADS * HEAD_DIM
QKV_DIM = Q_DIM + 2 * KV_DIM
CHUNK = 128
D_GATE = 3072
N_SG_GROUPS = 8
SG_GROUP_DIM = D_GATE // N_SG_GROUPS
N_KEYS = 128
PEER_HEADS = 8
PEER_TOPK = 16
D_HALF = 128
N_SLOTS = PEER_HEADS * PEER_TOPK
NORM_EPS = 1e-6
LN_EPS = 1e-5

LANES = 128
SUBLANES = 8
ROW_WORDS = D_MODEL // 2
ROW_SUB = ROW_WORDS // LANES
PLANE_STRIDE = 136
VMEM_LIMIT = 56 * 1024 * 1024

_GELU_C = math.sqrt(2.0 / math.pi)


def _gelu(x):
    return 0.5 * x * (1.0 + jnp.tanh(_GELU_C * (x + 0.044715 * (x * x * x))))


def _cparams(sem):
    return pltpu.CompilerParams(dimension_semantics=sem, vmem_limit_bytes=VMEM_LIMIT)


def _norm_matmul_kernel(*refs, act, emit_xn, add_res, precision):
    refs = list(refs)
    x_ref = refs.pop(0)
    r_ref = refs.pop(0) if add_res else None
    g_ref, w_ref, o_ref = refs.pop(0), refs.pop(0), refs.pop(0)
    xn_ref = refs.pop(0) if emit_xn else None
    sum_ref = refs.pop(0) if add_res else None
    xs_ref = refs.pop(0)

    @pl.when(pl.program_id(1) == 0)
    def _():
        x = x_ref[...]
        if add_res:
            x = x + r_ref[...]
            sum_ref[...] = x
        y = x * lax.rsqrt(jnp.mean(x * x, axis=-1, keepdims=True) + NORM_EPS) * g_ref[...]
        xs_ref[...] = y.astype(xs_ref.dtype)
        if emit_xn:
            xn_ref[...] = y

    acc = jnp.dot(xs_ref[...], w_ref[...], preferred_element_type=jnp.float32,
                  precision=precision)
    if act:
        acc = _gelu(acc)
    o_ref[...] = acc.astype(o_ref.dtype)


def norm_matmul(x, g, w, *, out_dtype, res=None, act=False, emit_xn=False, tm=512, tn=None,
                precision=None):
    T, D = x.shape
    N = w.shape[1]
    tn = tn or N
    tm = min(tm, T)
    row = pl.BlockSpec((tm, D), lambda i, j: (i, 0))
    out_shape = [jax.ShapeDtypeStruct((T, N), out_dtype)]
    out_specs = [pl.BlockSpec((tm, tn), lambda i, j: (i, j))]
    for flag in (emit_xn, res is not None):
        if flag:
            out_shape.append(jax.ShapeDtypeStruct((T, D), jnp.float32))
            out_specs.append(row)
    args = [x] + ([res] if res is not None else []) + [g.reshape(1, D), w]
    in_specs = [row] * (len(args) - 2) + [pl.BlockSpec((1, D), lambda i, j: (0, 0)),
                                          pl.BlockSpec((D, tn), lambda i, j: (0, j))]
    return pl.pallas_call(
        functools.partial(_norm_matmul_kernel, act=act, emit_xn=emit_xn, add_res=res is not None,
                          precision=precision),
        grid=(T // tm, N // tn),
        in_specs=in_specs,
        out_specs=out_specs,
        out_shape=out_shape,
        scratch_shapes=[pltpu.VMEM((tm, D), w.dtype)],
        compiler_params=_cparams(("parallel", "arbitrary")),
        name="norm_matmul",
    )(*args)


def _matmul_res_kernel(a_ref, w_ref, h_ref, o_ref):
    o_ref[...] = h_ref[...] + jnp.dot(a_ref[...], w_ref[...],
                                      preferred_element_type=jnp.float32)


def matmul_residual(a, w, h, *, tm=512):
    T, K = a.shape
    N = w.shape[1]
    tm = min(tm, T)
    return pl.pallas_call(
        _matmul_res_kernel,
        grid=(T // tm,),
        in_specs=[pl.BlockSpec((tm, K), lambda i: (i, 0)),
                  pl.BlockSpec((K, N), lambda i: (0, 0)),
                  pl.BlockSpec((tm, N), lambda i: (i, 0))],
        out_specs=pl.BlockSpec((tm, N), lambda i: (i, 0)),
        out_shape=jax.ShapeDtypeStruct((T, N), jnp.float32),
        compiler_params=_cparams(("parallel",)),
        name="matmul_residual",
    )(a, w, h)


def _t5_bucket(rel):
    nb = REL_BUCKETS // 2
    max_exact = nb // 2
    ret = jnp.where(rel > 0, nb, 0)
    n = jnp.abs(rel)
    nf = jnp.maximum(n, 1).astype(jnp.float32)
    large = max_exact + (jnp.log(nf / max_exact) / math.log(REL_MAX_DIST / max_exact)
                         * (nb - max_exact)).astype(jnp.int32)
    large = jnp.minimum(large, nb - 1)
    return (ret + jnp.where(n < max_exact, n, large)).astype(jnp.int32)


def _bias_kernel(bucket_ref, window_ref, table_ref, o_ref):
    bucket = bucket_ref[...]
    in_window = window_ref[...] > 0
    for hq in range(N_Q_HEADS):
        acc = jnp.zeros(bucket.shape, jnp.float32)
        for b in range(REL_BUCKETS):
            acc = jnp.where(bucket == b, table_ref[b, hq], acc)
        o_ref[hq] = jnp.where(in_window, acc, -jnp.inf)


def attn_bias(rel_table):
    qi = jnp.arange(BLOCK)[:, None]
    kj = jnp.arange(3 * BLOCK)[None, :]
    rel = kj - BLOCK - qi
    bucket = _t5_bucket(rel)
    window = (jnp.abs(rel) <= WINDOW).astype(jnp.int32)
    return pl.pallas_call(
        _bias_kernel,
        in_specs=[pl.BlockSpec(memory_space=pltpu.VMEM),
                  pl.BlockSpec(memory_space=pltpu.VMEM),
                  pl.BlockSpec(memory_space=pltpu.SMEM)],
        out_specs=pl.BlockSpec(memory_space=pltpu.VMEM),
        out_shape=jax.ShapeDtypeStruct((N_Q_HEADS, BLOCK, 3 * BLOCK), jnp.float32),
        name="attn_bias",
    )(bucket, window, rel_table)


def _attn_kernel(cur_ref, prev_ref, next_ref, bias_ref, sink_ref, o_ref):
    i = pl.program_id(1)
    nb = pl.num_programs(1)
    q = cur_ref[0, :, 0:Q_DIM]
    kband = jnp.concatenate([prev_ref[0, :, 0:KV_DIM], cur_ref[0, :, Q_DIM:Q_DIM + KV_DIM],
                             next_ref[0, :, 0:KV_DIM]], axis=0)
    vband = jnp.concatenate([prev_ref[0, :, KV_DIM:2 * KV_DIM], cur_ref[0, :, Q_DIM + KV_DIM:QKV_DIM],
                             next_ref[0, :, KV_DIM:2 * KV_DIM]], axis=0)
    col = lax.broadcasted_iota(jnp.int32, (1, 3 * BLOCK), 1)
    valid = jnp.logical_and(jnp.logical_or(col >= BLOCK, i > 0),
                            jnp.logical_or(col < 2 * BLOCK, i < nb - 1))
    heads = range(N_Q_HEADS)
    kv = [(kband[:, hk * HEAD_DIM:(hk + 1) * HEAD_DIM], vband[:, hk * HEAD_DIM:(hk + 1) * HEAD_DIM])
          for hk in range(N_KV_HEADS)]
    s = []
    for hq in heads:
        qh = q[:, hq * HEAD_DIM:(hq + 1) * HEAD_DIM]
        sc = lax.dot_general(qh, kv[hq // GQA_GROUP][0], (((1,), (1,)), ((), ())),
                             preferred_element_type=jnp.float32)
        s.append(jnp.where(valid, sc * (HEAD_DIM ** -0.5) + bias_ref[hq], -jnp.inf))
    mx = [jnp.maximum(jnp.max(s[hq], axis=-1, keepdims=True), sink_ref[hq]) for hq in heads]
    p = [jnp.exp(s[hq] - mx[hq]) for hq in heads]
    denom = [jnp.sum(p[hq], axis=-1, keepdims=True) + jnp.exp(sink_ref[hq] - mx[hq]) for hq in heads]
    outs = [jnp.dot((p[hq] / denom[hq]).astype(jnp.bfloat16), kv[hq // GQA_GROUP][1],
                    preferred_element_type=jnp.float32) for hq in heads]
    o_ref[0] = jnp.concatenate(outs, axis=-1).astype(o_ref.dtype)


def window_attention(qkv, bias, sink):
    B, S, _ = qkv.shape
    nb = S // BLOCK
    kv_col = Q_DIM // (2 * KV_DIM)
    return pl.pallas_call(
        _attn_kernel,
        grid=(B, nb),
        in_specs=[pl.BlockSpec((1, BLOCK, QKV_DIM), lambda b, i: (b, i, 0)),
                  pl.BlockSpec((1, BLOCK, 2 * KV_DIM),
                               lambda b, i: (b, jnp.maximum(i - 1, 0), kv_col)),
                  pl.BlockSpec((1, BLOCK, 2 * KV_DIM),
                               lambda b, i: (b, jnp.minimum(i + 1, nb - 1), kv_col)),
                  pl.BlockSpec((N_Q_HEADS, BLOCK, 3 * BLOCK), lambda b, i: (0, 0, 0)),
                  pl.BlockSpec(memory_space=pltpu.SMEM)],
        out_specs=pl.BlockSpec((1, BLOCK, Q_DIM), lambda b, i: (b, i, 0)),
        out_shape=jax.ShapeDtypeStruct((B, S, Q_DIM), jnp.bfloat16),
        compiler_params=_cparams(("parallel", "arbitrary")),
        name="window_attention",
    )(qkv, qkv, qkv, bias, sink)


def _spatial_gate_kernel(z_ref, g_ref, b_ref, wsp_ref, bsp_ref, o_ref):
    v = z_ref[:, D_GATE:2 * D_GATE].astype(jnp.float32)
    mu = jnp.mean(v, axis=-1, keepdims=True)
    vc = v - mu
    var = jnp.mean(vc * vc, axis=-1, keepdims=True)
    vn = (vc * lax.rsqrt(var + LN_EPS) * g_ref[...] + b_ref[...]).astype(jnp.bfloat16)
    for grp in range(N_SG_GROUPS):
        lo, hi = grp * SG_GROUP_DIM, (grp + 1) * SG_GROUP_DIM
        mixed = jnp.dot(wsp_ref[grp], vn[:, lo:hi], preferred_element_type=jnp.float32)
        mixed = mixed + bsp_ref[:, grp:grp + 1]
        u = z_ref[:, lo:hi].astype(jnp.float32)
        o_ref[:, lo:hi] = (u * mixed).astype(o_ref.dtype)


def spatial_gate(z, ln_g, ln_b, w_sp, b_sp):
    T = z.shape[0]
    return pl.pallas_call(
        _spatial_gate_kernel,
        grid=(T // CHUNK,),
        in_specs=[pl.BlockSpec((CHUNK, 2 * D_GATE), lambda i: (i, 0)),
                  pl.BlockSpec((1, D_GATE), lambda i: (0, 0)),
                  pl.BlockSpec((1, D_GATE), lambda i: (0, 0)),
                  pl.BlockSpec((N_SG_GROUPS, CHUNK, CHUNK), lambda i: (0, 0, 0)),
                  pl.BlockSpec((CHUNK, N_SG_GROUPS), lambda i: (0, 0))],
        out_specs=pl.BlockSpec((CHUNK, D_GATE), lambda i: (i, 0)),
        out_shape=jax.ShapeDtypeStruct((T, D_GATE), jnp.bfloat16),
        compiler_params=_cparams(("parallel",)),
        name="spatial_gate",
    )(z, ln_g.reshape(1, D_GATE), ln_b.reshape(1, D_GATE), w_sp.astype(jnp.bfloat16), b_sp.T)


def _oddeven_merge_sort_pairs(n):
    pairs = []
    p = 1
    while p < n:
        k = p
        while k >= 1:
            for j in range(k % p, n - k, 2 * k):
                for i in range(min(k, n - j - k)):
                    if (i + j) // (2 * p) == (i + j + k) // (2 * p):
                        pairs.append((i + j, i + j + k))
            k //= 2
        p *= 2
    return pairs


_SORT16 = _oddeven_merge_sort_pairs(N_KEYS // SUBLANES)


def _top16_of_keys(s):
    nv = N_KEYS // SUBLANES
    L = s.shape[1]
    sub = lax.broadcasted_iota(jnp.int32, (SUBLANES, L), 0)
    v = [s[j * SUBLANES:(j + 1) * SUBLANES] for j in range(nv)]
    ids = [sub + j * SUBLANES for j in range(nv)]
    for i, j in _SORT16:
        swap = jnp.logical_or(v[j] > v[i], jnp.logical_and(v[j] == v[i], ids[j] < ids[i]))
        v[i], v[j] = jnp.where(swap, v[j], v[i]), jnp.where(swap, v[i], v[j])
        ids[i], ids[j] = jnp.where(swap, ids[j], ids[i]), jnp.where(swap, ids[i], ids[j])
    vals, picks = [], []
    for it in range(PEER_TOPK):
        m = jnp.max(v[0], axis=0, keepdims=True)
        am = jnp.min(jnp.where(v[0] == m, ids[0], N_KEYS), axis=0, keepdims=True)
        hit = ids[0] == am
        vals.append(m)
        picks.append(am)
        last = PEER_TOPK - 1 - it
        for j in range(last):
            v[j] = jnp.where(hit, v[j + 1], v[j])
            ids[j] = jnp.where(hit, ids[j + 1], ids[j])
        v[last] = jnp.where(hit, -jnp.inf, v[last])
    return jnp.concatenate(vals, axis=0), jnp.concatenate(picks, axis=0)


_CAND_ROWS = (
    [(a, 0) for a in range(16)]
    + [None] + [(0, b) for b in range(1, 16)]
    + [None] + [(a, 1) for a in range(1, 8)]
    + [None, None] + [(1, b) for b in range(2, 8)]
    + [(2, 2), (3, 2), (4, 2), (2, 3), (2, 4), (3, 3), None, None]
)
assert sorted(x for x in _CAND_ROWS if x) == sorted(
    (a, b) for a in range(16) for b in range(16) if (a + 1) * (b + 1) <= 16)


def _rows(x, picks):
    pieces, i = [], 0
    while i < len(picks):
        j = i
        while j + 1 < len(picks) and picks[j + 1] == picks[j] + 1:
            j += 1
        pieces.append(x[picks[i]:picks[j] + 1])
        i = j + 1
    return pieces[0] if len(pieces) == 1 else jnp.concatenate(pieces, axis=0)


def _cand_positions(L):
    row = lax.broadcasted_iota(jnp.int32, (len(_CAND_ROWS), L), 0)
    pos = jnp.full(row.shape, _PAD_POS, jnp.int32)
    for r, c in enumerate(_CAND_ROWS):
        if c is not None:
            pos = jnp.where(row == r, c[0] * PEER_TOPK + c[1], pos)
    return pos


_PAD_POS = 1 << 20


def _joint_top16(v1, i1, v2, i2, pos):
    a_of = [c[0] if c else 0 for c in _CAND_ROWS]
    b_of = [c[1] if c else 0 for c in _CAND_ROWS]
    cand = jnp.where(pos < _PAD_POS, _rows(v1, a_of) + _rows(v2, b_of), -jnp.inf)
    cidx = _rows(i1, a_of) * N_KEYS + _rows(i2, b_of)
    vals, picks = [], []
    for _ in range(PEER_TOPK):
        m = jnp.max(cand, axis=0, keepdims=True)
        pm = jnp.min(jnp.where(cand == m, pos, 1 << 21), axis=0, keepdims=True)
        hit = pos == pm
        vals.append(m)
        picks.append(jnp.max(jnp.where(hit, cidx, -1), axis=0, keepdims=True))
        cand = jnp.where(hit, -jnp.inf, cand)
    return jnp.concatenate(vals, axis=0), jnp.concatenate(picks, axis=0)


def _route_kernel(q_ref, keys_ref, idx_ref, gate_ref, *, precision):
    nt = (((1,), (1,)), ((), ()))
    pos = _cand_positions(LANES)
    for j in range(q_ref.shape[0] // LANES):
        q = q_ref[j * LANES:(j + 1) * LANES, :]
        s1 = lax.dot_general(keys_ref[0, 0], q[:, 0:D_HALF], nt,
                             preferred_element_type=jnp.float32, precision=precision)
        s2 = lax.dot_general(keys_ref[0, 1], q[:, D_HALF:2 * D_HALF], nt,
                             preferred_element_type=jnp.float32, precision=precision)
        v1, i1 = _top16_of_keys(s1)
        v2, i2 = _top16_of_keys(s2)
        top_s, top_i = _joint_top16(v1, i1, v2, i2, pos)
        e = jnp.exp(top_s - top_s[0:1])
        gate_ref[:, j * LANES:(j + 1) * LANES] = e / jnp.sum(e, axis=0, keepdims=True)
        idx_ref[:, j * LANES:(j + 1) * LANES] = top_i * ROW_SUB


def peer_route(q, sub_keys, *, tb=512, precision=None):
    T = q.shape[0]
    tb = min(tb, T)
    return pl.pallas_call(
        functools.partial(_route_kernel, precision=precision),
        grid=(T // tb, PEER_HEADS),
        in_specs=[pl.BlockSpec((tb, 2 * D_HALF), lambda i, h: (i, h)),
                  pl.BlockSpec((1, 2, N_KEYS, D_HALF), lambda i, h: (h, 0, 0, 0))],
        out_specs=[pl.BlockSpec((PEER_TOPK, tb), lambda i, h: (h, i)),
                   pl.BlockSpec((PEER_TOPK, tb), lambda i, h: (h, i))],
        out_shape=[jax.ShapeDtypeStruct((N_SLOTS, T), jnp.int32),
                   jax.ShapeDtypeStruct((N_SLOTS, T), jnp.float32)],
        compiler_params=_cparams(("parallel", "arbitrary")),
        name="peer_route",
    )(q, sub_keys)


def pack_table(tbl):
    D = tbl.shape[1]
    bits = lax.bitcast_convert_type(tbl.astype(jnp.bfloat16), jnp.uint16).astype(jnp.uint32)
    return bits[:, :D // 2] | (bits[:, D // 2:] << 16)


def _unpack(words):
    lo = pltpu.bitcast(words << 16, jnp.float32)
    hi = pltpu.bitcast(words & jnp.uint32(0xFFFF0000), jnp.float32)
    return lo, hi


def _gather_rows(idx_ref, tbl_ref, t, planes_ref):
    for k in range(N_SLOTS):
        off = pl.multiple_of(idx_ref[t, k], ROW_SUB)
        planes_ref[pl.ds(k, ROW_SUB, stride=PLANE_STRIDE), :] = tbl_ref[pl.ds(off, ROW_SUB), :]


def _plane(planes_ref, c):
    return _unpack(planes_ref[c * PLANE_STRIDE:c * PLANE_STRIDE + N_SLOTS, :])


def _two_stage_tokens(tb, gather, compute, planes_a, planes_b, before_pair=None):
    gather(0, planes_a)

    def pair(i, carry):
        t0 = 2 * i
        if before_pair is not None:
            before_pair(t0)
        gather(t0 + 1, planes_b)
        compute(t0, planes_a, 0)
        gather(jnp.minimum(t0 + 2, tb - 1), planes_a)
        compute(t0 + 1, planes_b, 1)
        return carry

    lax.fori_loop(0, tb // 2, pair, 0)


def _peer_down_kernel(idx_ref, x_ref, gate_ref, tbl_ref, o_ref, planes_a, planes_b, part_ref):
    tb = x_ref.shape[0]
    lane = lax.broadcasted_iota(jnp.int32, (N_SLOTS, tb), 1)

    def compute(t, planes_ref, slot):
        acc = jnp.zeros((N_SLOTS, LANES), jnp.float32)
        for c in range(ROW_SUB):
            lo, hi = _plane(planes_ref, c)
            acc = acc + lo * x_ref[t, c:c + 1, :] + hi * x_ref[t, ROW_SUB + c:ROW_SUB + c + 1, :]
        part_ref[slot] = acc

    def place_pair(t0):
        col_a = jnp.sum(part_ref[0], axis=1, keepdims=True)
        col_b = jnp.sum(part_ref[1], axis=1, keepdims=True)
        o_ref[...] = jnp.where(lane == t0 - 2, col_a, jnp.where(lane == t0 - 1, col_b, o_ref[...]))

    o_ref[...] = jnp.zeros(o_ref.shape, jnp.float32)
    part_ref[...] = jnp.zeros(part_ref.shape, jnp.float32)
    _two_stage_tokens(tb, functools.partial(_gather_rows, idx_ref, tbl_ref), compute, planes_a, planes_b,
                      before_pair=place_pair)
    place_pair(tb)
    o_ref[...] = _gelu(o_ref[...]) * gate_ref[...]


def _expert_call(kernel_fn, T, tb, in_specs, out_spec, out_shape, name, extra_scratch=()):
    planes = pltpu.VMEM((ROW_SUB * PLANE_STRIDE, LANES), jnp.uint32)
    return pl.pallas_call(
        kernel_fn,
        grid=(T // tb,),
        in_specs=in_specs,
        out_specs=out_spec,
        out_shape=out_shape,
        scratch_shapes=[planes, planes, *extra_scratch],
        compiler_params=_cparams(("arbitrary",)),
        name=name,
    )


def peer_down(idx, xn3, gate_t, tbl, *, tb=128):
    T = xn3.shape[0]
    tb = min(tb, T)
    return _expert_call(
        _peer_down_kernel, T, tb,
        [pl.BlockSpec((tb, N_SLOTS), lambda i: (i, 0), memory_space=pltpu.SMEM),
         pl.BlockSpec((tb, SUBLANES, LANES), lambda i: (i, 0, 0)),
         pl.BlockSpec((N_SLOTS, tb), lambda i: (0, i)),
         pl.BlockSpec(tbl.shape, lambda i: (0, 0), pipeline_mode=pl.Buffered(1))],
        pl.BlockSpec((N_SLOTS, tb), lambda i: (0, i)),
        jax.ShapeDtypeStruct((N_SLOTS, T), jnp.float32),
        "peer_down",
        extra_scratch=[pltpu.VMEM((2, N_SLOTS, LANES), jnp.float32)],
    )(idx, xn3, gate_t, tbl)


def _peer_up_kernel(idx_ref, w_ref, h_ref, tbl_ref, o_ref, planes_a, planes_b):
    tb = h_ref.shape[0]
    lane = lax.broadcasted_iota(jnp.int32, (N_SLOTS, tb), 1)

    def compute(t, planes_ref, slot):
        col = jnp.sum(jnp.where(lane == t, w_ref[...], 0.0), axis=1, keepdims=True)
        wb = jnp.broadcast_to(col, (N_SLOTS, LANES))
        y_lo, y_hi = [], []
        for c in range(ROW_SUB):
            lo, hi = _plane(planes_ref, c)
            y_lo.append(jnp.sum(lo * wb, axis=0, keepdims=True))
            y_hi.append(jnp.sum(hi * wb, axis=0, keepdims=True))
        o_ref[t] = h_ref[t] + jnp.concatenate(y_lo + y_hi, axis=0)

    _two_stage_tokens(tb, functools.partial(_gather_rows, idx_ref, tbl_ref), compute, planes_a, planes_b)


def peer_up(idx, w_t, h3, tbl, *, tb=128):
    T = h3.shape[0]
    tb = min(tb, T)
    return _expert_call(
        _peer_up_kernel, T, tb,
        [pl.BlockSpec((tb, N_SLOTS), lambda i: (i, 0), memory_space=pltpu.SMEM),
         pl.BlockSpec((N_SLOTS, tb), lambda i: (0, i)),
         pl.BlockSpec((tb, SUBLANES, LANES), lambda i: (i, 0, 0)),
         pl.BlockSpec(tbl.shape, lambda i: (0, 0), pipeline_mode=pl.Buffered(1))],
        pl.BlockSpec((tb, SUBLANES, LANES), lambda i: (i, 0, 0)),
        jax.ShapeDtypeStruct(h3.shape, jnp.float32),
        "peer_up",
    )(idx, w_t, h3, tbl)


SC_CORES = 2
SC_SUBCORES = 16
SC_WORKERS = SC_CORES * SC_SUBCORES
SC_LANES = 16
SC_GATHER_ROWS = N_SLOTS // 2
SC_ROW_BLOCK = 8
SC_TOKEN_BLOCK = 32
SC_COL_GROUP = 8


def _sc_params():
    cp = pltpu.CompilerParams()
    if "needs_layout_passes" in pltpu.CompilerParams.__dataclass_fields__:
        cp = dataclasses.replace(cp, needs_layout_passes=False)
    return cp


def peer_up_sc(ids, w, tbl):
    T = ids.shape[0]
    L, CH, RB, TBK = SC_LANES, SC_GATHER_ROWS, SC_ROW_BLOCK, SC_TOKEN_BLOCK
    assert T % (SC_WORKERS * TBK) == 0
    tpw = T // SC_WORKERS
    mesh = plsc.VectorSubcoreMesh(core_axis_name="c", subcore_axis_name="s")
    rows_buf = pltpu.VMEM((CH, ROW_WORDS), jnp.uint32)
    y_buf = pltpu.VMEM((D_MODEL,), jnp.float32)

    @functools.partial(
        pl.kernel, mesh=mesh, compiler_params=_sc_params(),
        out_type=jax.ShapeDtypeStruct((T, D_MODEL), jnp.float32),
        scratch_types=[pltpu.VMEM((TBK, N_SLOTS), jnp.int32), pltpu.VMEM((TBK, N_SLOTS), jnp.float32),
                       rows_buf, rows_buf, y_buf, y_buf] + [pltpu.SemaphoreType.DMA] * 4,
    )
    def up_kernel(ids_hbm, w_hbm, tbl_hbm, out_hbm, ids_v, w_v, rows0, rows1, y0, y1, g0, g1, o0, o1):
        base = (lax.axis_index("s") * SC_CORES + lax.axis_index("c")) * tpw
        rows, gsem, ys, osem = (rows0, rows1), (g0, g1), (y0, y1), (o0, o1)

        def gather(tl, half):
            return pltpu.make_async_copy(tbl_hbm.at[ids_v.at[tl, pl.ds(half * CH, CH)]], rows[half], gsem[half])

        def out_copy(tok, slot):
            return pltpu.make_async_copy(ys[slot], out_hbm.at[tok], osem[slot])

        def accumulate(tl, half, y_v):
            rows_v = rows[half]

            def block(b, carry):
                r0 = b * RB
                tlv = jnp.full((L,), tl, jnp.int32)
                wks = [plsc.load_gather(w_v, [tlv, jnp.full((L,), half * CH + r0 + r, jnp.int32)])
                       for r in range(RB)]
                for q in range(ROW_WORDS // (SC_COL_GROUP * L)):
                    acc = [None] * (2 * SC_COL_GROUP)
                    for r in range(RB):
                        for i in range(SC_COL_GROUP):
                            v = rows_v[r0 + r, pl.ds((q * SC_COL_GROUP + i) * L, L)]
                            lo = plsc.bitcast(v << 16, jnp.float32) * wks[r]
                            hi = plsc.bitcast(v & jnp.uint32(0xFFFF0000), jnp.float32) * wks[r]
                            acc[2 * i] = lo if r == 0 else acc[2 * i] + lo
                            acc[2 * i + 1] = hi if r == 0 else acc[2 * i + 1] + hi
                    for i in range(SC_COL_GROUP):
                        col = (q * SC_COL_GROUP + i) * L
                        plsc.addupdate(y_v.at[pl.ds(col, L)], acc[2 * i])
                        plsc.addupdate(y_v.at[pl.ds(ROW_WORDS + col, L)], acc[2 * i + 1])
                return carry

            lax.fori_loop(0, CH // RB, block, 0)

        def token_block(bi, carry):
            tok0 = base + bi * TBK
            pltpu.sync_copy(ids_hbm.at[pl.ds(tok0, TBK)], ids_v)
            pltpu.sync_copy(w_hbm.at[pl.ds(tok0, TBK)], w_v)
            gather(0, 0).start()

            def token_pair(pi, carry2):
                for slot in range(2):
                    tl = 2 * pi + slot
                    y_v = ys[slot]

                    @pl.when(pi > 0)
                    def _():
                        out_copy(tok0 + tl - 2, slot).wait()

                    zero = jnp.zeros((L,), jnp.float32)
                    for j in range(D_MODEL // L):
                        y_v[pl.ds(j * L, L)] = zero
                    gather(tl, 1).start()
                    gather(tl, 0).wait()
                    accumulate(tl, 0, y_v)

                    @pl.when(tl + 1 < TBK)
                    def _():
                        gather(tl + 1, 0).start()

                    gather(tl, 1).wait()
                    accumulate(tl, 1, y_v)
                    out_copy(tok0 + tl, slot).start()
                return carry2

            lax.fori_loop(0, TBK // 2, token_pair, 0)
            out_copy(tok0 + TBK - 2, 0).wait()
            out_copy(tok0 + TBK - 1, 1).wait()
            return carry

        lax.fori_loop(0, tpw // TBK, token_block, 0)

    return up_kernel(ids, w, tbl)


def peer_layer(h, g, w_query, sub_keys, down_rows, up_rows):
    T = h.shape[0]
    q, xn = norm_matmul(h, g, w_query, out_dtype=jnp.float32, emit_xn=True)
    idx_t, gate_t = peer_route(q, sub_keys, precision=lax.Precision.HIGHEST)
    idx = idx_t.T
    w_t = peer_down(idx, xn.reshape(T, SUBLANES, LANES), gate_t, down_rows)
    return peer_up_sc(idx // ROW_SUB, w_t.T, up_rows)


def _rms_kernel(x_ref, r_ref, g_ref, o_ref):
    x = x_ref[...] + r_ref[...]
    o_ref[...] = x * lax.rsqrt(jnp.mean(x * x, axis=-1, keepdims=True) + NORM_EPS) * g_ref[...]


def rms_norm_sum(x, r, g, *, tm=512):
    T, D = x.shape
    tm = min(tm, T)
    row = pl.BlockSpec((tm, D), lambda i: (i, 0))
    return pl.pallas_call(
        _rms_kernel,
        grid=(T // tm,),
        in_specs=[row, row, pl.BlockSpec((1, D), lambda i: (0, 0))],
        out_specs=row,
        out_shape=jax.ShapeDtypeStruct((T, D), jnp.float32),
        compiler_params=_cparams(("parallel",)),
        name="final_norm",
    )(x, r, g.reshape(1, D))


N_PIECES = 8


def kernel(x, rel_table, mix_norm_g, attn_w_in, attn_w_out, attn_sink, sg_w_in, sg_ln_g, sg_ln_b, sg_w_spatial, sg_b_spatial, sg_w_out, ffn_norm_g, peer_w_query, peer_sub_keys, peer_down, peer_up, final_norm_g):
    B, S, D = x.shape
    bf16 = jnp.bfloat16
    n_pieces = N_PIECES if B % N_PIECES == 0 else 1
    Bp = B // n_pieces
    Tp = Bp * S

    bias = attn_bias(rel_table)
    attn_in, attn_out = attn_w_in[0].astype(bf16), attn_w_out[0].astype(bf16)
    sg_in, sg_out = sg_w_in[0].astype(bf16), sg_w_out[0].astype(bf16)
    w_query = peer_w_query.astype(bf16)
    down_rows = [pack_table(peer_down[i]).reshape(-1, LANES) for i in range(2)]
    up_rows = [pack_table(peer_up[i]) for i in range(2)]

    mid = []
    for p in range(n_pieces):
        h = x[p * Bp:(p + 1) * Bp].reshape(Tp, D)
        qkv, = norm_matmul(h, mix_norm_g[0], attn_in, out_dtype=bf16)
        att = window_attention(qkv.reshape(Bp, S, QKV_DIM), bias, attn_sink[0])
        h = matmul_residual(att.reshape(Tp, Q_DIM), attn_out, h)
        mid.append((h, peer_layer(h, ffn_norm_g[0], w_query[0], peer_sub_keys[0], down_rows[0], up_rows[0])))

    outs = []
    for h, y in mid:
        z, h = norm_matmul(h, mix_norm_g[1], sg_in, res=y, out_dtype=bf16, act=True, tn=1536)
        gated = spatial_gate(z, sg_ln_g[0], sg_ln_b[0], sg_w_spatial[0], sg_b_spatial[0])
        h = matmul_residual(gated, sg_out, h)
        y = peer_layer(h, ffn_norm_g[1], w_query[1], peer_sub_keys[1], down_rows[1], up_rows[1])
        outs.append(rms_norm_sum(h, y, final_norm_g).reshape(Bp, S, D))
    return jnp.concatenate(outs, axis=0)
```

```python
import dataclasses
import functools
import math

import jax
import jax.numpy as jnp
from jax import lax
from jax.experimental import pallas as pl
from jax.experimental.pallas import tpu as pltpu
from jax.experimental.pallas import tpu_sc as plsc

D_MODEL = 1024
HEAD_DIM = 64
N_Q_HEADS = 16
N_KV_HEADS = 4
GQA_GROUP = 4
WINDOW = 128
BLOCK = 128
REL_BUCKETS = 32
REL_MAX_DIST = 128
Q_DIM = N_Q_HEADS * HEAD_DIM
KV_DIM = N_KV_HEADS * HEAD_DIM
QKV_DIM = Q_DIM + 2 * KV_DIM
CHUNK = 128
D_GATE = 3072
N_SG_GROUPS = 8
SG_GROUP_DIM = D_GATE // N_SG_GROUPS
N_KEYS = 128
PEER_HEADS = 8
PEER_TOPK = 16
D_HALF = 128
N_SLOTS = PEER_HEADS * PEER_TOPK
NORM_EPS = 1e-6
LN_EPS = 1e-5

LANES = 128
SUBLANES = 8
ROW_WORDS = D_MODEL // 2
ROW_SUB = ROW_WORDS // LANES
PLANE_STRIDE = 136
VMEM_LIMIT = 56 * 1024 * 1024

_GELU_C = math.sqrt(2.0 / math.pi)


def _gelu(x):
    return 0.5 * x * (1.0 + jnp.tanh(_GELU_C * (x + 0.044715 * (x * x * x))))


def _cparams(sem):
    return pltpu.CompilerParams(dimension_semantics=sem, vmem_limit_bytes=VMEM_LIMIT)


def _norm_matmul_kernel(*refs, act, emit_xn, add_res, precision):
    refs = list(refs)
    x_ref = refs.pop(0)
    r_ref = refs.pop(0) if add_res else None
    g_ref, w_ref, o_ref = refs.pop(0), refs.pop(0), refs.pop(0)
    xn_ref = refs.pop(0) if emit_xn else None
    sum_ref = refs.pop(0) if add_res else None
    xs_ref = refs.pop(0)

    @pl.when(pl.program_id(1) == 0)
    def _():
        x = x_ref[...]
        if add_res:
            x = x + r_ref[...]
            sum_ref[...] = x
        y = x * lax.rsqrt(jnp.mean(x * x, axis=-1, keepdims=True) + NORM_EPS) * g_ref[...]
        xs_ref[...] = y.astype(xs_ref.dtype)
        if emit_xn:
            xn_ref[...] = y

    acc = jnp.dot(xs_ref[...], w_ref[...], preferred_element_type=jnp.float32,
                  precision=precision)
    if act:
        acc = _gelu(acc)
    o_ref[...] = acc.astype(o_ref.dtype)


def norm_matmul(x, g, w, *, out_dtype, res=None, act=False, emit_xn=False, tm=512, tn=None,
                precision=None):
    T, D = x.shape
    N = w.shape[1]
    tn = tn or N
    tm = min(tm, T)
    row = pl.BlockSpec((tm, D), lambda i, j: (i, 0))
    out_shape = [jax.ShapeDtypeStruct((T, N), out_dtype)]
    out_specs = [pl.BlockSpec((tm, tn), lambda i, j: (i, j))]
    for flag in (emit_xn, res is not None):
        if flag:
            out_shape.append(jax.ShapeDtypeStruct((T, D), jnp.float32))
            out_specs.append(row)
    args = [x] + ([res] if res is not None else []) + [g.reshape(1, D), w]
    in_specs = [row] * (len(args) - 2) + [pl.BlockSpec((1, D), lambda i, j: (0, 0)),
                                          pl.BlockSpec((D, tn), lambda i, j: (0, j))]
    return pl.pallas_call(
        functools.partial(_norm_matmul_kernel, act=act, emit_xn=emit_xn, add_res=res is not None,
                          precision=precision),
        grid=(T // tm, N // tn),
        in_specs=in_specs,
        out_specs=out_specs,
        out_shape=out_shape,
        scratch_shapes=[pltpu.VMEM((tm, D), w.dtype)],
        compiler_params=_cparams(("parallel", "arbitrary")),
        name="norm_matmul",
    )(*args)


def _matmul_res_kernel(a_ref, w_ref, h_ref, o_ref):
    o_ref[...] = h_ref[...] + jnp.dot(a_ref[...], w_ref[...],
                                      preferred_element_type=jnp.float32)


def matmul_residual(a, w, h, *, tm=512):
    T, K = a.shape
    N = w.shape[1]
    tm = min(tm, T)
    return pl.pallas_call(
        _matmul_res_kernel,
        grid=(T // tm,),
        in_specs=[pl.BlockSpec((tm, K), lambda i: (i, 0)),
                  pl.BlockSpec((K, N), lambda i: (0, 0)),
                  pl.BlockSpec((tm, N), lambda i: (i, 0))],
        out_specs=pl.BlockSpec((tm, N), lambda i: (i, 0)),
        out_shape=jax.ShapeDtypeStruct((T, N), jnp.float32),
        compiler_params=_cparams(("parallel",)),
        name="matmul_residual",
    )(a, w, h)


def _t5_bucket(rel):
    nb = REL_BUCKETS // 2
    max_exact = nb // 2
    ret = jnp.where(rel > 0, nb, 0)
    n = jnp.abs(rel)
    nf = jnp.maximum(n, 1).astype(jnp.float32)
    large = max_exact + (jnp.log(nf / max_exact) / math.log(REL_MAX_DIST / max_exact)
                         * (nb - max_exact)).astype(jnp.int32)
    large = jnp.minimum(large, nb - 1)
    return (ret + jnp.where(n < max_exact, n, large)).astype(jnp.int32)


def _bias_kernel(bucket_ref, window_ref, table_ref, o_ref):
    bucket = bucket_ref[...]
    in_window = window_ref[...] > 0
    for hq in range(N_Q_HEADS):
        acc = jnp.zeros(bucket.shape, jnp.float32)
        for b in range(REL_BUCKETS):
            acc = jnp.where(bucket == b, table_ref[b, hq], acc)
        o_ref[hq] = jnp.where(in_window, acc, -jnp.inf)


def attn_bias(rel_table):
    qi = jnp.arange(BLOCK)[:, None]
    kj = jnp.arange(3 * BLOCK)[None, :]
    rel = kj - BLOCK - qi
    bucket = _t5_bucket(rel)
    window = (jnp.abs(rel) <= WINDOW).astype(jnp.int32)
    return pl.pallas_call(
        _bias_kernel,
        in_specs=[pl.BlockSpec(memory_space=pltpu.VMEM),
                  pl.BlockSpec(memory_space=pltpu.VMEM),
                  pl.BlockSpec(memory_space=pltpu.SMEM)],
        out_specs=pl.BlockSpec(memory_space=pltpu.VMEM),
        out_shape=jax.ShapeDtypeStruct((N_Q_HEADS, BLOCK, 3 * BLOCK), jnp.float32),
        name="attn_bias",
    )(bucket, window, rel_table)


def _attn_kernel(cur_ref, prev_ref, next_ref, bias_ref, sink_ref, o_ref):
    i = pl.program_id(1)
    nb = pl.num_programs(1)
    q = cur_ref[0, :, 0:Q_DIM]
    kband = jnp.concatenate([prev_ref[0, :, 0:KV_DIM], cur_ref[0, :, Q_DIM:Q_DIM + KV_DIM],
                             next_ref[0, :, 0:KV_DIM]], axis=0)
    vband = jnp.concatenate([prev_ref[0, :, KV_DIM:2 * KV_DIM], cur_ref[0, :, Q_DIM + KV_DIM:QKV_DIM],
                             next_ref[0, :, KV_DIM:2 * KV_DIM]], axis=0)
    col = lax.broadcasted_iota(jnp.int32, (1, 3 * BLOCK), 1)
    valid = jnp.logical_and(jnp.logical_or(col >= BLOCK, i > 0),
                            jnp.logical_or(col < 2 * BLOCK, i < nb - 1))
    heads = range(N_Q_HEADS)
    kv = [(kband[:, hk * HEAD_DIM:(hk + 1) * HEAD_DIM], vband[:, hk * HEAD_DIM:(hk + 1) * HEAD_DIM])
          for hk in range(N_KV_HEADS)]
    s = []
    for hq in heads:
        qh = q[:, hq * HEAD_DIM:(hq + 1) * HEAD_DIM]
        sc = lax.dot_general(qh, kv[hq // GQA_GROUP][0], (((1,), (1,)), ((), ())),
                             preferred_element_type=jnp.float32)
        s.append(jnp.where(valid, sc * (HEAD_DIM ** -0.5) + bias_ref[hq], -jnp.inf))
    mx = [jnp.maximum(jnp.max(s[hq], axis=-1, keepdims=True), sink_ref[hq]) for hq in heads]
    p = [jnp.exp(s[hq] - mx[hq]) for hq in heads]
    denom = [jnp.sum(p[hq], axis=-1, keepdims=True) + jnp.exp(sink_ref[hq] - mx[hq]) for hq in heads]
    outs = [jnp.dot((p[hq] / denom[hq]).astype(jnp.bfloat16), kv[hq // GQA_GROUP][1],
                    preferred_element_type=jnp.float32) for hq in heads]
    o_ref[0] = jnp.concatenate(outs, axis=-1).astype(o_ref.dtype)


def window_attention(qkv, bias, sink):
    B, S, _ = qkv.shape
    nb = S // BLOCK
    kv_col = Q_DIM // (2 * KV_DIM)
    return pl.pallas_call(
        _attn_kernel,
        grid=(B, nb),
        in_specs=[pl.BlockSpec((1, BLOCK, QKV_DIM), lambda b, i: (b, i, 0)),
                  pl.BlockSpec((1, BLOCK, 2 * KV_DIM),
                               lambda b, i: (b, jnp.maximum(i - 1, 0), kv_col)),
                  pl.BlockSpec((1, BLOCK, 2 * KV_DIM),
                               lambda b, i: (b, jnp.minimum(i + 1, nb - 1), kv_col)),
                  pl.BlockSpec((N_Q_HEADS, BLOCK, 3 * BLOCK), lambda b, i: (0, 0, 0)),
                  pl.BlockSpec(memory_space=pltpu.SMEM)],
        out_specs=pl.BlockSpec((1, BLOCK, Q_DIM), lambda b, i: (b, i, 0)),
        out_shape=jax.ShapeDtypeStruct((B, S, Q_DIM), jnp.bfloat16),
        compiler_params=_cparams(("parallel", "arbitrary")),
        name="window_attention",
    )(qkv, qkv, qkv, bias, sink)


def _spatial_gate_kernel(z_ref, g_ref, b_ref, wsp_ref, bsp_ref, o_ref):
    v = z_ref[:, D_GATE:2 * D_GATE].astype(jnp.float32)
    mu = jnp.mean(v, axis=-1, keepdims=True)
    vc = v - mu
    var = jnp.mean(vc * vc, axis=-1, keepdims=True)
    vn = (vc * lax.rsqrt(var + LN_EPS) * g_ref[...] + b_ref[...]).astype(jnp.bfloat16)
    for grp in range(N_SG_GROUPS):
        lo, hi = grp * SG_GROUP_DIM, (grp + 1) * SG_GROUP_DIM
        mixed = jnp.dot(wsp_ref[grp], vn[:, lo:hi], preferred_element_type=jnp.float32)
        mixed = mixed + bsp_ref[:, grp:grp + 1]
        u = z_ref[:, lo:hi].astype(jnp.float32)
        o_ref[:, lo:hi] = (u * mixed).astype(o_ref.dtype)


def spatial_gate(z, ln_g, ln_b, w_sp, b_sp):
    T = z.shape[0]
    return pl.pallas_call(
        _spatial_gate_kernel,
        grid=(T // CHUNK,),
        in_specs=[pl.BlockSpec((CHUNK, 2 * D_GATE), lambda i: (i, 0)),
                  pl.BlockSpec((1, D_GATE), lambda i: (0, 0)),
                  pl.BlockSpec((1, D_GATE), lambda i: (0, 0)),
                  pl.BlockSpec((N_SG_GROUPS, CHUNK, CHUNK), lambda i: (0, 0, 0)),
                  pl.BlockSpec((CHUNK, N_SG_GROUPS), lambda i: (0, 0))],
        out_specs=pl.BlockSpec((CHUNK, D_GATE), lambda i: (i, 0)),
        out_shape=jax.ShapeDtypeStruct((T, D_GATE), jnp.bfloat16),
        compiler_params=_cparams(("parallel",)),
        name="spatial_gate",
    )(z, ln_g.reshape(1, D_GATE), ln_b.reshape(1, D_GATE), w_sp.astype(jnp.bfloat16), b_sp.T)


def _oddeven_merge_sort_pairs(n):
    pairs = []
    p = 1
    while p < n:
        k = p
        while k >= 1:
            for j in range(k % p, n - k, 2 * k):
                for i in range(min(k, n - j - k)):
                    if (i + j) // (2 * p) == (i + j + k) // (2 * p):
                        pairs.append((i + j, i + j + k))
            k //= 2
        p *= 2
    return pairs


_SORT16 = _oddeven_merge_sort_pairs(N_KEYS // SUBLANES)


def _top16_of_keys(s):
    nv = N_KEYS // SUBLANES
    L = s.shape[1]
    sub = lax.broadcasted_iota(jnp.int32, (SUBLANES, L), 0)
    v = [s[j * SUBLANES:(j + 1) * SUBLANES] for j in range(nv)]
    ids = [sub + j * SUBLANES for j in range(nv)]
    for i, j in _SORT16:
        swap = jnp.logical_or(v[j] > v[i], jnp.logical_and(v[j] == v[i], ids[j] < ids[i]))
        v[i], v[j] = jnp.where(swap, v[j], v[i]), jnp.where(swap, v[i], v[j])
        ids[i], ids[j] = jnp.where(swap, ids[j], ids[i]), jnp.where(swap, ids[i], ids[j])
    vals, picks = [], []
    for it in range(PEER_TOPK):
        m = jnp.max(v[0], axis=0, keepdims=True)
        am = jnp.min(jnp.where(v[0] == m, ids[0], N_KEYS), axis=0, keepdims=True)
        hit = ids[0] == am
        vals.append(m)
        picks.append(am)
        last = PEER_TOPK - 1 - it
        for j in range(last):
            v[j] = jnp.where(hit, v[j + 1], v[j])
            ids[j] = jnp.where(hit, ids[j + 1], ids[j])
        v[last] = jnp.where(hit, -jnp.inf, v[last])
    return jnp.concatenate(vals, axis=0), jnp.concatenate(picks, axis=0)


_CAND_ROWS = (
    [(a, 0) for a in range(16)]
    + [None] + [(0, b) for b in range(1, 16)]
    + [None] + [(a, 1) for a in range(1, 8)]
    + [None, None] + [(1, b) for b in range(2, 8)]
    + [(2, 2), (3, 2), (4, 2), (2, 3), (2, 4), (3, 3), None, None]
)
assert sorted(x for x in _CAND_ROWS if x) == sorted(
    (a, b) for a in range(16) for b in range(16) if (a + 1) * (b + 1) <= 16)


def _rows(x, picks):
    pieces, i = [], 0
    while i < len(picks):
        j = i
        while j + 1 < len(picks) and picks[j + 1] == picks[j] + 1:
            j += 1
        pieces.append(x[picks[i]:picks[j] + 1])
        i = j + 1
    return pieces[0] if len(pieces) == 1 else jnp.concatenate(pieces, axis=0)


def _cand_positions(L):
    row = lax.broadcasted_iota(jnp.int32, (len(_CAND_ROWS), L), 0)
    pos = jnp.full(row.shape, _PAD_POS, jnp.int32)
    for r, c in enumerate(_CAND_ROWS):
        if c is not None:
            pos = jnp.where(row == r, c[0] * PEER_TOPK + c[1], pos)
    return pos


_PAD_POS = 1 << 20


def _joint_top16(v1, i1, v2, i2, pos):
    a_of = [c[0] if c else 0 for c in _CAND_ROWS]
    b_of = [c[1] if c else 0 for c in _CAND_ROWS]
    cand = jnp.where(pos < _PAD_POS, _rows(v1, a_of) + _rows(v2, b_of), -jnp.inf)
    cidx = _rows(i1, a_of) * N_KEYS + _rows(i2, b_of)
    vals, picks = [], []
    for _ in range(PEER_TOPK):
        m = jnp.max(cand, axis=0, keepdims=True)
        pm = jnp.min(jnp.where(cand == m, pos, 1 << 21), axis=0, keepdims=True)
        hit = pos == pm
        vals.append(m)
        picks.append(jnp.max(jnp.where(hit, cidx, -1), axis=0, keepdims=True))
        cand = jnp.where(hit, -jnp.inf, cand)
    return jnp.concatenate(vals, axis=0), jnp.concatenate(picks, axis=0)


def _route_kernel(q_ref, keys_ref, idx_ref, gate_ref, *, precision):
    nt = (((1,), (1,)), ((), ()))
    pos = _cand_positions(LANES)
    for j in range(q_ref.shape[0] // LANES):
        q = q_ref[j * LANES:(j + 1) * LANES, :]
        s1 = lax.dot_general(keys_ref[0, 0], q[:, 0:D_HALF], nt,
                             preferred_element_type=jnp.float32, precision=precision)
        s2 = lax.dot_general(keys_ref[0, 1], q[:, D_HALF:2 * D_HALF], nt,
                             preferred_element_type=jnp.float32, precision=precision)
        v1, i1 = _top16_of_keys(s1)
        v2, i2 = _top16_of_keys(s2)
        top_s, top_i = _joint_top16(v1, i1, v2, i2, pos)
        e = jnp.exp(top_s - top_s[0:1])
        gate_ref[:, j * LANES:(j + 1) * LANES] = e / jnp.sum(e, axis=0, keepdims=True)
        idx_ref[:, j * LANES:(j + 1) * LANES] = top_i * ROW_SUB


def peer_route(q, sub_keys, *, tb=512, precision=None):
    T = q.shape[0]
    tb = min(tb, T)
    return pl.pallas_call(
        functools.partial(_route_kernel, precision=precision),
        grid=(T // tb, PEER_HEADS),
        in_specs=[pl.BlockSpec((tb, 2 * D_HALF), lambda i, h: (i, h)),
                  pl.BlockSpec((1, 2, N_KEYS, D_HALF), lambda i, h: (h, 0, 0, 0))],
        out_specs=[pl.BlockSpec((PEER_TOPK, tb), lambda i, h: (h, i)),
                   pl.BlockSpec((PEER_TOPK, tb), lambda i, h: (h, i))],
        out_shape=[jax.ShapeDtypeStruct((N_SLOTS, T), jnp.int32),
                   jax.ShapeDtypeStruct((N_SLOTS, T), jnp.float32)],
        compiler_params=_cparams(("parallel", "arbitrary")),
        name="peer_route",
    )(q, sub_keys)


def pack_table(tbl):
    D = tbl.shape[1]
    bits = lax.bitcast_convert_type(tbl.astype(jnp.bfloat16), jnp.uint16).astype(jnp.uint32)
    return bits[:, :D // 2] | (bits[:, D // 2:] << 16)


def _unpack(words):
    lo = pltpu.bitcast(words << 16, jnp.float32)
    hi = pltpu.bitcast(words & jnp.uint32(0xFFFF0000), jnp.float32)
    return lo, hi


IDX_GROUP = SUBLANES


def _gather_rows(ids_ref, row, tbl_ref, planes_ref, slots=range(N_SLOTS)):
    for k in slots:
        off = pl.multiple_of(ids_ref[row, k], ROW_SUB)
        planes_ref[pl.ds(k, ROW_SUB, stride=PLANE_STRIDE), :] = tbl_ref[pl.ds(off, ROW_SUB), :]


def _plane(planes_ref, c):
    return _unpack(planes_ref[c * PLANE_STRIDE:c * PLANE_STRIDE + N_SLOTS, :])


def _peer_down_kernel(idx_ref, x_ref, gate_ref, tbl_ref, o_ref, planes_a, planes_b, part_ref,
                      ids_a, ids_b, sem):
    tb = x_ref.shape[0]
    G = IDX_GROUP
    n_trips = tb // (2 * G)
    planes, ids = (planes_a, planes_b), (ids_a, ids_b)
    lane = lax.broadcasted_iota(jnp.int32, (N_SLOTS, tb), 1)

    def stage(group, buf):
        rows = pl.ds(pl.multiple_of(group * G, G), G)
        return pltpu.make_async_copy(idx_ref.at[rows], ids[buf], sem.at[buf])

    def gather_and_compute(t, planes_ref, slot, next_ids, next_row, next_planes):
        per = N_SLOTS // ROW_SUB
        acc = jnp.zeros((N_SLOTS, LANES), jnp.float32)
        for c in range(ROW_SUB):
            _gather_rows(next_ids, next_row, tbl_ref, next_planes, range(c * per, (c + 1) * per))
            lo, hi = _plane(planes_ref, c)
            acc = acc + lo * x_ref[t, c:c + 1, :] + hi * x_ref[t, ROW_SUB + c:ROW_SUB + c + 1, :]
        part_ref[slot] = acc

    def place_pair(t0):
        col_a = jnp.sum(part_ref[0], axis=1, keepdims=True)
        col_b = jnp.sum(part_ref[1], axis=1, keepdims=True)
        o_ref[...] = jnp.where(lane == t0 - 2, col_a, jnp.where(lane == t0 - 1, col_b, o_ref[...]))

    o_ref[...] = jnp.zeros(o_ref.shape, jnp.float32)
    part_ref[...] = jnp.zeros(part_ref.shape, jnp.float32)
    stage(0, 0).start()
    stage(1, 1).start()
    stage(0, 0).wait()
    _gather_rows(ids_a, 0, tbl_ref, planes_a)

    def trip(j, carry):
        not_last = j < n_trips - 1
        for u in range(2 * G):
            t = j * (2 * G) + u
            if u % 2 == 0:
                place_pair(t)
            nxt = u + 1
            if nxt < 2 * G:
                nbuf, nrow = divmod(nxt, G)
                if nrow == 0:
                    stage(2 * j + 1, 1).wait()
                gather_and_compute(t, planes[u % 2], u % 2, ids[nbuf], nrow, planes[nxt % 2])
                if nrow == G - 1:

                    @pl.when(not_last)
                    def _():
                        stage(2 * j + 2 + nbuf, nbuf).start()
            else:
                @pl.when(not_last)
                def _():
                    stage(2 * j + 2, 0).wait()

                gather_and_compute(t, planes[u % 2], u % 2, ids_a, 0, planes_a)
        return carry

    lax.fori_loop(0, n_trips, trip, 0)
    place_pair(tb)
    o_ref[...] = _gelu(o_ref[...]) * gate_ref[...]


def peer_down(idx, xn3, gate_t, tbl, *, tb=128):
    T = xn3.shape[0]
    tb = min(tb, T)
    assert tb % (2 * IDX_GROUP) == 0 and T % tb == 0
    planes = pltpu.VMEM((ROW_SUB * PLANE_STRIDE, LANES), jnp.uint32)
    ids = pltpu.SMEM((IDX_GROUP, N_SLOTS), jnp.int32)
    return pl.pallas_call(
        _peer_down_kernel,
        grid=(T // tb,),
        in_specs=[pl.BlockSpec((tb, N_SLOTS), lambda i: (i, 0)),
                  pl.BlockSpec((tb, SUBLANES, LANES), lambda i: (i, 0, 0)),
                  pl.BlockSpec((N_SLOTS, tb), lambda i: (0, i)),
                  pl.BlockSpec(tbl.shape, lambda i: (0, 0), pipeline_mode=pl.Buffered(1))],
        out_specs=pl.BlockSpec((N_SLOTS, tb), lambda i: (0, i)),
        out_shape=jax.ShapeDtypeStruct((N_SLOTS, T), jnp.float32),
        scratch_shapes=[planes, planes, pltpu.VMEM((2, N_SLOTS, LANES), jnp.float32), ids, ids,
                        pltpu.SemaphoreType.DMA((2,))],
        compiler_params=_cparams(("arbitrary",)),
        name="peer_down",
    )(idx, xn3, gate_t, tbl)


SC_CORES = 2
SC_SUBCORES = 16
SC_WORKERS = SC_CORES * SC_SUBCORES
SC_LANES = 16
SC_GATHER_ROWS = N_SLOTS // 2
SC_ROW_BLOCK = 8
SC_TOKEN_BLOCK = 32
SC_COL_GROUP = 8


def _sc_params():
    cp = pltpu.CompilerParams()
    if "needs_layout_passes" in pltpu.CompilerParams.__dataclass_fields__:
        cp = dataclasses.replace(cp, needs_layout_passes=False)
    return cp


def peer_up_sc(ids, w, tbl):
    T = ids.shape[0]
    L, CH, RB, TBK = SC_LANES, SC_GATHER_ROWS, SC_ROW_BLOCK, SC_TOKEN_BLOCK
    assert T % (SC_WORKERS * TBK) == 0
    tpw = T // SC_WORKERS
    mesh = plsc.VectorSubcoreMesh(core_axis_name="c", subcore_axis_name="s")
    rows_buf = pltpu.VMEM((CH, ROW_WORDS), jnp.uint32)
    y_buf = pltpu.VMEM((D_MODEL,), jnp.float32)

    @functools.partial(
        pl.kernel, mesh=mesh, compiler_params=_sc_params(),
        out_type=jax.ShapeDtypeStruct((T, D_MODEL), jnp.float32),
        scratch_types=[pltpu.VMEM((TBK, N_SLOTS), jnp.int32), pltpu.VMEM((TBK, N_SLOTS), jnp.float32),
                       rows_buf, rows_buf, y_buf, y_buf] + [pltpu.SemaphoreType.DMA] * 4,
    )
    def up_kernel(ids_hbm, w_hbm, tbl_hbm, out_hbm, ids_v, w_v, rows0, rows1, y0, y1, g0, g1, o0, o1):
        base = (lax.axis_index("s") * SC_CORES + lax.axis_index("c")) * tpw
        rows, gsem, ys, osem = (rows0, rows1), (g0, g1), (y0, y1), (o0, o1)

        def gather(tl, half):
            return pltpu.make_async_copy(tbl_hbm.at[ids_v.at[tl, pl.ds(half * CH, CH)]], rows[half], gsem[half])

        def out_copy(tok, slot):
            return pltpu.make_async_copy(ys[slot], out_hbm.at[tok], osem[slot])

        def accumulate(tl, half, y_v):
            rows_v = rows[half]

            def block(b, carry):
                r0 = b * RB
                tlv = jnp.full((L,), tl, jnp.int32)
                wks = [plsc.load_gather(w_v, [tlv, jnp.full((L,), half * CH + r0 + r, jnp.int32)])
                       for r in range(RB)]
                for q in range(ROW_WORDS // (SC_COL_GROUP * L)):
                    acc = [None] * (2 * SC_COL_GROUP)
                    for r in range(RB):
                        for i in range(SC_COL_GROUP):
                            v = rows_v[r0 + r, pl.ds((q * SC_COL_GROUP + i) * L, L)]
                            lo = plsc.bitcast(v << 16, jnp.float32) * wks[r]
                            hi = plsc.bitcast(v & jnp.uint32(0xFFFF0000), jnp.float32) * wks[r]
                            acc[2 * i] = lo if r == 0 else acc[2 * i] + lo
                            acc[2 * i + 1] = hi if r == 0 else acc[2 * i + 1] + hi
                    for i in range(SC_COL_GROUP):
                        col = (q * SC_COL_GROUP + i) * L
                        plsc.addupdate(y_v.at[pl.ds(col, L)], acc[2 * i])
                        plsc.addupdate(y_v.at[pl.ds(ROW_WORDS + col, L)], acc[2 * i + 1])
                return carry

            lax.fori_loop(0, CH // RB, block, 0)

        def token_block(bi, carry):
            tok0 = base + bi * TBK
            pltpu.sync_copy(ids_hbm.at[pl.ds(tok0, TBK)], ids_v)
            pltpu.sync_copy(w_hbm.at[pl.ds(tok0, TBK)], w_v)
            gather(0, 0).start()

            def token_pair(pi, carry2):
                for slot in range(2):
                    tl = 2 * pi + slot
                    y_v = ys[slot]

                    @pl.when(pi > 0)
                    def _():
                        out_copy(tok0 + tl - 2, slot).wait()

                    zero = jnp.zeros((L,), jnp.float32)
                    for j in range(D_MODEL // L):
                        y_v[pl.ds(j * L, L)] = zero
                    gather(tl, 1).start()
                    gather(tl, 0).wait()
                    accumulate(tl, 0, y_v)

                    @pl.when(tl + 1 < TBK)
                    def _():
                        gather(tl + 1, 0).start()

                    gather(tl, 1).wait()
                    accumulate(tl, 1, y_v)
                    out_copy(tok0 + tl, slot).start()
                return carry2

            lax.fori_loop(0, TBK // 2, token_pair, 0)
            out_copy(tok0 + TBK - 2, 0).wait()
            out_copy(tok0 + TBK - 1, 1).wait()
            return carry

        lax.fori_loop(0, tpw // TBK, token_block, 0)

    return up_kernel(ids, w, tbl)


def peer_layer(h, g, w_query, sub_keys, down_rows, up_rows):
    T = h.shape[0]
    q, xn = norm_matmul(h, g, w_query, out_dtype=jnp.float32, emit_xn=True)
    idx_t, gate_t = peer_route(q, sub_keys, precision=lax.Precision.HIGHEST)
    idx = idx_t.T
    w_t = peer_down(idx, xn.reshape(T, SUBLANES, LANES), gate_t, down_rows)
    return peer_up_sc(idx // ROW_SUB, w_t.T, up_rows)


def _rms_kernel(x_ref, r_ref, g_ref, o_ref):
    x = x_ref[...] + r_ref[...]
    o_ref[...] = x * lax.rsqrt(jnp.mean(x * x, axis=-1, keepdims=True) + NORM_EPS) * g_ref[...]


def rms_norm_sum(x, r, g, *, tm=512):
    T, D = x.shape
    tm = min(tm, T)
    row = pl.BlockSpec((tm, D), lambda i: (i, 0))
    return pl.pallas_call(
        _rms_kernel,
        grid=(T // tm,),
        in_specs=[row, row, pl.BlockSpec((1, D), lambda i: (0, 0))],
        out_specs=row,
        out_shape=jax.ShapeDtypeStruct((T, D), jnp.float32),
        compiler_params=_cparams(("parallel",)),
        name="final_norm",
    )(x, r, g.reshape(1, D))


N_PIECES = 8


def kernel(x, rel_table, mix_norm_g, attn_w_in, attn_w_out, attn_sink, sg_w_in, sg_ln_g, sg_ln_b, sg_w_spatial, sg_b_spatial, sg_w_out, ffn_norm_g, peer_w_query, peer_sub_keys, peer_down, peer_up, final_norm_g):
    B, S, D = x.shape
    bf16 = jnp.bfloat16
    n_pieces = N_PIECES if B % N_PIECES == 0 else 1
    Bp = B // n_pieces
    Tp = Bp * S

    bias = attn_bias(rel_table)
    attn_in, attn_out = attn_w_in[0].astype(bf16), attn_w_out[0].astype(bf16)
    sg_in, sg_out = sg_w_in[0].astype(bf16), sg_w_out[0].astype(bf16)
    w_query = peer_w_query.astype(bf16)
    down_rows = [pack_table(peer_down[i]).reshape(-1, LANES) for i in range(2)]
    up_rows = [pack_table(peer_up[i]) for i in range(2)]

    mid = []
    for p in range(n_pieces):
        h = x[p * Bp:(p + 1) * Bp].reshape(Tp, D)
        qkv, = norm_matmul(h, mix_norm_g[0], attn_in, out_dtype=bf16)
        att = window_attention(qkv.reshape(Bp, S, QKV_DIM), bias, attn_sink[0])
        h = matmul_residual(att.reshape(Tp, Q_DIM), attn_out, h)
        mid.append((h, peer_layer(h, ffn_norm_g[0], w_query[0], peer_sub_keys[0], down_rows[0], up_rows[0])))

    outs = []
    for h, y in mid:
        z, h = norm_matmul(h, mix_norm_g[1], sg_in, res=y, out_dtype=bf16, act=True, tn=1536)
        gated = spatial_gate(z, sg_ln_g[0], sg_ln_b[0], sg_w_spatial[0], sg_b_spatial[0])
        h = matmul_residual(gated, sg_out, h)
        y = peer_layer(h, ffn_norm_g[1], w_query[1], peer_sub_keys[1], down_rows[1], up_rows[1])
        outs.append(rms_norm_sum(h, y, final_norm_g).reshape(Bp, S, D))
    return jnp.concatenate(outs, axis=0)
```

```python
import dataclasses
import functools
import math

import jax
import jax.numpy as jnp
from jax import lax
from jax.experimental import pallas as pl
from jax.experimental.pallas import tpu as pltpu
from jax.experimental.pallas import tpu_sc as plsc

D_MODEL = 1024
HEAD_DIM = 64
N_Q_HEADS = 16
N_KV_HEADS = 4
GQA_GROUP = 4
WINDOW = 128
BLOCK = 128
REL_BUCKETS = 32
REL_MAX_DIST = 128
Q_DIM = N_Q_HEADS * HEAD_DIM
KV_DIM = N_KV_HEADS * HEAD_DIM
QKV_DIM = Q_DIM + 2 * KV_DIM
CHUNK = 128
D_GATE = 3072
N_SG_GROUPS = 8
SG_GROUP_DIM = D_GATE // N_SG_GROUPS
N_KEYS = 128
PEER_HEADS = 8
PEER_TOPK = 16
D_HALF = 128
N_SLOTS = PEER_HEADS * PEER_TOPK
NORM_EPS = 1e-6
LN_EPS = 1e-5

LANES = 128
SUBLANES = 8
ROW_WORDS = D_MODEL // 2
ROW_SUB = ROW_WORDS // LANES
PLANE_STRIDE = 136
VMEM_LIMIT = 56 * 1024 * 1024

_GELU_C = math.sqrt(2.0 / math.pi)


def _gelu(x):
    return 0.5 * x * (1.0 + jnp.tanh(_GELU_C * (x + 0.044715 * (x * x * x))))


def _cparams(sem):
    return pltpu.CompilerParams(dimension_semantics=sem, vmem_limit_bytes=VMEM_LIMIT)


def _norm_matmul_kernel(*refs, act, emit_xn, add_res, precision):
    refs = list(refs)
    x_ref = refs.pop(0)
    r_ref = refs.pop(0) if add_res else None
    g_ref, w_ref, o_ref = refs.pop(0), refs.pop(0), refs.pop(0)
    xn_ref = refs.pop(0) if emit_xn else None
    sum_ref = refs.pop(0) if add_res else None
    xs_ref = refs.pop(0)

    @pl.when(pl.program_id(1) == 0)
    def _():
        x = x_ref[...]
        if add_res:
            x = x + r_ref[...]
            sum_ref[...] = x
        y = x * lax.rsqrt(jnp.mean(x * x, axis=-1, keepdims=True) + NORM_EPS) * g_ref[...]
        xs_ref[...] = y.astype(xs_ref.dtype)
        if emit_xn:
            xn_ref[...] = y

    acc = jnp.dot(xs_ref[...], w_ref[...], preferred_element_type=jnp.float32,
                  precision=precision)
    if act:
        acc = _gelu(acc)
    o_ref[...] = acc.astype(o_ref.dtype)


def norm_matmul(x, g, w, *, out_dtype, res=None, act=False, emit_xn=False, tm=512, tn=None,
                precision=None):
    T, D = x.shape
    N = w.shape[1]
    tn = tn or N
    tm = min(tm, T)
    row = pl.BlockSpec((tm, D), lambda i, j: (i, 0))
    out_shape = [jax.ShapeDtypeStruct((T, N), out_dtype)]
    out_specs = [pl.BlockSpec((tm, tn), lambda i, j: (i, j))]
    for flag in (emit_xn, res is not None):
        if flag:
            out_shape.append(jax.ShapeDtypeStruct((T, D), jnp.float32))
            out_specs.append(row)
    args = [x] + ([res] if res is not None else []) + [g.reshape(1, D), w]
    in_specs = [row] * (len(args) - 2) + [pl.BlockSpec((1, D), lambda i, j: (0, 0)),
                                          pl.BlockSpec((D, tn), lambda i, j: (0, j))]
    return pl.pallas_call(
        functools.partial(_norm_matmul_kernel, act=act, emit_xn=emit_xn, add_res=res is not None,
                          precision=precision),
        grid=(T // tm, N // tn),
        in_specs=in_specs,
        out_specs=out_specs,
        out_shape=out_shape,
        scratch_shapes=[pltpu.VMEM((tm, D), w.dtype)],
        compiler_params=_cparams(("parallel", "arbitrary")),
        name="norm_matmul",
    )(*args)


def _matmul_res_kernel(a_ref, w_ref, h_ref, o_ref):
    o_ref[...] = h_ref[...] + jnp.dot(a_ref[...], w_ref[...],
                                      preferred_element_type=jnp.float32)


def matmul_residual(a, w, h, *, tm=512):
    T, K = a.shape
    N = w.shape[1]
    tm = min(tm, T)
    return pl.pallas_call(
        _matmul_res_kernel,
        grid=(T // tm,),
        in_specs=[pl.BlockSpec((tm, K), lambda i: (i, 0)),
                  pl.BlockSpec((K, N), lambda i: (0, 0)),
                  pl.BlockSpec((tm, N), lambda i: (i, 0))],
        out_specs=pl.BlockSpec((tm, N), lambda i: (i, 0)),
        out_shape=jax.ShapeDtypeStruct((T, N), jnp.float32),
        compiler_params=_cparams(("parallel",)),
        name="matmul_residual",
    )(a, w, h)


def _t5_bucket(rel):
    nb = REL_BUCKETS // 2
    max_exact = nb // 2
    ret = jnp.where(rel > 0, nb, 0)
    n = jnp.abs(rel)
    nf = jnp.maximum(n, 1).astype(jnp.float32)
    large = max_exact + (jnp.log(nf / max_exact) / math.log(REL_MAX_DIST / max_exact)
                         * (nb - max_exact)).astype(jnp.int32)
    large = jnp.minimum(large, nb - 1)
    return (ret + jnp.where(n < max_exact, n, large)).astype(jnp.int32)


def _bias_kernel(bucket_ref, window_ref, table_ref, o_ref):
    bucket = bucket_ref[...]
    in_window = window_ref[...] > 0
    for hq in range(N_Q_HEADS):
        acc = jnp.zeros(bucket.shape, jnp.float32)
        for b in range(REL_BUCKETS):
            acc = jnp.where(bucket == b, table_ref[b, hq], acc)
        o_ref[hq] = jnp.where(in_window, acc, -jnp.inf)


def attn_bias(rel_table):
    qi = jnp.arange(BLOCK)[:, None]
    kj = jnp.arange(3 * BLOCK)[None, :]
    rel = kj - BLOCK - qi
    bucket = _t5_bucket(rel)
    window = (jnp.abs(rel) <= WINDOW).astype(jnp.int32)
    return pl.pallas_call(
        _bias_kernel,
        in_specs=[pl.BlockSpec(memory_space=pltpu.VMEM),
                  pl.BlockSpec(memory_space=pltpu.VMEM),
                  pl.BlockSpec(memory_space=pltpu.SMEM)],
        out_specs=pl.BlockSpec(memory_space=pltpu.VMEM),
        out_shape=jax.ShapeDtypeStruct((N_Q_HEADS, BLOCK, 3 * BLOCK), jnp.float32),
        name="attn_bias",
    )(bucket, window, rel_table)


def _attn_kernel(cur_ref, prev_ref, next_ref, bias_ref, sink_ref, o_ref):
    i = pl.program_id(1)
    nb = pl.num_programs(1)
    q = cur_ref[0, :, 0:Q_DIM]
    kband = jnp.concatenate([prev_ref[0, :, 0:KV_DIM], cur_ref[0, :, Q_DIM:Q_DIM + KV_DIM],
                             next_ref[0, :, 0:KV_DIM]], axis=0)
    vband = jnp.concatenate([prev_ref[0, :, KV_DIM:2 * KV_DIM], cur_ref[0, :, Q_DIM + KV_DIM:QKV_DIM],
                             next_ref[0, :, KV_DIM:2 * KV_DIM]], axis=0)
    col = lax.broadcasted_iota(jnp.int32, (1, 3 * BLOCK), 1)
    valid = jnp.logical_and(jnp.logical_or(col >= BLOCK, i > 0),
                            jnp.logical_or(col < 2 * BLOCK, i < nb - 1))
    heads = range(N_Q_HEADS)
    kv = [(kband[:, hk * HEAD_DIM:(hk + 1) * HEAD_DIM], vband[:, hk * HEAD_DIM:(hk + 1) * HEAD_DIM])
          for hk in range(N_KV_HEADS)]
    s = []
    for hq in heads:
        qh = q[:, hq * HEAD_DIM:(hq + 1) * HEAD_DIM]
        sc = lax.dot_general(qh, kv[hq // GQA_GROUP][0], (((1,), (1,)), ((), ())),
                             preferred_element_type=jnp.float32)
        s.append(jnp.where(valid, sc * (HEAD_DIM ** -0.5) + bias_ref[hq], -jnp.inf))
    mx = [jnp.maximum(jnp.max(s[hq], axis=-1, keepdims=True), sink_ref[hq]) for hq in heads]
    p = [jnp.exp(s[hq] - mx[hq]) for hq in heads]
    denom = [jnp.sum(p[hq], axis=-1, keepdims=True) + jnp.exp(sink_ref[hq] - mx[hq]) for hq in heads]
    outs = [jnp.dot((p[hq] / denom[hq]).astype(jnp.bfloat16), kv[hq // GQA_GROUP][1],
                    preferred_element_type=jnp.float32) for hq in heads]
    o_ref[0] = jnp.concatenate(outs, axis=-1).astype(o_ref.dtype)


def window_attention(qkv, bias, sink):
    B, S, _ = qkv.shape
    nb = S // BLOCK
    kv_col = Q_DIM // (2 * KV_DIM)
    return pl.pallas_call(
        _attn_kernel,
        grid=(B, nb),
        in_specs=[pl.BlockSpec((1, BLOCK, QKV_DIM), lambda b, i: (b, i, 0)),
                  pl.BlockSpec((1, BLOCK, 2 * KV_DIM),
                               lambda b, i: (b, jnp.maximum(i - 1, 0), kv_col)),
                  pl.BlockSpec((1, BLOCK, 2 * KV_DIM),
                               lambda b, i: (b, jnp.minimum(i + 1, nb - 1), kv_col)),
                  pl.BlockSpec((N_Q_HEADS, BLOCK, 3 * BLOCK), lambda b, i: (0, 0, 0)),
                  pl.BlockSpec(memory_space=pltpu.SMEM)],
        out_specs=pl.BlockSpec((1, BLOCK, Q_DIM), lambda b, i: (b, i, 0)),
        out_shape=jax.ShapeDtypeStruct((B, S, Q_DIM), jnp.bfloat16),
        compiler_params=_cparams(("parallel", "arbitrary")),
        name="window_attention",
    )(qkv, qkv, qkv, bias, sink)


def _spatial_gate_kernel(z_ref, g_ref, b_ref, wsp_ref, bsp_ref, o_ref):
    v = z_ref[:, D_GATE:2 * D_GATE].astype(jnp.float32)
    mu = jnp.mean(v, axis=-1, keepdims=True)
    vc = v - mu
    var = jnp.mean(vc * vc, axis=-1, keepdims=True)
    vn = (vc * lax.rsqrt(var + LN_EPS) * g_ref[...] + b_ref[...]).astype(jnp.bfloat16)
    for grp in range(N_SG_GROUPS):
        lo, hi = grp * SG_GROUP_DIM, (grp + 1) * SG_GROUP_DIM
        mixed = jnp.dot(wsp_ref[grp], vn[:, lo:hi], preferred_element_type=jnp.float32)
        mixed = mixed + bsp_ref[:, grp:grp + 1]
        u = z_ref[:, lo:hi].astype(jnp.float32)
        o_ref[:, lo:hi] = (u * mixed).astype(o_ref.dtype)


def spatial_gate(z, ln_g, ln_b, w_sp, b_sp):
    T = z.shape[0]
    return pl.pallas_call(
        _spatial_gate_kernel,
        grid=(T // CHUNK,),
        in_specs=[pl.BlockSpec((CHUNK, 2 * D_GATE), lambda i: (i, 0)),
                  pl.BlockSpec((1, D_GATE), lambda i: (0, 0)),
                  pl.BlockSpec((1, D_GATE), lambda i: (0, 0)),
                  pl.BlockSpec((N_SG_GROUPS, CHUNK, CHUNK), lambda i: (0, 0, 0)),
                  pl.BlockSpec((CHUNK, N_SG_GROUPS), lambda i: (0, 0))],
        out_specs=pl.BlockSpec((CHUNK, D_GATE), lambda i: (i, 0)),
        out_shape=jax.ShapeDtypeStruct((T, D_GATE), jnp.bfloat16),
        compiler_params=_cparams(("parallel",)),
        name="spatial_gate",
    )(z, ln_g.reshape(1, D_GATE), ln_b.reshape(1, D_GATE), w_sp.astype(jnp.bfloat16), b_sp.T)


def _oddeven_merge_sort_pairs(n):
    pairs = []
    p = 1
    while p < n:
        k = p
        while k >= 1:
            for j in range(k % p, n - k, 2 * k):
                for i in range(min(k, n - j - k)):
                    if (i + j) // (2 * p) == (i + j + k) // (2 * p):
                        pairs.append((i + j, i + j + k))
            k //= 2
        p *= 2
    return pairs


_SORT16 = _oddeven_merge_sort_pairs(N_KEYS // SUBLANES)


def _top16_of_keys(s):
    nv = N_KEYS // SUBLANES
    L = s.shape[1]
    sub = lax.broadcasted_iota(jnp.int32, (SUBLANES, L), 0)
    v = [s[j * SUBLANES:(j + 1) * SUBLANES] for j in range(nv)]
    ids = [sub + j * SUBLANES for j in range(nv)]
    for i, j in _SORT16:
        swap = jnp.logical_or(v[j] > v[i], jnp.logical_and(v[j] == v[i], ids[j] < ids[i]))
        v[i], v[j] = jnp.where(swap, v[j], v[i]), jnp.where(swap, v[i], v[j])
        ids[i], ids[j] = jnp.where(swap, ids[j], ids[i]), jnp.where(swap, ids[i], ids[j])
    vals, picks = [], []
    for it in range(PEER_TOPK):
        m = jnp.max(v[0], axis=0, keepdims=True)
        am = jnp.min(jnp.where(v[0] == m, ids[0], N_KEYS), axis=0, keepdims=True)
        hit = ids[0] == am
        vals.append(m)
        picks.append(am)
        last = PEER_TOPK - 1 - it
        for j in range(last):
            v[j] = jnp.where(hit, v[j + 1], v[j])
            ids[j] = jnp.where(hit, ids[j + 1], ids[j])
        v[last] = jnp.where(hit, -jnp.inf, v[last])
    return jnp.concatenate(vals, axis=0), jnp.concatenate(picks, axis=0)


_CAND_ROWS = (
    [(a, 0) for a in range(16)]
    + [None] + [(0, b) for b in range(1, 16)]
    + [None] + [(a, 1) for a in range(1, 8)]
    + [None, None] + [(1, b) for b in range(2, 8)]
    + [(2, 2), (3, 2), (4, 2), (2, 3), (2, 4), (3, 3), None, None]
)
assert sorted(x for x in _CAND_ROWS if x) == sorted(
    (a, b) for a in range(16) for b in range(16) if (a + 1) * (b + 1) <= 16)


def _rows(x, picks):
    pieces, i = [], 0
    while i < len(picks):
        j = i
        while j + 1 < len(picks) and picks[j + 1] == picks[j] + 1:
            j += 1
        pieces.append(x[picks[i]:picks[j] + 1])
        i = j + 1
    return pieces[0] if len(pieces) == 1 else jnp.concatenate(pieces, axis=0)


def _cand_positions(L):
    row = lax.broadcasted_iota(jnp.int32, (len(_CAND_ROWS), L), 0)
    pos = jnp.full(row.shape, _PAD_POS, jnp.int32)
    for r, c in enumerate(_CAND_ROWS):
        if c is not None:
            pos = jnp.where(row == r, c[0] * PEER_TOPK + c[1], pos)
    return pos


_PAD_POS = 1 << 20


def _joint_top16(v1, i1, v2, i2, pos):
    a_of = [c[0] if c else 0 for c in _CAND_ROWS]
    b_of = [c[1] if c else 0 for c in _CAND_ROWS]
    cand = jnp.where(pos < _PAD_POS, _rows(v1, a_of) + _rows(v2, b_of), -jnp.inf)
    cidx = _rows(i1, a_of) * N_KEYS + _rows(i2, b_of)
    vals, picks = [], []
    for _ in range(PEER_TOPK):
        m = jnp.max(cand, axis=0, keepdims=True)
        pm = jnp.min(jnp.where(cand == m, pos, 1 << 21), axis=0, keepdims=True)
        hit = pos == pm
        vals.append(m)
        picks.append(jnp.max(jnp.where(hit, cidx, -1), axis=0, keepdims=True))
        cand = jnp.where(hit, -jnp.inf, cand)
    return jnp.concatenate(vals, axis=0), jnp.concatenate(picks, axis=0)


def _route_kernel(q_ref, keys_ref, idx_ref, gate_ref, *, precision):
    nt = (((1,), (1,)), ((), ()))
    pos = _cand_positions(LANES)
    for j in range(q_ref.shape[0] // LANES):
        q = q_ref[j * LANES:(j + 1) * LANES, :]
        s1 = lax.dot_general(keys_ref[0, 0], q[:, 0:D_HALF], nt,
                             preferred_element_type=jnp.float32, precision=precision)
        s2 = lax.dot_general(keys_ref[0, 1], q[:, D_HALF:2 * D_HALF], nt,
                             preferred_element_type=jnp.float32, precision=precision)
        v1, i1 = _top16_of_keys(s1)
        v2, i2 = _top16_of_keys(s2)
        top_s, top_i = _joint_top16(v1, i1, v2, i2, pos)
        e = jnp.exp(top_s - top_s[0:1])
        gate_ref[:, j * LANES:(j + 1) * LANES] = e / jnp.sum(e, axis=0, keepdims=True)
        idx_ref[:, j * LANES:(j + 1) * LANES] = top_i * ROW_SUB


def peer_route(q, sub_keys, *, tb=512, precision=None):
    T = q.shape[0]
    tb = min(tb, T)
    return pl.pallas_call(
        functools.partial(_route_kernel, precision=precision),
        grid=(T // tb, PEER_HEADS),
        in_specs=[pl.BlockSpec((tb, 2 * D_HALF), lambda i, h: (i, h)),
                  pl.BlockSpec((1, 2, N_KEYS, D_HALF), lambda i, h: (h, 0, 0, 0))],
        out_specs=[pl.BlockSpec((PEER_TOPK, tb), lambda i, h: (h, i)),
                   pl.BlockSpec((PEER_TOPK, tb), lambda i, h: (h, i))],
        out_shape=[jax.ShapeDtypeStruct((N_SLOTS, T), jnp.int32),
                   jax.ShapeDtypeStruct((N_SLOTS, T), jnp.float32)],
        compiler_params=_cparams(("parallel", "arbitrary")),
        name="peer_route",
    )(q, sub_keys)


def _pack_kernel(x_ref, o_ref, *, split_rows):
    x = x_ref[...]
    half = x.shape[1] // 2

    def bf16_bits(v):
        return pltpu.bitcast(v.astype(jnp.bfloat16).astype(jnp.float32), jnp.uint32)

    words = (bf16_bits(x[:, :half]) >> 16) | (bf16_bits(x[:, half:]) & jnp.uint32(0xFFFF0000))
    if split_rows:
        n = x.shape[0]
        for c in range(ROW_SUB):
            o_ref[pl.ds(c, n, stride=ROW_SUB), :] = words[:, c * LANES:(c + 1) * LANES]
    else:
        o_ref[...] = words


def pack_table(tbl, *, split_rows, te=256):
    E, D = tbl.shape
    out_block, out_shape = ((te * ROW_SUB, LANES), (E * ROW_SUB, LANES)) if split_rows else ((te, D // 2), (E, D // 2))
    return pl.pallas_call(
        functools.partial(_pack_kernel, split_rows=split_rows),
        grid=(E // te,),
        in_specs=[pl.BlockSpec((te, D), lambda i: (i, 0))],
        out_specs=pl.BlockSpec(out_block, lambda i: (i, 0)),
        out_shape=jax.ShapeDtypeStruct(out_shape, jnp.uint32),
        compiler_params=_cparams(("parallel",)),
        name="pack_table",
    )(tbl)


def _unpack(words):
    lo = pltpu.bitcast(words << 16, jnp.float32)
    hi = pltpu.bitcast(words & jnp.uint32(0xFFFF0000), jnp.float32)
    return lo, hi


def _gather_rows(idx_ref, tbl_ref, t, planes_ref):
    for k in range(N_SLOTS):
        off = pl.multiple_of(idx_ref[t, k], ROW_SUB)
        planes_ref[pl.ds(k, ROW_SUB, stride=PLANE_STRIDE), :] = tbl_ref[pl.ds(off, ROW_SUB), :]


def _plane(planes_ref, c):
    return _unpack(planes_ref[c * PLANE_STRIDE:c * PLANE_STRIDE + N_SLOTS, :])


def _peer_down_kernel(idx_ref, x_ref, gate_ref, tbl_ref, o_ref, planes_a, planes_b, part_ref):
    tb = x_ref.shape[0]
    lane = lax.broadcasted_iota(jnp.int32, (N_SLOTS, tb), 1)

    def compute(t, planes_ref, slot):
        acc = jnp.zeros((N_SLOTS, LANES), jnp.float32)
        for c in range(ROW_SUB):
            lo, hi = _plane(planes_ref, c)
            acc = acc + lo * x_ref[t, c:c + 1, :] + hi * x_ref[t, ROW_SUB + c:ROW_SUB + c + 1, :]
        part_ref[slot] = acc

    def place_pair(t0):
        col_a = jnp.sum(part_ref[0], axis=1, keepdims=True)
        col_b = jnp.sum(part_ref[1], axis=1, keepdims=True)
        o_ref[...] = jnp.where(lane == t0 - 2, col_a, jnp.where(lane == t0 - 1, col_b, o_ref[...]))

    o_ref[...] = jnp.zeros(o_ref.shape, jnp.float32)
    part_ref[...] = jnp.zeros(part_ref.shape, jnp.float32)
    _gather_rows(idx_ref, tbl_ref, 0, planes_a)

    def pair(i, carry):
        t0 = 2 * i
        place_pair(t0)
        _gather_rows(idx_ref, tbl_ref, t0 + 1, planes_b)
        compute(t0, planes_a, 0)
        _gather_rows(idx_ref, tbl_ref, jnp.minimum(t0 + 2, tb - 1), planes_a)
        compute(t0 + 1, planes_b, 1)
        return carry

    lax.fori_loop(0, tb // 2, pair, 0)
    place_pair(tb)
    o_ref[...] = _gelu(o_ref[...]) * gate_ref[...]


def peer_down(idx, xn3, gate_t, tbl, *, tb=128):
    T = xn3.shape[0]
    tb = min(tb, T)
    assert tb % 2 == 0 and T % tb == 0
    planes = pltpu.VMEM((ROW_SUB * PLANE_STRIDE, LANES), jnp.uint32)
    return pl.pallas_call(
        _peer_down_kernel,
        grid=(T // tb,),
        in_specs=[pl.BlockSpec((tb, N_SLOTS), lambda i: (i, 0), memory_space=pltpu.SMEM),
                  pl.BlockSpec((tb, SUBLANES, LANES), lambda i: (i, 0, 0)),
                  pl.BlockSpec((N_SLOTS, tb), lambda i: (0, i)),
                  pl.BlockSpec(tbl.shape, lambda i: (0, 0), pipeline_mode=pl.Buffered(1))],
        out_specs=pl.BlockSpec((N_SLOTS, tb), lambda i: (0, i)),
        out_shape=jax.ShapeDtypeStruct((N_SLOTS, T), jnp.float32),
        scratch_shapes=[planes, planes, pltpu.VMEM((2, N_SLOTS, LANES), jnp.float32)],
        compiler_params=_cparams(("arbitrary",)),
        name="peer_down",
    )(idx, xn3, gate_t, tbl)


SC_CORES = 2
SC_SUBCORES = 16
SC_WORKERS = SC_CORES * SC_SUBCORES
SC_LANES = 16
SC_GATHER_ROWS = N_SLOTS // 2
SC_ROW_BLOCK = 8
SC_TOKEN_BLOCK = 32
SC_COL_GROUP = 8


def _sc_params():
    cp = pltpu.CompilerParams()
    if "needs_layout_passes" in pltpu.CompilerParams.__dataclass_fields__:
        cp = dataclasses.replace(cp, needs_layout_passes=False)
    return cp


def peer_up_sc(ids, w, tbl):
    T = ids.shape[0]
    L, CH, RB, TBK = SC_LANES, SC_GATHER_ROWS, SC_ROW_BLOCK, SC_TOKEN_BLOCK
    assert T % (SC_WORKERS * TBK) == 0
    tpw = T // SC_WORKERS
    mesh = plsc.VectorSubcoreMesh(core_axis_name="c", subcore_axis_name="s")
    rows_buf = pltpu.VMEM((CH, ROW_WORDS), jnp.uint32)
    y_buf = pltpu.VMEM((D_MODEL,), jnp.float32)

    @functools.partial(
        pl.kernel, mesh=mesh, compiler_params=_sc_params(),
        out_type=jax.ShapeDtypeStruct((T, D_MODEL), jnp.float32),
        scratch_types=[pltpu.VMEM((TBK, N_SLOTS), jnp.int32), pltpu.VMEM((TBK, N_SLOTS), jnp.float32),
                       rows_buf, rows_buf, y_buf, y_buf] + [pltpu.SemaphoreType.DMA] * 4,
    )
    def up_kernel(ids_hbm, w_hbm, tbl_hbm, out_hbm, ids_v, w_v, rows0, rows1, y0, y1, g0, g1, o0, o1):
        base = (lax.axis_index("s") * SC_CORES + lax.axis_index("c")) * tpw
        rows, gsem, ys, osem = (rows0, rows1), (g0, g1), (y0, y1), (o0, o1)

        def gather(tl, half):
            return pltpu.make_async_copy(tbl_hbm.at[ids_v.at[tl, pl.ds(half * CH, CH)]], rows[half], gsem[half])

        def out_copy(tok, slot):
            return pltpu.make_async_copy(ys[slot], out_hbm.at[tok], osem[slot])

        def accumulate(tl, half, y_v):
            rows_v = rows[half]

            def block(b, carry):
                r0 = b * RB
                tlv = jnp.full((L,), tl, jnp.int32)
                wks = [plsc.load_gather(w_v, [tlv, jnp.full((L,), half * CH + r0 + r, jnp.int32)])
                       for r in range(RB)]
                for q in range(ROW_WORDS // (SC_COL_GROUP * L)):
                    acc = [None] * (2 * SC_COL_GROUP)
                    for r in range(RB):
                        for i in range(SC_COL_GROUP):
                            v = rows_v[r0 + r, pl.ds((q * SC_COL_GROUP + i) * L, L)]
                            lo = plsc.bitcast(v << 16, jnp.float32) * wks[r]
                            hi = plsc.bitcast(v & jnp.uint32(0xFFFF0000), jnp.float32) * wks[r]
                            acc[2 * i] = lo if r == 0 else acc[2 * i] + lo
                            acc[2 * i + 1] = hi if r == 0 else acc[2 * i + 1] + hi
                    for i in range(SC_COL_GROUP):
                        col = (q * SC_COL_GROUP + i) * L
                        plsc.addupdate(y_v.at[pl.ds(col, L)], acc[2 * i])
                        plsc.addupdate(y_v.at[pl.ds(ROW_WORDS + col, L)], acc[2 * i + 1])
                return carry

            lax.fori_loop(0, CH // RB, block, 0)

        def token_block(bi, carry):
            tok0 = base + bi * TBK
            pltpu.sync_copy(ids_hbm.at[pl.ds(tok0, TBK)], ids_v)
            pltpu.sync_copy(w_hbm.at[pl.ds(tok0, TBK)], w_v)
            gather(0, 0).start()

            def token_pair(pi, carry2):
                for slot in range(2):
                    tl = 2 * pi + slot
                    y_v = ys[slot]

                    @pl.when(pi > 0)
                    def _():
                        out_copy(tok0 + tl - 2, slot).wait()

                    zero = jnp.zeros((L,), jnp.float32)
                    for j in range(D_MODEL // L):
                        y_v[pl.ds(j * L, L)] = zero
                    gather(tl, 1).start()
                    gather(tl, 0).wait()
                    accumulate(tl, 0, y_v)

                    @pl.when(tl + 1 < TBK)
                    def _():
                        gather(tl + 1, 0).start()

                    gather(tl, 1).wait()
                    accumulate(tl, 1, y_v)
                    out_copy(tok0 + tl, slot).start()
                return carry2

            lax.fori_loop(0, TBK // 2, token_pair, 0)
            out_copy(tok0 + TBK - 2, 0).wait()
            out_copy(tok0 + TBK - 1, 1).wait()
            return carry

        lax.fori_loop(0, tpw // TBK, token_block, 0)

    return up_kernel(ids, w, tbl)


def peer_layer(h, g, w_query, sub_keys, down_rows, up_rows):
    T = h.shape[0]
    q, xn = norm_matmul(h, g, w_query, out_dtype=jnp.float32, emit_xn=True)
    idx_t, gate_t = peer_route(q, sub_keys, precision=lax.Precision.HIGHEST)
    idx = idx_t.T
    w_t = peer_down(idx, xn.reshape(T, SUBLANES, LANES), gate_t, down_rows)
    return peer_up_sc(idx // ROW_SUB, w_t.T, up_rows)


def _rms_kernel(x_ref, r_ref, g_ref, o_ref):
    x = x_ref[...] + r_ref[...]
    o_ref[...] = x * lax.rsqrt(jnp.mean(x * x, axis=-1, keepdims=True) + NORM_EPS) * g_ref[...]


def rms_norm_sum(x, r, g, *, tm=512):
    T, D = x.shape
    tm = min(tm, T)
    row = pl.BlockSpec((tm, D), lambda i: (i, 0))
    return pl.pallas_call(
        _rms_kernel,
        grid=(T // tm,),
        in_specs=[row, row, pl.BlockSpec((1, D), lambda i: (0, 0))],
        out_specs=row,
        out_shape=jax.ShapeDtypeStruct((T, D), jnp.float32),
        compiler_params=_cparams(("parallel",)),
        name="final_norm",
    )(x, r, g.reshape(1, D))


N_PIECES = 8


def kernel(x, rel_table, mix_norm_g, attn_w_in, attn_w_out, attn_sink, sg_w_in, sg_ln_g, sg_ln_b, sg_w_spatial, sg_b_spatial, sg_w_out, ffn_norm_g, peer_w_query, peer_sub_keys, peer_down, peer_up, final_norm_g):
    B, S, D = x.shape
    bf16 = jnp.bfloat16
    n_pieces = N_PIECES if B % N_PIECES == 0 else 1
    Bp = B // n_pieces
    Tp = Bp * S

    bias = attn_bias(rel_table)
    attn_in, attn_out = attn_w_in[0].astype(bf16), attn_w_out[0].astype(bf16)
    sg_in, sg_out = sg_w_in[0].astype(bf16), sg_w_out[0].astype(bf16)
    w_query = peer_w_query.astype(bf16)
    down_rows = [pack_table(peer_down[i], split_rows=True) for i in range(2)]
    up_rows = [pack_table(peer_up[i], split_rows=False) for i in range(2)]

    mid = []
    for p in range(n_pieces):
        h = x[p * Bp:(p + 1) * Bp].reshape(Tp, D)
        qkv, = norm_matmul(h, mix_norm_g[0], attn_in, out_dtype=bf16)
        att = window_attention(qkv.reshape(Bp, S, QKV_DIM), bias, attn_sink[0])
        h = matmul_residual(att.reshape(Tp, Q_DIM), attn_out, h)
        mid.append((h, peer_layer(h, ffn_norm_g[0], w_query[0], peer_sub_keys[0], down_rows[0], up_rows[0])))

    outs = []
    for h, y in mid:
        z, h = norm_matmul(h, mix_norm_g[1], sg_in, res=y, out_dtype=bf16, act=True, tn=1536)
        gated = spatial_gate(z, sg_ln_g[0], sg_ln_b[0], sg_w_spatial[0], sg_b_spatial[0])
        h = matmul_residual(gated, sg_out, h)
        y = peer_layer(h, ffn_norm_g[1], w_query[1], peer_sub_keys[1], down_rows[1], up_rows[1])
        outs.append(rms_norm_sum(h, y, final_norm_g).reshape(Bp, S, D))
    return jnp.concatenate(outs, axis=0)
```

```python
import dataclasses
import functools
import math

import jax
import jax.numpy as jnp
from jax import lax
from jax.experimental import pallas as pl
from jax.experimental.pallas import tpu as pltpu
from jax.experimental.pallas import tpu_sc as plsc

D_MODEL = 1024
HEAD_DIM = 64
N_Q_HEADS = 16
N_KV_HEADS = 4
GQA_GROUP = 4
WINDOW = 128
BLOCK = 128
REL_BUCKETS = 32
REL_MAX_DIST = 128
Q_DIM = N_Q_HEADS * HEAD_DIM
KV_DIM = N_KV_HEADS * HEAD_DIM
QKV_DIM = Q_DIM + 2 * KV_DIM
CHUNK = 128
D_GATE = 3072
N_SG_GROUPS = 8
SG_GROUP_DIM = D_GATE // N_SG_GROUPS
N_KEYS = 128
PEER_HEADS = 8
PEER_TOPK = 16
D_HALF = 128
N_SLOTS = PEER_HEADS * PEER_TOPK
NORM_EPS = 1e-6
LN_EPS = 1e-5

LANES = 128
SUBLANES = 8
ROW_WORDS = D_MODEL // 2
ROW_SUB = ROW_WORDS // LANES
PLANE_STRIDE = 136
VMEM_LIMIT = 56 * 1024 * 1024

_GELU_C = math.sqrt(2.0 / math.pi)


def _gelu(x):
    return 0.5 * x * (1.0 + jnp.tanh(_GELU_C * (x + 0.044715 * (x * x * x))))


def _cparams(sem):
    return pltpu.CompilerParams(dimension_semantics=sem, vmem_limit_bytes=VMEM_LIMIT)


def _norm_matmul_kernel(*refs, act, emit_xn, add_res, precision):
    refs = list(refs)
    x_ref = refs.pop(0)
    r_ref = refs.pop(0) if add_res else None
    g_ref, w_ref, o_ref = refs.pop(0), refs.pop(0), refs.pop(0)
    xn_ref = refs.pop(0) if emit_xn else None
    sum_ref = refs.pop(0) if add_res else None
    xs_ref = refs.pop(0)

    @pl.when(pl.program_id(1) == 0)
    def _():
        x = x_ref[...]
        if add_res:
            x = x + r_ref[...]
            sum_ref[...] = x
        y = x * lax.rsqrt(jnp.mean(x * x, axis=-1, keepdims=True) + NORM_EPS) * g_ref[...]
        xs_ref[...] = y.astype(xs_ref.dtype)
        if emit_xn:
            xn_ref[...] = y

    acc = jnp.dot(xs_ref[...], w_ref[...], preferred_element_type=jnp.float32,
                  precision=precision)
    if act:
        acc = _gelu(acc)
    o_ref[...] = acc.astype(o_ref.dtype)


def norm_matmul(x, g, w, *, out_dtype, res=None, act=False, emit_xn=False, tm=512, tn=None,
                precision=None):
    T, D = x.shape
    N = w.shape[1]
    tn = tn or N
    tm = min(tm, T)
    row = pl.BlockSpec((tm, D), lambda i, j: (i, 0))
    out_shape = [jax.ShapeDtypeStruct((T, N), out_dtype)]
    out_specs = [pl.BlockSpec((tm, tn), lambda i, j: (i, j))]
    for flag in (emit_xn, res is not None):
        if flag:
            out_shape.append(jax.ShapeDtypeStruct((T, D), jnp.float32))
            out_specs.append(row)
    args = [x] + ([res] if res is not None else []) + [g.reshape(1, D), w]
    in_specs = [row] * (len(args) - 2) + [pl.BlockSpec((1, D), lambda i, j: (0, 0)),
                                          pl.BlockSpec((D, tn), lambda i, j: (0, j))]
    return pl.pallas_call(
        functools.partial(_norm_matmul_kernel, act=act, emit_xn=emit_xn, add_res=res is not None,
                          precision=precision),
        grid=(T // tm, N // tn),
        in_specs=in_specs,
        out_specs=out_specs,
        out_shape=out_shape,
        scratch_shapes=[pltpu.VMEM((tm, D), w.dtype)],
        compiler_params=_cparams(("parallel", "arbitrary")),
        name="norm_matmul",
    )(*args)


def _matmul_res_kernel(a_ref, w_ref, h_ref, o_ref):
    o_ref[...] = h_ref[...] + jnp.dot(a_ref[...], w_ref[...],
                                      preferred_element_type=jnp.float32)


def matmul_residual(a, w, h, *, tm=512):
    T, K = a.shape
    N = w.shape[1]
    tm = min(tm, T)
    return pl.pallas_call(
        _matmul_res_kernel,
        grid=(T // tm,),
        in_specs=[pl.BlockSpec((tm, K), lambda i: (i, 0)),
                  pl.BlockSpec((K, N), lambda i: (0, 0)),
                  pl.BlockSpec((tm, N), lambda i: (i, 0))],
        out_specs=pl.BlockSpec((tm, N), lambda i: (i, 0)),
        out_shape=jax.ShapeDtypeStruct((T, N), jnp.float32),
        compiler_params=_cparams(("parallel",)),
        name="matmul_residual",
    )(a, w, h)


def _t5_bucket(rel):
    nb = REL_BUCKETS // 2
    max_exact = nb // 2
    ret = jnp.where(rel > 0, nb, 0)
    n = jnp.abs(rel)
    nf = jnp.maximum(n, 1).astype(jnp.float32)
    large = max_exact + (jnp.log(nf / max_exact) / math.log(REL_MAX_DIST / max_exact)
                         * (nb - max_exact)).astype(jnp.int32)
    large = jnp.minimum(large, nb - 1)
    return (ret + jnp.where(n < max_exact, n, large)).astype(jnp.int32)


def _bias_kernel(bucket_ref, window_ref, table_ref, o_ref):
    bucket = bucket_ref[...]
    in_window = window_ref[...] > 0
    for hq in range(N_Q_HEADS):
        acc = jnp.zeros(bucket.shape, jnp.float32)
        for b in range(REL_BUCKETS):
            acc = jnp.where(bucket == b, table_ref[b, hq], acc)
        o_ref[hq] = jnp.where(in_window, acc, -jnp.inf)


def attn_bias(rel_table):
    qi = jnp.arange(BLOCK)[:, None]
    kj = jnp.arange(3 * BLOCK)[None, :]
    rel = kj - BLOCK - qi
    bucket = _t5_bucket(rel)
    window = (jnp.abs(rel) <= WINDOW).astype(jnp.int32)
    return pl.pallas_call(
        _bias_kernel,
        in_specs=[pl.BlockSpec(memory_space=pltpu.VMEM),
                  pl.BlockSpec(memory_space=pltpu.VMEM),
                  pl.BlockSpec(memory_space=pltpu.SMEM)],
        out_specs=pl.BlockSpec(memory_space=pltpu.VMEM),
        out_shape=jax.ShapeDtypeStruct((N_Q_HEADS, BLOCK, 3 * BLOCK), jnp.float32),
        name="attn_bias",
    )(bucket, window, rel_table)


def _attn_kernel(cur_ref, prev_ref, next_ref, bias_ref, sink_ref, o_ref):
    i = pl.program_id(1)
    nb = pl.num_programs(1)
    q = cur_ref[0, :, 0:Q_DIM]
    kband = jnp.concatenate([prev_ref[0, :, 0:KV_DIM], cur_ref[0, :, Q_DIM:Q_DIM + KV_DIM],
                             next_ref[0, :, 0:KV_DIM]], axis=0)
    vband = jnp.concatenate([prev_ref[0, :, KV_DIM:2 * KV_DIM], cur_ref[0, :, Q_DIM + KV_DIM:QKV_DIM],
                             next_ref[0, :, KV_DIM:2 * KV_DIM]], axis=0)
    col = lax.broadcasted_iota(jnp.int32, (1, 3 * BLOCK), 1)
    valid = jnp.logical_and(jnp.logical_or(col >= BLOCK, i > 0),
                            jnp.logical_or(col < 2 * BLOCK, i < nb - 1))
    heads = range(N_Q_HEADS)
    kv = [(kband[:, hk * HEAD_DIM:(hk + 1) * HEAD_DIM], vband[:, hk * HEAD_DIM:(hk + 1) * HEAD_DIM])
          for hk in range(N_KV_HEADS)]
    s = []
    for hq in heads:
        qh = q[:, hq * HEAD_DIM:(hq + 1) * HEAD_DIM]
        sc = lax.dot_general(qh, kv[hq // GQA_GROUP][0], (((1,), (1,)), ((), ())),
                             preferred_element_type=jnp.float32)
        s.append(jnp.where(valid, sc * (HEAD_DIM ** -0.5) + bias_ref[hq], -jnp.inf))
    mx = [jnp.maximum(jnp.max(s[hq], axis=-1, keepdims=True), sink_ref[hq]) for hq in heads]
    p = [jnp.exp(s[hq] - mx[hq]) for hq in heads]
    denom = [jnp.sum(p[hq], axis=-1, keepdims=True) + jnp.exp(sink_ref[hq] - mx[hq]) for hq in heads]
    outs = [jnp.dot((p[hq] / denom[hq]).astype(jnp.bfloat16), kv[hq // GQA_GROUP][1],
                    preferred_element_type=jnp.float32) for hq in heads]
    o_ref[0] = jnp.concatenate(outs, axis=-1).astype(o_ref.dtype)


def window_attention(qkv, bias, sink):
    B, S, _ = qkv.shape
    nb = S // BLOCK
    kv_col = Q_DIM // (2 * KV_DIM)
    return pl.pallas_call(
        _attn_kernel,
        grid=(B, nb),
        in_specs=[pl.BlockSpec((1, BLOCK, QKV_DIM), lambda b, i: (b, i, 0)),
                  pl.BlockSpec((1, BLOCK, 2 * KV_DIM),
                               lambda b, i: (b, jnp.maximum(i - 1, 0), kv_col)),
                  pl.BlockSpec((1, BLOCK, 2 * KV_DIM),
                               lambda b, i: (b, jnp.minimum(i + 1, nb - 1), kv_col)),
                  pl.BlockSpec((N_Q_HEADS, BLOCK, 3 * BLOCK), lambda b, i: (0, 0, 0)),
                  pl.BlockSpec(memory_space=pltpu.SMEM)],
        out_specs=pl.BlockSpec((1, BLOCK, Q_DIM), lambda b, i: (b, i, 0)),
        out_shape=jax.ShapeDtypeStruct((B, S, Q_DIM), jnp.bfloat16),
        compiler_params=_cparams(("parallel", "arbitrary")),
        name="window_attention",
    )(qkv, qkv, qkv, bias, sink)


def _spatial_gate_kernel(z_ref, g_ref, b_ref, wsp_ref, bsp_ref, o_ref):
    v = z_ref[:, D_GATE:2 * D_GATE].astype(jnp.float32)
    mu = jnp.mean(v, axis=-1, keepdims=True)
    vc = v - mu
    var = jnp.mean(vc * vc, axis=-1, keepdims=True)
    vn = (vc * lax.rsqrt(var + LN_EPS) * g_ref[...] + b_ref[...]).astype(jnp.bfloat16)
    for grp in range(N_SG_GROUPS):
        lo, hi = grp * SG_GROUP_DIM, (grp + 1) * SG_GROUP_DIM
        mixed = jnp.dot(wsp_ref[grp], vn[:, lo:hi], preferred_element_type=jnp.float32)
        mixed = mixed + bsp_ref[:, grp:grp + 1]
        u = z_ref[:, lo:hi].astype(jnp.float32)
        o_ref[:, lo:hi] = (u * mixed).astype(o_ref.dtype)


def spatial_gate(z, ln_g, ln_b, w_sp, b_sp):
    T = z.shape[0]
    return pl.pallas_call(
        _spatial_gate_kernel,
        grid=(T // CHUNK,),
        in_specs=[pl.BlockSpec((CHUNK, 2 * D_GATE), lambda i: (i, 0)),
                  pl.BlockSpec((1, D_GATE), lambda i: (0, 0)),
                  pl.BlockSpec((1, D_GATE), lambda i: (0, 0)),
                  pl.BlockSpec((N_SG_GROUPS, CHUNK, CHUNK), lambda i: (0, 0, 0)),
                  pl.BlockSpec((CHUNK, N_SG_GROUPS), lambda i: (0, 0))],
        out_specs=pl.BlockSpec((CHUNK, D_GATE), lambda i: (i, 0)),
        out_shape=jax.ShapeDtypeStruct((T, D_GATE), jnp.bfloat16),
        compiler_params=_cparams(("parallel",)),
        name="spatial_gate",
    )(z, ln_g.reshape(1, D_GATE), ln_b.reshape(1, D_GATE), w_sp.astype(jnp.bfloat16), b_sp.T)


def _oddeven_merge_sort_pairs(n):
    pairs = []
    p = 1
    while p < n:
        k = p
        while k >= 1:
            for j in range(k % p, n - k, 2 * k):
                for i in range(min(k, n - j - k)):
                    if (i + j) // (2 * p) == (i + j + k) // (2 * p):
                        pairs.append((i + j, i + j + k))
            k //= 2
        p *= 2
    return pairs


_SORT16 = _oddeven_merge_sort_pairs(N_KEYS // SUBLANES)


def _top16_of_keys(s):
    nv = N_KEYS // SUBLANES
    L = s.shape[1]
    sub = lax.broadcasted_iota(jnp.int32, (SUBLANES, L), 0)
    v = [s[j * SUBLANES:(j + 1) * SUBLANES] for j in range(nv)]
    ids = [sub + j * SUBLANES for j in range(nv)]
    for i, j in _SORT16:
        swap = jnp.logical_or(v[j] > v[i], jnp.logical_and(v[j] == v[i], ids[j] < ids[i]))
        v[i], v[j] = jnp.where(swap, v[j], v[i]), jnp.where(swap, v[i], v[j])
        ids[i], ids[j] = jnp.where(swap, ids[j], ids[i]), jnp.where(swap, ids[i], ids[j])
    vals, picks = [], []
    for it in range(PEER_TOPK):
        m = jnp.max(v[0], axis=0, keepdims=True)
        am = jnp.min(jnp.where(v[0] == m, ids[0], N_KEYS), axis=0, keepdims=True)
        hit = ids[0] == am
        vals.append(m)
        picks.append(am)
        last = PEER_TOPK - 1 - it
        for j in range(last):
            v[j] = jnp.where(hit, v[j + 1], v[j])
            ids[j] = jnp.where(hit, ids[j + 1], ids[j])
        v[last] = jnp.where(hit, -jnp.inf, v[last])
    return jnp.concatenate(vals, axis=0), jnp.concatenate(picks, axis=0)


_CAND_ROWS = (
    [(a, 0) for a in range(16)]
    + [None] + [(0, b) for b in range(1, 16)]
    + [None] + [(a, 1) for a in range(1, 8)]
    + [None, None] + [(1, b) for b in range(2, 8)]
    + [(2, 2), (3, 2), (4, 2), (2, 3), (2, 4), (3, 3), None, None]
)
assert sorted(x for x in _CAND_ROWS if x) == sorted(
    (a, b) for a in range(16) for b in range(16) if (a + 1) * (b + 1) <= 16)


def _rows(x, picks):
    pieces, i = [], 0
    while i < len(picks):
        j = i
        while j + 1 < len(picks) and picks[j + 1] == picks[j] + 1:
            j += 1
        pieces.append(x[picks[i]:picks[j] + 1])
        i = j + 1
    return pieces[0] if len(pieces) == 1 else jnp.concatenate(pieces, axis=0)


def _cand_positions(L):
    row = lax.broadcasted_iota(jnp.int32, (len(_CAND_ROWS), L), 0)
    pos = jnp.full(row.shape, _PAD_POS, jnp.int32)
    for r, c in enumerate(_CAND_ROWS):
        if c is not None:
            pos = jnp.where(row == r, c[0] * PEER_TOPK + c[1], pos)
    return pos


_PAD_POS = 1 << 20


def _joint_top16(v1, i1, v2, i2, pos):
    a_of = [c[0] if c else 0 for c in _CAND_ROWS]
    b_of = [c[1] if c else 0 for c in _CAND_ROWS]
    cand = jnp.where(pos < _PAD_POS, _rows(v1, a_of) + _rows(v2, b_of), -jnp.inf)
    cidx = _rows(i1, a_of) * N_KEYS + _rows(i2, b_of)
    vals, picks = [], []
    for _ in range(PEER_TOPK):
        m = jnp.max(cand, axis=0, keepdims=True)
        pm = jnp.min(jnp.where(cand == m, pos, 1 << 21), axis=0, keepdims=True)
        hit = pos == pm
        vals.append(m)
        picks.append(jnp.max(jnp.where(hit, cidx, -1), axis=0, keepdims=True))
        cand = jnp.where(hit, -jnp.inf, cand)
    return jnp.concatenate(vals, axis=0), jnp.concatenate(picks, axis=0)


def _route_kernel(q_ref, keys_ref, idx_ref, gate_ref, *, precision):
    nt = (((1,), (1,)), ((), ()))
    pos = _cand_positions(LANES)
    for j in range(q_ref.shape[0] // LANES):
        q = q_ref[j * LANES:(j + 1) * LANES, :]
        s1 = lax.dot_general(keys_ref[0, 0], q[:, 0:D_HALF], nt,
                             preferred_element_type=jnp.float32, precision=precision)
        s2 = lax.dot_general(keys_ref[0, 1], q[:, D_HALF:2 * D_HALF], nt,
                             preferred_element_type=jnp.float32, precision=precision)
        v1, i1 = _top16_of_keys(s1)
        v2, i2 = _top16_of_keys(s2)
        top_s, top_i = _joint_top16(v1, i1, v2, i2, pos)
        e = jnp.exp(top_s - top_s[0:1])
        gate_ref[:, j * LANES:(j + 1) * LANES] = e / jnp.sum(e, axis=0, keepdims=True)
        idx_ref[:, j * LANES:(j + 1) * LANES] = top_i * ROW_SUB


def peer_route(q, sub_keys, *, tb=512, precision=None):
    T = q.shape[0]
    tb = min(tb, T)
    return pl.pallas_call(
        functools.partial(_route_kernel, precision=precision),
        grid=(T // tb, PEER_HEADS),
        in_specs=[pl.BlockSpec((tb, 2 * D_HALF), lambda i, h: (i, h)),
                  pl.BlockSpec((1, 2, N_KEYS, D_HALF), lambda i, h: (h, 0, 0, 0))],
        out_specs=[pl.BlockSpec((PEER_TOPK, tb), lambda i, h: (h, i)),
                   pl.BlockSpec((PEER_TOPK, tb), lambda i, h: (h, i))],
        out_shape=[jax.ShapeDtypeStruct((N_SLOTS, T), jnp.int32),
                   jax.ShapeDtypeStruct((N_SLOTS, T), jnp.float32)],
        compiler_params=_cparams(("parallel", "arbitrary")),
        name="peer_route",
    )(q, sub_keys)


def _pack_kernel(x_ref, o_ref, *, split_rows):
    x = x_ref[...]
    half = x.shape[1] // 2

    def bf16_bits(v):
        return pltpu.bitcast(v.astype(jnp.bfloat16).astype(jnp.float32), jnp.uint32)

    words = (bf16_bits(x[:, :half]) >> 16) | (bf16_bits(x[:, half:]) & jnp.uint32(0xFFFF0000))
    if split_rows:
        n = x.shape[0]
        for c in range(ROW_SUB):
            o_ref[pl.ds(c, n, stride=ROW_SUB), :] = words[:, c * LANES:(c + 1) * LANES]
    else:
        o_ref[...] = words


def pack_table(tbls, layer, *, split_rows, te=512):
    _, E, D = tbls.shape
    out_block, out_shape = ((te * ROW_SUB, LANES), (E * ROW_SUB, LANES)) if split_rows else ((te, D // 2), (E, D // 2))
    return pl.pallas_call(
        functools.partial(_pack_kernel, split_rows=split_rows),
        grid=(E // te,),
        in_specs=[pl.BlockSpec((None, te, D), lambda i: (layer, i, 0))],
        out_specs=pl.BlockSpec(out_block, lambda i: (i, 0)),
        out_shape=jax.ShapeDtypeStruct(out_shape, jnp.uint32),
        compiler_params=_cparams(("parallel",)),
        name="pack_table",
    )(tbls)


def _unpack(words):
    lo = pltpu.bitcast(words << 16, jnp.float32)
    hi = pltpu.bitcast(words & jnp.uint32(0xFFFF0000), jnp.float32)
    return lo, hi


def _gather_rows(idx_ref, tbl_ref, t, planes_ref):
    for k in range(N_SLOTS):
        off = pl.multiple_of(idx_ref[t, k], ROW_SUB)
        planes_ref[pl.ds(k, ROW_SUB, stride=PLANE_STRIDE), :] = tbl_ref[pl.ds(off, ROW_SUB), :]


def _plane(planes_ref, c):
    return _unpack(planes_ref[c * PLANE_STRIDE:c * PLANE_STRIDE + N_SLOTS, :])


def _peer_down_kernel(idx_ref, x_ref, gate_ref, tbl_ref, o_ref, planes_a, planes_b, part_ref):
    tb = x_ref.shape[0]
    lane = lax.broadcasted_iota(jnp.int32, (N_SLOTS, tb), 1)

    def compute(t, planes_ref, slot):
        acc = jnp.zeros((N_SLOTS, LANES), jnp.float32)
        for c in range(ROW_SUB):
            lo, hi = _plane(planes_ref, c)
            acc = acc + lo * x_ref[t, c:c + 1, :] + hi * x_ref[t, ROW_SUB + c:ROW_SUB + c + 1, :]
        part_ref[slot] = acc

    def place_pair(t0):
        col_a = jnp.sum(part_ref[0], axis=1, keepdims=True)
        col_b = jnp.sum(part_ref[1], axis=1, keepdims=True)
        o_ref[...] = jnp.where(lane == t0 - 2, col_a, jnp.where(lane == t0 - 1, col_b, o_ref[...]))

    o_ref[...] = jnp.zeros(o_ref.shape, jnp.float32)
    part_ref[...] = jnp.zeros(part_ref.shape, jnp.float32)
    _gather_rows(idx_ref, tbl_ref, 0, planes_a)

    def pair(i, carry):
        t0 = 2 * i
        place_pair(t0)
        _gather_rows(idx_ref, tbl_ref, t0 + 1, planes_b)
        compute(t0, planes_a, 0)
        _gather_rows(idx_ref, tbl_ref, jnp.minimum(t0 + 2, tb - 1), planes_a)
        compute(t0 + 1, planes_b, 1)
        return carry

    lax.fori_loop(0, tb // 2, pair, 0)
    place_pair(tb)
    o_ref[...] = _gelu(o_ref[...]) * gate_ref[...]


def peer_down(idx, xn3, gate_t, tbl, *, tb=128):
    T = xn3.shape[0]
    tb = min(tb, T)
    assert tb % 2 == 0 and T % tb == 0
    planes = pltpu.VMEM((ROW_SUB * PLANE_STRIDE, LANES), jnp.uint32)
    return pl.pallas_call(
        _peer_down_kernel,
        grid=(T // tb,),
        in_specs=[pl.BlockSpec((tb, N_SLOTS), lambda i: (i, 0), memory_space=pltpu.SMEM),
                  pl.BlockSpec((tb, SUBLANES, LANES), lambda i: (i, 0, 0)),
                  pl.BlockSpec((N_SLOTS, tb), lambda i: (0, i)),
                  pl.BlockSpec(tbl.shape, lambda i: (0, 0), pipeline_mode=pl.Buffered(1))],
        out_specs=pl.BlockSpec((N_SLOTS, tb), lambda i: (0, i)),
        out_shape=jax.ShapeDtypeStruct((N_SLOTS, T), jnp.float32),
        scratch_shapes=[planes, planes, pltpu.VMEM((2, N_SLOTS, LANES), jnp.float32)],
        compiler_params=_cparams(("arbitrary",)),
        name="peer_down",
    )(idx, xn3, gate_t, tbl)


SC_CORES = 2
SC_SUBCORES = 16
SC_WORKERS = SC_CORES * SC_SUBCORES
SC_LANES = 16
SC_GATHER_ROWS = N_SLOTS // 2
SC_ROW_BLOCK = 8
SC_TOKEN_BLOCK = 32
SC_COL_GROUP = 8


def _sc_params():
    cp = pltpu.CompilerParams()
    if "needs_layout_passes" in pltpu.CompilerParams.__dataclass_fields__:
        cp = dataclasses.replace(cp, needs_layout_passes=False)
    return cp


def peer_up_sc(ids, w, tbl):
    T = ids.shape[0]
    L, CH, RB, TBK = SC_LANES, SC_GATHER_ROWS, SC_ROW_BLOCK, SC_TOKEN_BLOCK
    assert T % (SC_WORKERS * TBK) == 0
    tpw = T // SC_WORKERS
    mesh = plsc.VectorSubcoreMesh(core_axis_name="c", subcore_axis_name="s")
    rows_buf = pltpu.VMEM((CH, ROW_WORDS), jnp.uint32)
    y_buf = pltpu.VMEM((D_MODEL,), jnp.float32)

    @functools.partial(
        pl.kernel, mesh=mesh, compiler_params=_sc_params(),
        out_type=jax.ShapeDtypeStruct((T, D_MODEL), jnp.float32),
        scratch_types=[pltpu.VMEM((TBK, N_SLOTS), jnp.int32), pltpu.VMEM((TBK, N_SLOTS), jnp.float32),
                       rows_buf, rows_buf, y_buf, y_buf] + [pltpu.SemaphoreType.DMA] * 4,
    )
    def up_kernel(ids_hbm, w_hbm, tbl_hbm, out_hbm, ids_v, w_v, rows0, rows1, y0, y1, g0, g1, o0, o1):
        base = (lax.axis_index("s") * SC_CORES + lax.axis_index("c")) * tpw
        rows, gsem, ys, osem = (rows0, rows1), (g0, g1), (y0, y1), (o0, o1)

        def gather(tl, half):
            return pltpu.make_async_copy(tbl_hbm.at[ids_v.at[tl, pl.ds(half * CH, CH)]], rows[half], gsem[half])

        def out_copy(tok, slot):
            return pltpu.make_async_copy(ys[slot], out_hbm.at[tok], osem[slot])

        def accumulate(tl, half, y_v):
            rows_v = rows[half]

            def block(b, carry):
                r0 = b * RB
                tlv = jnp.full((L,), tl, jnp.int32)
                wks = [plsc.load_gather(w_v, [tlv, jnp.full((L,), half * CH + r0 + r, jnp.int32)])
                       for r in range(RB)]
                for q in range(ROW_WORDS // (SC_COL_GROUP * L)):
                    acc = [None] * (2 * SC_COL_GROUP)
                    for r in range(RB):
                        for i in range(SC_COL_GROUP):
                            v = rows_v[r0 + r, pl.ds((q * SC_COL_GROUP + i) * L, L)]
                            lo = plsc.bitcast(v << 16, jnp.float32) * wks[r]
                            hi = plsc.bitcast(v & jnp.uint32(0xFFFF0000), jnp.float32) * wks[r]
                            acc[2 * i] = lo if r == 0 else acc[2 * i] + lo
                            acc[2 * i + 1] = hi if r == 0 else acc[2 * i + 1] + hi
                    for i in range(SC_COL_GROUP):
                        col = (q * SC_COL_GROUP + i) * L
                        plsc.addupdate(y_v.at[pl.ds(col, L)], acc[2 * i])
                        plsc.addupdate(y_v.at[pl.ds(ROW_WORDS + col, L)], acc[2 * i + 1])
                return carry

            lax.fori_loop(0, CH // RB, block, 0)

        def token_block(bi, carry):
            tok0 = base + bi * TBK
            pltpu.sync_copy(ids_hbm.at[pl.ds(tok0, TBK)], ids_v)
            pltpu.sync_copy(w_hbm.at[pl.ds(tok0, TBK)], w_v)
            gather(0, 0).start()

            def token_pair(pi, carry2):
                for slot in range(2):
                    tl = 2 * pi + slot
                    y_v = ys[slot]

                    @pl.when(pi > 0)
                    def _():
                        out_copy(tok0 + tl - 2, slot).wait()

                    zero = jnp.zeros((L,), jnp.float32)
                    for j in range(D_MODEL // L):
                        y_v[pl.ds(j * L, L)] = zero
                    gather(tl, 1).start()
                    gather(tl, 0).wait()
                    accumulate(tl, 0, y_v)

                    @pl.when(tl + 1 < TBK)
                    def _():
                        gather(tl + 1, 0).start()

                    gather(tl, 1).wait()
                    accumulate(tl, 1, y_v)
                    out_copy(tok0 + tl, slot).start()
                return carry2

            lax.fori_loop(0, TBK // 2, token_pair, 0)
            out_copy(tok0 + TBK - 2, 0).wait()
            out_copy(tok0 + TBK - 1, 1).wait()
            return carry

        lax.fori_loop(0, tpw // TBK, token_block, 0)

    return up_kernel(ids, w, tbl)


def peer_layer(h, g, w_query, sub_keys, down_rows, up_rows):
    T = h.shape[0]
    q, xn = norm_matmul(h, g, w_query, out_dtype=jnp.float32, emit_xn=True)
    idx_t, gate_t = peer_route(q, sub_keys, precision=lax.Precision.HIGHEST)
    idx = idx_t.T
    w_t = peer_down(idx, xn.reshape(T, SUBLANES, LANES), gate_t, down_rows)
    return peer_up_sc(idx // ROW_SUB, w_t.T, up_rows)


def _rms_kernel(x_ref, r_ref, g_ref, o_ref):
    x = x_ref[...] + r_ref[...]
    o_ref[...] = x * lax.rsqrt(jnp.mean(x * x, axis=-1, keepdims=True) + NORM_EPS) * g_ref[...]


def rms_norm_sum(x, r, g, *, tm=512):
    T, D = x.shape
    tm = min(tm, T)
    row = pl.BlockSpec((tm, D), lambda i: (i, 0))
    return pl.pallas_call(
        _rms_kernel,
        grid=(T // tm,),
        in_specs=[row, row, pl.BlockSpec((1, D), lambda i: (0, 0))],
        out_specs=row,
        out_shape=jax.ShapeDtypeStruct((T, D), jnp.float32),
        compiler_params=_cparams(("parallel",)),
        name="final_norm",
    )(x, r, g.reshape(1, D))


PIECE_SEQS = 2


def _piece_bounds(B):
    sizes = [PIECE_SEQS] * (B // PIECE_SEQS) + ([B % PIECE_SEQS] if B % PIECE_SEQS else [])
    if sizes[-1] > 1:
        sizes[-1:] = [sizes[-1] - sizes[-1] // 2, sizes[-1] // 2]
    starts = [sum(sizes[:i]) for i in range(len(sizes))]
    return list(zip(starts, sizes))


def kernel(x, rel_table, mix_norm_g, attn_w_in, attn_w_out, attn_sink, sg_w_in, sg_ln_g, sg_ln_b, sg_w_spatial, sg_b_spatial, sg_w_out, ffn_norm_g, peer_w_query, peer_sub_keys, peer_down, peer_up, final_norm_g):
    B, S, D = x.shape
    bf16 = jnp.bfloat16

    bias = attn_bias(rel_table)
    attn_in, attn_out = attn_w_in[0].astype(bf16), attn_w_out[0].astype(bf16)
    sg_in, sg_out = sg_w_in[0].astype(bf16), sg_w_out[0].astype(bf16)
    w_query = peer_w_query.astype(bf16)
    down_rows = [pack_table(peer_down, i, split_rows=True) for i in range(2)]
    up_rows = [pack_table(peer_up, i, split_rows=False) for i in range(2)]

    mid = []
    for b0, nb in _piece_bounds(B):
        h = x[b0:b0 + nb].reshape(nb * S, D)
        qkv, = norm_matmul(h, mix_norm_g[0], attn_in, out_dtype=bf16)
        att = window_attention(qkv.reshape(nb, S, QKV_DIM), bias, attn_sink[0])
        h = matmul_residual(att.reshape(nb * S, Q_DIM), attn_out, h)
        mid.append((h, peer_layer(h, ffn_norm_g[0], w_query[0], peer_sub_keys[0], down_rows[0], up_rows[0])))

    outs = []
    for h, y in mid:
        z, h = norm_matmul(h, mix_norm_g[1], sg_in, res=y, out_dtype=bf16, act=True, tn=1536)
        gated = spatial_gate(z, sg_ln_g[0], sg_ln_b[0], sg_w_spatial[0], sg_b_spatial[0])
        h = matmul_residual(gated, sg_out, h)
        y = peer_layer(h, ffn_norm_g[1], w_query[1], peer_sub_keys[1], down_rows[1], up_rows[1])
        outs.append(rms_norm_sum(h, y, final_norm_g).reshape(-1, S, D))
    return jnp.concatenate(outs, axis=0)
```

```python
import dataclasses
import functools
import math

import jax
import jax.numpy as jnp
from jax import lax
from jax.experimental import pallas as pl
from jax.experimental.pallas import tpu as pltpu
from jax.experimental.pallas import tpu_sc as plsc

D_MODEL = 1024
HEAD_DIM = 64
N_Q_HEADS = 16
N_KV_HEADS = 4
GQA_GROUP = 4
WINDOW = 128
BLOCK = 128
REL_BUCKETS = 32
REL_MAX_DIST = 128
Q_DIM = N_Q_HEADS * HEAD_DIM
KV_DIM = N_KV_HEADS * HEAD_DIM
QKV_DIM = Q_DIM + 2 * KV_DIM
CHUNK = 128
D_GATE = 3072
N_SG_GROUPS = 8
SG_GROUP_DIM = D_GATE // N_SG_GROUPS
N_KEYS = 128
PEER_HEADS = 8
PEER_TOPK = 16
D_HALF = 128
N_SLOTS = PEER_HEADS * PEER_TOPK
NORM_EPS = 1e-6
LN_EPS = 1e-5

LANES = 128
SUBLANES = 8
ROW_WORDS = D_MODEL // 2
ROW_SUB = ROW_WORDS // LANES
PLANE_STRIDE = 136
VMEM_LIMIT = 56 * 1024 * 1024

_GELU_C = math.sqrt(2.0 / math.pi)


def _gelu(x):
    return 0.5 * x * (1.0 + jnp.tanh(_GELU_C * (x + 0.044715 * (x * x * x))))


def _cparams(sem):
    return pltpu.CompilerParams(dimension_semantics=sem, vmem_limit_bytes=VMEM_LIMIT)


def _norm_matmul_kernel(*refs, act, emit_xn, add_res, precision):
    refs = list(refs)
    x_ref = refs.pop(0)
    r_ref = refs.pop(0) if add_res else None
    g_ref, w_ref, o_ref = refs.pop(0), refs.pop(0), refs.pop(0)
    xn_ref = refs.pop(0) if emit_xn else None
    sum_ref = refs.pop(0) if add_res else None
    xs_ref = refs.pop(0)

    @pl.when(pl.program_id(1) == 0)
    def _():
        x = x_ref[...]
        if add_res:
            x = x + r_ref[...]
            sum_ref[...] = x
        y = x * lax.rsqrt(jnp.mean(x * x, axis=-1, keepdims=True) + NORM_EPS) * g_ref[...]
        xs_ref[...] = y.astype(xs_ref.dtype)
        if emit_xn:
            xn_ref[...] = y

    acc = jnp.dot(xs_ref[...], w_ref[...], preferred_element_type=jnp.float32,
                  precision=precision)
    if act:
        acc = _gelu(acc)
    o_ref[...] = acc.astype(o_ref.dtype)


def norm_matmul(x, g, w, *, out_dtype, res=None, act=False, emit_xn=False, tm=512, tn=None,
                precision=None):
    T, D = x.shape
    N = w.shape[1]
    tn = tn or N
    tm = min(tm, T)
    row = pl.BlockSpec((tm, D), lambda i, j: (i, 0))
    out_shape = [jax.ShapeDtypeStruct((T, N), out_dtype)]
    out_specs = [pl.BlockSpec((tm, tn), lambda i, j: (i, j))]
    for flag in (emit_xn, res is not None):
        if flag:
            out_shape.append(jax.ShapeDtypeStruct((T, D), jnp.float32))
            out_specs.append(row)
    args = [x] + ([res] if res is not None else []) + [g.reshape(1, D), w]
    in_specs = [row] * (len(args) - 2) + [pl.BlockSpec((1, D), lambda i, j: (0, 0)),
                                          pl.BlockSpec((D, tn), lambda i, j: (0, j))]
    return pl.pallas_call(
        functools.partial(_norm_matmul_kernel, act=act, emit_xn=emit_xn, add_res=res is not None,
                          precision=precision),
        grid=(T // tm, N // tn),
        in_specs=in_specs,
        out_specs=out_specs,
        out_shape=out_shape,
        scratch_shapes=[pltpu.VMEM((tm, D), w.dtype)],
        compiler_params=_cparams(("parallel", "arbitrary")),
        name="norm_matmul",
    )(*args)


def _matmul_res_kernel(a_ref, w_ref, h_ref, o_ref):
    o_ref[...] = h_ref[...] + jnp.dot(a_ref[...], w_ref[...],
                                      preferred_element_type=jnp.float32)


def matmul_residual(a, w, h, *, tm=512):
    T, K = a.shape
    N = w.shape[1]
    tm = min(tm, T)
    return pl.pallas_call(
        _matmul_res_kernel,
        grid=(T // tm,),
        in_specs=[pl.BlockSpec((tm, K), lambda i: (i, 0)),
                  pl.BlockSpec((K, N), lambda i: (0, 0)),
                  pl.BlockSpec((tm, N), lambda i: (i, 0))],
        out_specs=pl.BlockSpec((tm, N), lambda i: (i, 0)),
        out_shape=jax.ShapeDtypeStruct((T, N), jnp.float32),
        compiler_params=_cparams(("parallel",)),
        name="matmul_residual",
    )(a, w, h)


def _t5_bucket(rel):
    nb = REL_BUCKETS // 2
    max_exact = nb // 2
    ret = jnp.where(rel > 0, nb, 0)
    n = jnp.abs(rel)
    nf = jnp.maximum(n, 1).astype(jnp.float32)
    large = max_exact + (jnp.log(nf / max_exact) / math.log(REL_MAX_DIST / max_exact)
                         * (nb - max_exact)).astype(jnp.int32)
    large = jnp.minimum(large, nb - 1)
    return (ret + jnp.where(n < max_exact, n, large)).astype(jnp.int32)


def _bias_kernel(bucket_ref, window_ref, table_ref, o_ref):
    bucket = bucket_ref[...]
    in_window = window_ref[...] > 0
    for hq in range(N_Q_HEADS):
        acc = jnp.zeros(bucket.shape, jnp.float32)
        for b in range(REL_BUCKETS):
            acc = jnp.where(bucket == b, table_ref[b, hq], acc)
        o_ref[hq] = jnp.where(in_window, acc, -jnp.inf)


def attn_bias(rel_table):
    qi = jnp.arange(BLOCK)[:, None]
    kj = jnp.arange(3 * BLOCK)[None, :]
    rel = kj - BLOCK - qi
    bucket = _t5_bucket(rel)
    window = (jnp.abs(rel) <= WINDOW).astype(jnp.int32)
    return pl.pallas_call(
        _bias_kernel,
        in_specs=[pl.BlockSpec(memory_space=pltpu.VMEM),
                  pl.BlockSpec(memory_space=pltpu.VMEM),
                  pl.BlockSpec(memory_space=pltpu.SMEM)],
        out_specs=pl.BlockSpec(memory_space=pltpu.VMEM),
        out_shape=jax.ShapeDtypeStruct((N_Q_HEADS, BLOCK, 3 * BLOCK), jnp.float32),
        name="attn_bias",
    )(bucket, window, rel_table)


def _attn_kernel(cur_ref, prev_ref, next_ref, bias_ref, sink_ref, o_ref):
    i = pl.program_id(1)
    nb = pl.num_programs(1)
    q = cur_ref[0, :, 0:Q_DIM]
    kband = jnp.concatenate([prev_ref[0, :, 0:KV_DIM], cur_ref[0, :, Q_DIM:Q_DIM + KV_DIM],
                             next_ref[0, :, 0:KV_DIM]], axis=0)
    vband = jnp.concatenate([prev_ref[0, :, KV_DIM:2 * KV_DIM], cur_ref[0, :, Q_DIM + KV_DIM:QKV_DIM],
                             next_ref[0, :, KV_DIM:2 * KV_DIM]], axis=0)
    col = lax.broadcasted_iota(jnp.int32, (1, 3 * BLOCK), 1)
    valid = jnp.logical_and(jnp.logical_or(col >= BLOCK, i > 0),
                            jnp.logical_or(col < 2 * BLOCK, i < nb - 1))
    heads = range(N_Q_HEADS)
    kv = [(kband[:, hk * HEAD_DIM:(hk + 1) * HEAD_DIM], vband[:, hk * HEAD_DIM:(hk + 1) * HEAD_DIM])
          for hk in range(N_KV_HEADS)]
    s = []
    for hq in heads:
        qh = q[:, hq * HEAD_DIM:(hq + 1) * HEAD_DIM]
        sc = lax.dot_general(qh, kv[hq // GQA_GROUP][0], (((1,), (1,)), ((), ())),
                             preferred_element_type=jnp.float32)
        s.append(jnp.where(valid, sc * (HEAD_DIM ** -0.5) + bias_ref[hq], -jnp.inf))
    mx = [jnp.maximum(jnp.max(s[hq], axis=-1, keepdims=True), sink_ref[hq]) for hq in heads]
    p = [jnp.exp(s[hq] - mx[hq]) for hq in heads]
    denom = [jnp.sum(p[hq], axis=-1, keepdims=True) + jnp.exp(sink_ref[hq] - mx[hq]) for hq in heads]
    outs = [jnp.dot((p[hq] / denom[hq]).astype(jnp.bfloat16), kv[hq // GQA_GROUP][1],
                    preferred_element_type=jnp.float32) for hq in heads]
    o_ref[0] = jnp.concatenate(outs, axis=-1).astype(o_ref.dtype)


def window_attention(qkv, bias, sink):
    B, S, _ = qkv.shape
    nb = S // BLOCK
    kv_col = Q_DIM // (2 * KV_DIM)
    return pl.pallas_call(
        _attn_kernel,
        grid=(B, nb),
        in_specs=[pl.BlockSpec((1, BLOCK, QKV_DIM), lambda b, i: (b, i, 0)),
                  pl.BlockSpec((1, BLOCK, 2 * KV_DIM),
                               lambda b, i: (b, jnp.maximum(i - 1, 0), kv_col)),
                  pl.BlockSpec((1, BLOCK, 2 * KV_DIM),
                               lambda b, i: (b, jnp.minimum(i + 1, nb - 1), kv_col)),
                  pl.BlockSpec((N_Q_HEADS, BLOCK, 3 * BLOCK), lambda b, i: (0, 0, 0)),
                  pl.BlockSpec(memory_space=pltpu.SMEM)],
        out_specs=pl.BlockSpec((1, BLOCK, Q_DIM), lambda b, i: (b, i, 0)),
        out_shape=jax.ShapeDtypeStruct((B, S, Q_DIM), jnp.bfloat16),
        compiler_params=_cparams(("parallel", "arbitrary")),
        name="window_attention",
    )(qkv, qkv, qkv, bias, sink)


def _spatial_gate_kernel(z_ref, g_ref, b_ref, wsp_ref, bsp_ref, o_ref):
    v = z_ref[:, D_GATE:2 * D_GATE].astype(jnp.float32)
    mu = jnp.mean(v, axis=-1, keepdims=True)
    vc = v - mu
    var = jnp.mean(vc * vc, axis=-1, keepdims=True)
    vn = (vc * lax.rsqrt(var + LN_EPS) * g_ref[...] + b_ref[...]).astype(jnp.bfloat16)
    for grp in range(N_SG_GROUPS):
        lo, hi = grp * SG_GROUP_DIM, (grp + 1) * SG_GROUP_DIM
        mixed = jnp.dot(wsp_ref[grp], vn[:, lo:hi], preferred_element_type=jnp.float32)
        mixed = mixed + bsp_ref[:, grp:grp + 1]
        u = z_ref[:, lo:hi].astype(jnp.float32)
        o_ref[:, lo:hi] = (u * mixed).astype(o_ref.dtype)


def spatial_gate(z, ln_g, ln_b, w_sp, b_sp):
    T = z.shape[0]
    return pl.pallas_call(
        _spatial_gate_kernel,
        grid=(T // CHUNK,),
        in_specs=[pl.BlockSpec((CHUNK, 2 * D_GATE), lambda i: (i, 0)),
                  pl.BlockSpec((1, D_GATE), lambda i: (0, 0)),
                  pl.BlockSpec((1, D_GATE), lambda i: (0, 0)),
                  pl.BlockSpec((N_SG_GROUPS, CHUNK, CHUNK), lambda i: (0, 0, 0)),
                  pl.BlockSpec((CHUNK, N_SG_GROUPS), lambda i: (0, 0))],
        out_specs=pl.BlockSpec((CHUNK, D_GATE), lambda i: (i, 0)),
        out_shape=jax.ShapeDtypeStruct((T, D_GATE), jnp.bfloat16),
        compiler_params=_cparams(("parallel",)),
        name="spatial_gate",
    )(z, ln_g.reshape(1, D_GATE), ln_b.reshape(1, D_GATE), w_sp.astype(jnp.bfloat16), b_sp.T)


def _oddeven_merge_sort_pairs(n):
    pairs = []
    p = 1
    while p < n:
        k = p
        while k >= 1:
            for j in range(k % p, n - k, 2 * k):
                for i in range(min(k, n - j - k)):
                    if (i + j) // (2 * p) == (i + j + k) // (2 * p):
                        pairs.append((i + j, i + j + k))
            k //= 2
        p *= 2
    return pairs


_SORT16 = _oddeven_merge_sort_pairs(N_KEYS // SUBLANES)


def _top16_of_keys(s):
    nv = N_KEYS // SUBLANES
    L = s.shape[1]
    sub = lax.broadcasted_iota(jnp.int32, (SUBLANES, L), 0)
    v = [s[j * SUBLANES:(j + 1) * SUBLANES] for j in range(nv)]
    ids = [sub + j * SUBLANES for j in range(nv)]
    for i, j in _SORT16:
        swap = jnp.logical_or(v[j] > v[i], jnp.logical_and(v[j] == v[i], ids[j] < ids[i]))
        v[i], v[j] = jnp.where(swap, v[j], v[i]), jnp.where(swap, v[i], v[j])
        ids[i], ids[j] = jnp.where(swap, ids[j], ids[i]), jnp.where(swap, ids[i], ids[j])
    vals, picks = [], []
    for it in range(PEER_TOPK):
        m = jnp.max(v[0], axis=0, keepdims=True)
        am = jnp.min(jnp.where(v[0] == m, ids[0], N_KEYS), axis=0, keepdims=True)
        hit = ids[0] == am
        vals.append(m)
        picks.append(am)
        last = PEER_TOPK - 1 - it
        for j in range(last):
            v[j] = jnp.where(hit, v[j + 1], v[j])
            ids[j] = jnp.where(hit, ids[j + 1], ids[j])
        v[last] = jnp.where(hit, -jnp.inf, v[last])
    return jnp.concatenate(vals, axis=0), jnp.concatenate(picks, axis=0)


_CAND_ROWS = (
    [(a, 0) for a in range(16)]
    + [None] + [(0, b) for b in range(1, 16)]
    + [None] + [(a, 1) for a in range(1, 8)]
    + [None, None] + [(1, b) for b in range(2, 8)]
    + [(2, 2), (3, 2), (4, 2), (2, 3), (2, 4), (3, 3), None, None]
)
assert sorted(x for x in _CAND_ROWS if x) == sorted(
    (a, b) for a in range(16) for b in range(16) if (a + 1) * (b + 1) <= 16)


def _rows(x, picks):
    pieces, i = [], 0
    while i < len(picks):
        j = i
        while j + 1 < len(picks) and picks[j + 1] == picks[j] + 1:
            j += 1
        pieces.append(x[picks[i]:picks[j] + 1])
        i = j + 1
    return pieces[0] if len(pieces) == 1 else jnp.concatenate(pieces, axis=0)


def _cand_positions(L):
    row = lax.broadcasted_iota(jnp.int32, (len(_CAND_ROWS), L), 0)
    pos = jnp.full(row.shape, _PAD_POS, jnp.int32)
    for r, c in enumerate(_CAND_ROWS):
        if c is not None:
            pos = jnp.where(row == r, c[0] * PEER_TOPK + c[1], pos)
    return pos


_PAD_POS = 1 << 20


def _joint_top16(v1, i1, v2, i2, pos):
    a_of = [c[0] if c else 0 for c in _CAND_ROWS]
    b_of = [c[1] if c else 0 for c in _CAND_ROWS]
    cand = jnp.where(pos < _PAD_POS, _rows(v1, a_of) + _rows(v2, b_of), -jnp.inf)
    cidx = _rows(i1, a_of) * N_KEYS + _rows(i2, b_of)
    vals, picks = [], []
    for _ in range(PEER_TOPK):
        m = jnp.max(cand, axis=0, keepdims=True)
        pm = jnp.min(jnp.where(cand == m, pos, 1 << 21), axis=0, keepdims=True)
        hit = pos == pm
        vals.append(m)
        picks.append(jnp.max(jnp.where(hit, cidx, -1), axis=0, keepdims=True))
        cand = jnp.where(hit, -jnp.inf, cand)
    return jnp.concatenate(vals, axis=0), jnp.concatenate(picks, axis=0)


def _route_kernel(q_ref, keys_ref, idx_ref, gate_ref, *, precision):
    nt = (((1,), (1,)), ((), ()))
    pos = _cand_positions(LANES)
    for j in range(q_ref.shape[0] // LANES):
        q = q_ref[j * LANES:(j + 1) * LANES, :]
        s1 = lax.dot_general(keys_ref[0, 0], q[:, 0:D_HALF], nt,
                             preferred_element_type=jnp.float32, precision=precision)
        s2 = lax.dot_general(keys_ref[0, 1], q[:, D_HALF:2 * D_HALF], nt,
                             preferred_element_type=jnp.float32, precision=precision)
        v1, i1 = _top16_of_keys(s1)
        v2, i2 = _top16_of_keys(s2)
        top_s, top_i = _joint_top16(v1, i1, v2, i2, pos)
        e = jnp.exp(top_s - top_s[0:1])
        gate_ref[:, j * LANES:(j + 1) * LANES] = e / jnp.sum(e, axis=0, keepdims=True)
        idx_ref[:, j * LANES:(j + 1) * LANES] = top_i * ROW_SUB


def peer_route(q, sub_keys, *, tb=512, precision=None):
    T = q.shape[0]
    tb = min(tb, T)
    return pl.pallas_call(
        functools.partial(_route_kernel, precision=precision),
        grid=(T // tb, PEER_HEADS),
        in_specs=[pl.BlockSpec((tb, 2 * D_HALF), lambda i, h: (i, h)),
                  pl.BlockSpec((1, 2, N_KEYS, D_HALF), lambda i, h: (h, 0, 0, 0))],
        out_specs=[pl.BlockSpec((PEER_TOPK, tb), lambda i, h: (h, i)),
                   pl.BlockSpec((PEER_TOPK, tb), lambda i, h: (h, i))],
        out_shape=[jax.ShapeDtypeStruct((N_SLOTS, T), jnp.int32),
                   jax.ShapeDtypeStruct((N_SLOTS, T), jnp.float32)],
        compiler_params=_cparams(("parallel", "arbitrary")),
        name="peer_route",
    )(q, sub_keys)


def _pack_kernel(x_ref, *o_refs, wide, split):
    x = x_ref[...]
    half = x.shape[1] // 2

    def bf16_bits(v):
        return pltpu.bitcast(v.astype(jnp.bfloat16).astype(jnp.float32), jnp.uint32)

    words = (bf16_bits(x[:, :half]) >> 16) | (bf16_bits(x[:, half:]) & jnp.uint32(0xFFFF0000))
    o_refs = list(o_refs)
    if wide:
        o_refs.pop(0)[...] = words
    if split:
        o_ref, n = o_refs.pop(0), x.shape[0]
        for c in range(ROW_SUB):
            o_ref[pl.ds(c, n, stride=ROW_SUB), :] = words[:, c * LANES:(c + 1) * LANES]


def pack_table(tbls, layer, *, wide, split, te=512):
    _, E, D = tbls.shape
    specs, shapes = [], []
    if wide:
        specs.append(pl.BlockSpec((te, D // 2), lambda i: (i, 0)))
        shapes.append(jax.ShapeDtypeStruct((E, D // 2), jnp.uint32))
    if split:
        specs.append(pl.BlockSpec((te * ROW_SUB, LANES), lambda i: (i, 0)))
        shapes.append(jax.ShapeDtypeStruct((E * ROW_SUB, LANES), jnp.uint32))
    return pl.pallas_call(
        functools.partial(_pack_kernel, wide=wide, split=split),
        grid=(E // te,),
        in_specs=[pl.BlockSpec((None, te, D), lambda i: (layer, i, 0))],
        out_specs=specs,
        out_shape=shapes,
        compiler_params=_cparams(("parallel",)),
        name="pack_table",
    )(tbls)


def _unpack(words):
    lo = pltpu.bitcast(words << 16, jnp.float32)
    hi = pltpu.bitcast(words & jnp.uint32(0xFFFF0000), jnp.float32)
    return lo, hi


def _gather_rows(idx_ref, tbl_ref, t, planes_ref):
    for k in range(N_SLOTS):
        off = pl.multiple_of(idx_ref[t, k], ROW_SUB)
        planes_ref[pl.ds(k, ROW_SUB, stride=PLANE_STRIDE), :] = tbl_ref[pl.ds(off, ROW_SUB), :]


def _plane(planes_ref, c):
    return _unpack(planes_ref[c * PLANE_STRIDE:c * PLANE_STRIDE + N_SLOTS, :])


def _peer_down_kernel(idx_ref, x_ref, gate_ref, tbl_ref, o_ref, planes_a, planes_b, part_ref):
    tb = x_ref.shape[0]
    lane = lax.broadcasted_iota(jnp.int32, (N_SLOTS, tb), 1)

    def compute(t, planes_ref, slot):
        acc = jnp.zeros((N_SLOTS, LANES), jnp.float32)
        for c in range(ROW_SUB):
            lo, hi = _plane(planes_ref, c)
            acc = acc + lo * x_ref[t, c:c + 1, :] + hi * x_ref[t, ROW_SUB + c:ROW_SUB + c + 1, :]
        part_ref[slot] = acc

    def place_pair(t0):
        col_a = jnp.sum(part_ref[0], axis=1, keepdims=True)
        col_b = jnp.sum(part_ref[1], axis=1, keepdims=True)
        o_ref[...] = jnp.where(lane == t0 - 2, col_a, jnp.where(lane == t0 - 1, col_b, o_ref[...]))

    o_ref[...] = jnp.zeros(o_ref.shape, jnp.float32)
    part_ref[...] = jnp.zeros(part_ref.shape, jnp.float32)
    _gather_rows(idx_ref, tbl_ref, 0, planes_a)

    def pair(i, carry):
        t0 = 2 * i
        place_pair(t0)
        _gather_rows(idx_ref, tbl_ref, t0 + 1, planes_b)
        compute(t0, planes_a, 0)
        _gather_rows(idx_ref, tbl_ref, jnp.minimum(t0 + 2, tb - 1), planes_a)
        compute(t0 + 1, planes_b, 1)
        return carry

    lax.fori_loop(0, tb // 2, pair, 0)
    place_pair(tb)
    o_ref[...] = _gelu(o_ref[...]) * gate_ref[...]


def peer_down(idx, xn3, gate_t, tbl, *, tb=128):
    T = xn3.shape[0]
    tb = min(tb, T)
    assert tb % 2 == 0 and T % tb == 0
    planes = pltpu.VMEM((ROW_SUB * PLANE_STRIDE, LANES), jnp.uint32)
    return pl.pallas_call(
        _peer_down_kernel,
        grid=(T // tb,),
        in_specs=[pl.BlockSpec((tb, N_SLOTS), lambda i: (i, 0), memory_space=pltpu.SMEM),
                  pl.BlockSpec((tb, SUBLANES, LANES), lambda i: (i, 0, 0)),
                  pl.BlockSpec((N_SLOTS, tb), lambda i: (0, i)),
                  pl.BlockSpec(tbl.shape, lambda i: (0, 0), pipeline_mode=pl.Buffered(1))],
        out_specs=pl.BlockSpec((N_SLOTS, tb), lambda i: (0, i)),
        out_shape=jax.ShapeDtypeStruct((N_SLOTS, T), jnp.float32),
        scratch_shapes=[planes, planes, pltpu.VMEM((2, N_SLOTS, LANES), jnp.float32)],
        compiler_params=_cparams(("arbitrary",)),
        name="peer_down",
    )(idx, xn3, gate_t, tbl)


SC_CORES = 2
SC_SUBCORES = 16
SC_WORKERS = SC_CORES * SC_SUBCORES
SC_LANES = 16
SC_GATHER_ROWS = N_SLOTS // 2
SC_ROW_BLOCK = 8
SC_TOKEN_BLOCK = 32
SC_COL_GROUP = 8


def _sc_params():
    cp = pltpu.CompilerParams()
    if "needs_layout_passes" in pltpu.CompilerParams.__dataclass_fields__:
        cp = dataclasses.replace(cp, needs_layout_passes=False)
    return cp


def peer_up_sc(ids, w, tbl):
    T = ids.shape[0]
    L, CH, RB, TBK = SC_LANES, SC_GATHER_ROWS, SC_ROW_BLOCK, SC_TOKEN_BLOCK
    assert T % (SC_WORKERS * TBK) == 0
    tpw = T // SC_WORKERS
    mesh = plsc.VectorSubcoreMesh(core_axis_name="c", subcore_axis_name="s")
    rows_buf = pltpu.VMEM((CH, ROW_WORDS), jnp.uint32)
    y_buf = pltpu.VMEM((D_MODEL,), jnp.float32)

    @functools.partial(
        pl.kernel, mesh=mesh, compiler_params=_sc_params(),
        out_type=jax.ShapeDtypeStruct((T, D_MODEL), jnp.float32),
        scratch_types=[pltpu.VMEM((TBK, N_SLOTS), jnp.int32), pltpu.VMEM((TBK, N_SLOTS), jnp.float32),
                       rows_buf, rows_buf, y_buf, y_buf] + [pltpu.SemaphoreType.DMA] * 4,
    )
    def up_kernel(ids_hbm, w_hbm, tbl_hbm, out_hbm, ids_v, w_v, rows0, rows1, y0, y1, g0, g1, o0, o1):
        base = (lax.axis_index("s") * SC_CORES + lax.axis_index("c")) * tpw
        rows, gsem, ys, osem = (rows0, rows1), (g0, g1), (y0, y1), (o0, o1)

        def gather(tl, half):
            return pltpu.make_async_copy(tbl_hbm.at[ids_v.at[tl, pl.ds(half * CH, CH)]], rows[half], gsem[half])

        def out_copy(tok, slot):
            return pltpu.make_async_copy(ys[slot], out_hbm.at[tok], osem[slot])

        def accumulate(tl, half, y_v):
            rows_v = rows[half]

            def block(b, carry):
                r0 = b * RB
                tlv = jnp.full((L,), tl, jnp.int32)
                wks = [plsc.load_gather(w_v, [tlv, jnp.full((L,), half * CH + r0 + r, jnp.int32)])
                       for r in range(RB)]
                for q in range(ROW_WORDS // (SC_COL_GROUP * L)):
                    acc = [None] * (2 * SC_COL_GROUP)
                    for r in range(RB):
                        for i in range(SC_COL_GROUP):
                            v = rows_v[r0 + r, pl.ds((q * SC_COL_GROUP + i) * L, L)]
                            lo = plsc.bitcast(v << 16, jnp.float32) * wks[r]
                            hi = plsc.bitcast(v & jnp.uint32(0xFFFF0000), jnp.float32) * wks[r]
                            acc[2 * i] = lo if r == 0 else acc[2 * i] + lo
                            acc[2 * i + 1] = hi if r == 0 else acc[2 * i + 1] + hi
                    for i in range(SC_COL_GROUP):
                        col = (q * SC_COL_GROUP + i) * L
                        plsc.addupdate(y_v.at[pl.ds(col, L)], acc[2 * i])
                        plsc.addupdate(y_v.at[pl.ds(ROW_WORDS + col, L)], acc[2 * i + 1])
                return carry

            lax.fori_loop(0, CH // RB, block, 0)

        def token_block(bi, carry):
            tok0 = base + bi * TBK
            pltpu.sync_copy(ids_hbm.at[pl.ds(tok0, TBK)], ids_v)
            pltpu.sync_copy(w_hbm.at[pl.ds(tok0, TBK)], w_v)
            gather(0, 0).start()

            def token_pair(pi, carry2):
                for slot in range(2):
                    tl = 2 * pi + slot
                    y_v = ys[slot]

                    @pl.when(pi > 0)
                    def _():
                        out_copy(tok0 + tl - 2, slot).wait()

                    zero = jnp.zeros((L,), jnp.float32)
                    for j in range(D_MODEL // L):
                        y_v[pl.ds(j * L, L)] = zero
                    gather(tl, 1).start()
                    gather(tl, 0).wait()
                    accumulate(tl, 0, y_v)

                    @pl.when(tl + 1 < TBK)
                    def _():
                        gather(tl + 1, 0).start()

                    gather(tl, 1).wait()
                    accumulate(tl, 1, y_v)
                    out_copy(tok0 + tl, slot).start()
                return carry2

            lax.fori_loop(0, TBK // 2, token_pair, 0)
            out_copy(tok0 + TBK - 2, 0).wait()
            out_copy(tok0 + TBK - 1, 1).wait()
            return carry

        lax.fori_loop(0, tpw // TBK, token_block, 0)

    return up_kernel(ids, w, tbl)


def peer_down_sc(ids, x, gate, tbl):
    T = ids.shape[0]
    L, CH, RB, CG = SC_LANES, SC_GATHER_ROWS, SC_ROW_BLOCK, SC_COL_GROUP
    tpw = T // SC_WORKERS
    TBK = min(SC_TOKEN_BLOCK, tpw)
    assert T % SC_WORKERS == 0 and tpw % TBK == 0 and TBK % 2 == 0
    mesh = plsc.VectorSubcoreMesh(core_axis_name="c", subcore_axis_name="s")
    rows_buf = pltpu.VMEM((CH, ROW_WORDS), jnp.uint32)
    x_buf = pltpu.VMEM((D_MODEL,), jnp.float32)
    w_buf = pltpu.VMEM((N_SLOTS,), jnp.float32)

    @functools.partial(
        pl.kernel, mesh=mesh, compiler_params=_sc_params(),
        out_type=jax.ShapeDtypeStruct((T, N_SLOTS), jnp.float32),
        scratch_types=[pltpu.VMEM((TBK, N_SLOTS), jnp.int32), pltpu.VMEM((TBK, N_SLOTS), jnp.float32),
                       rows_buf, rows_buf, x_buf, x_buf, pltpu.VMEM((N_SLOTS, L), jnp.float32), w_buf, w_buf]
        + [pltpu.SemaphoreType.DMA] * 6,
    )
    def down_kernel(ids_hbm, x_hbm, gate_hbm, tbl_hbm, out_hbm, ids_v, gate_v, rows0, rows1, x0, x1, part_v,
                    w0, w1, g0, g1, xs0, xs1, o0, o1):
        base = (lax.axis_index("s") * SC_CORES + lax.axis_index("c")) * tpw
        rows, gsem = (rows0, rows1), (g0, g1)
        xs, xsem = (x0, x1), (xs0, xs1)
        ws, osem = (w0, w1), (o0, o1)

        def gather(tl, half):
            return pltpu.make_async_copy(tbl_hbm.at[ids_v.at[tl, pl.ds(half * CH, CH)]], rows[half], gsem[half])

        def x_copy(tok, slot):
            return pltpu.make_async_copy(x_hbm.at[tok], xs[slot], xsem[slot])

        def out_copy(tok, slot):
            return pltpu.make_async_copy(ws[slot], out_hbm.at[tok], osem[slot])

        def dots(half, x_v):
            rows_v = rows[half]

            def block(b, carry):
                r0 = b * RB
                acc = [None] * RB
                for q in range(ROW_WORDS // (CG * L)):
                    xlo = [x_v[pl.ds((q * CG + i) * L, L)] for i in range(CG)]
                    xhi = [x_v[pl.ds(ROW_WORDS + (q * CG + i) * L, L)] for i in range(CG)]
                    for r in range(RB):
                        for i in range(CG):
                            v = rows_v[r0 + r, pl.ds((q * CG + i) * L, L)]
                            p = (plsc.bitcast(v << 16, jnp.float32) * xlo[i]
                                 + plsc.bitcast(v & jnp.uint32(0xFFFF0000), jnp.float32) * xhi[i])
                            acc[r] = p if acc[r] is None else acc[r] + p
                for r in range(RB):
                    part_v[half * CH + r0 + r, pl.ds(0, L)] = acc[r]
                return carry

            lax.fori_loop(0, CH // RB, block, 0)

        def finish(tl, w_v):
            lanes = lax.iota(jnp.int32, L)
            for grp in range(N_SLOTS // L):
                slots = lanes + grp * L
                a = plsc.load_gather(part_v, [slots, jnp.zeros((L,), jnp.int32)])
                for j in range(1, L):
                    a = a + plsc.load_gather(part_v, [slots, jnp.full((L,), j, jnp.int32)])
                z = _GELU_C * (a + 0.044715 * (a * a * a))
                th = 1.0 - 2.0 / (jnp.exp(2.0 * z) + 1.0)
                w_v[pl.ds(grp * L, L)] = 0.5 * a * (1.0 + th) * gate_v[tl, pl.ds(grp * L, L)]

        def token_block(bi, carry):
            tok0 = base + bi * TBK
            pltpu.sync_copy(ids_hbm.at[pl.ds(tok0, TBK)], ids_v)
            pltpu.sync_copy(gate_hbm.at[pl.ds(tok0, TBK)], gate_v)
            gather(0, 0).start()
            x_copy(tok0, 0).start()

            def token_pair(pi, carry2):
                for slot in range(2):
                    tl = 2 * pi + slot

                    @pl.when(pi > 0)
                    def _():
                        out_copy(tok0 + tl - 2, slot).wait()

                    gather(tl, 1).start()

                    @pl.when(tl + 1 < TBK)
                    def _():
                        x_copy(tok0 + tl + 1, 1 - slot).start()

                    x_copy(tok0 + tl, slot).wait()
                    gather(tl, 0).wait()
                    dots(0, xs[slot])

                    @pl.when(tl + 1 < TBK)
                    def _():
                        gather(tl + 1, 0).start()

                    gather(tl, 1).wait()
                    dots(1, xs[slot])
                    finish(tl, ws[slot])
                    out_copy(tok0 + tl, slot).start()
                return carry2

            lax.fori_loop(0, TBK // 2, token_pair, 0)
            out_copy(tok0 + TBK - 2, 0).wait()
            out_copy(tok0 + TBK - 1, 1).wait()
            return carry

        lax.fori_loop(0, tpw // TBK, token_block, 0)

    return down_kernel(ids, x, gate, tbl)


SC_DOWN_SHARE = 8


def peer_layer(h, g, w_query, sub_keys, down_wide, down_rows, up_wide):
    T = h.shape[0]
    q, xn = norm_matmul(h, g, w_query, out_dtype=jnp.float32, emit_xn=True)
    idx_t, gate_t = peer_route(q, sub_keys, precision=lax.Precision.HIGHEST)
    idx = idx_t.T
    ids = idx // ROW_SUB
    t_sc = T // SC_DOWN_SHARE
    t_tc = T - t_sc
    w_t = peer_down(idx[:t_tc], xn[:t_tc].reshape(t_tc, SUBLANES, LANES), gate_t[:, :t_tc], down_rows)
    w_sc = peer_down_sc(ids[t_tc:], xn[t_tc:], gate_t[:, t_tc:].T, down_wide)
    return peer_up_sc(ids, jnp.concatenate([w_t.T, w_sc], axis=0), up_wide)


def _rms_kernel(x_ref, r_ref, g_ref, o_ref):
    x = x_ref[...] + r_ref[...]
    o_ref[...] = x * lax.rsqrt(jnp.mean(x * x, axis=-1, keepdims=True) + NORM_EPS) * g_ref[...]


def rms_norm_sum(x, r, g, *, tm=512):
    T, D = x.shape
    tm = min(tm, T)
    row = pl.BlockSpec((tm, D), lambda i: (i, 0))
    return pl.pallas_call(
        _rms_kernel,
        grid=(T // tm,),
        in_specs=[row, row, pl.BlockSpec((1, D), lambda i: (0, 0))],
        out_specs=row,
        out_shape=jax.ShapeDtypeStruct((T, D), jnp.float32),
        compiler_params=_cparams(("parallel",)),
        name="final_norm",
    )(x, r, g.reshape(1, D))


PIECE_SEQS = 2


def _piece_bounds(B):
    sizes = [PIECE_SEQS] * (B // PIECE_SEQS) + ([B % PIECE_SEQS] if B % PIECE_SEQS else [])
    if sizes[-1] > 1:
        sizes[-1:] = [sizes[-1] - sizes[-1] // 2, sizes[-1] // 2]
    starts = [sum(sizes[:i]) for i in range(len(sizes))]
    return list(zip(starts, sizes))


def kernel(x, rel_table, mix_norm_g, attn_w_in, attn_w_out, attn_sink, sg_w_in, sg_ln_g, sg_ln_b, sg_w_spatial, sg_b_spatial, sg_w_out, ffn_norm_g, peer_w_query, peer_sub_keys, peer_down, peer_up, final_norm_g):
    B, S, D = x.shape
    bf16 = jnp.bfloat16

    bias = attn_bias(rel_table)
    attn_in, attn_out = attn_w_in[0].astype(bf16), attn_w_out[0].astype(bf16)
    sg_in, sg_out = sg_w_in[0].astype(bf16), sg_w_out[0].astype(bf16)
    w_query = peer_w_query.astype(bf16)
    down = [pack_table(peer_down, i, wide=True, split=True) for i in range(2)]
    up = [pack_table(peer_up, i, wide=True, split=False)[0] for i in range(2)]

    mid = []
    for b0, nb in _piece_bounds(B):
        h = x[b0:b0 + nb].reshape(nb * S, D)
        qkv, = norm_matmul(h, mix_norm_g[0], attn_in, out_dtype=bf16)
        att = window_attention(qkv.reshape(nb, S, QKV_DIM), bias, attn_sink[0])
        h = matmul_residual(att.reshape(nb * S, Q_DIM), attn_out, h)
        mid.append((h, peer_layer(h, ffn_norm_g[0], w_query[0], peer_sub_keys[0], *down[0], up[0])))

    outs = []
    for h, y in mid:
        z, h = norm_matmul(h, mix_norm_g[1], sg_in, res=y, out_dtype=bf16, act=True, tn=1536)
        gated = spatial_gate(z, sg_ln_g[0], sg_ln_b[0], sg_w_spatial[0], sg_b_spatial[0])
        h = matmul_residual(gated, sg_out, h)
        y = peer_layer(h, ffn_norm_g[1], w_query[1], peer_sub_keys[1], *down[1], up[1])
        outs.append(rms_norm_sum(h, y, final_norm_g).reshape(-1, S, D))
    return jnp.concatenate(outs, axis=0)
```

```python
import dataclasses
import functools
import math

import jax
import jax.numpy as jnp
from jax import lax
from jax.experimental import pallas as pl
from jax.experimental.pallas import tpu as pltpu
from jax.experimental.pallas import tpu_sc as plsc

D_MODEL = 1024
HEAD_DIM = 64
N_Q_HEADS = 16
N_KV_HEADS = 4
GQA_GROUP = 4
WINDOW = 128
BLOCK = 128
REL_BUCKETS = 32
REL_MAX_DIST = 128
Q_DIM = N_Q_HEADS * HEAD_DIM
KV_DIM = N_KV_HEADS * HEAD_DIM
QKV_DIM = Q_DIM + 2 * KV_DIM
CHUNK = 128
D_GATE = 3072
N_SG_GROUPS = 8
SG_GROUP_DIM = D_GATE // N_SG_GROUPS
N_KEYS = 128
PEER_HEADS = 8
PEER_TOPK = 16
D_HALF = 128
N_SLOTS = PEER_HEADS * PEER_TOPK
NORM_EPS = 1e-6
LN_EPS = 1e-5

LANES = 128
SUBLANES = 8
ROW_WORDS = D_MODEL // 2
ROW_SUB = ROW_WORDS // LANES
PLANE_STRIDE = 136
VMEM_LIMIT = 56 * 1024 * 1024

_GELU_C = math.sqrt(2.0 / math.pi)


def _gelu(x):
    return 0.5 * x * (1.0 + jnp.tanh(_GELU_C * (x + 0.044715 * (x * x * x))))


def _cparams(sem):
    return pltpu.CompilerParams(dimension_semantics=sem, vmem_limit_bytes=VMEM_LIMIT)


def _norm_matmul_kernel(*refs, act, emit_xn, add_res, precision):
    refs = list(refs)
    x_ref = refs.pop(0)
    r_ref = refs.pop(0) if add_res else None
    g_ref, w_ref, o_ref = refs.pop(0), refs.pop(0), refs.pop(0)
    xn_ref = refs.pop(0) if emit_xn else None
    sum_ref = refs.pop(0) if add_res else None
    xs_ref = refs.pop(0)

    @pl.when(pl.program_id(1) == 0)
    def _():
        x = x_ref[...]
        if add_res:
            x = x + r_ref[...]
            sum_ref[...] = x
        y = x * lax.rsqrt(jnp.mean(x * x, axis=-1, keepdims=True) + NORM_EPS) * g_ref[...]
        xs_ref[...] = y.astype(xs_ref.dtype)
        if emit_xn:
            xn_ref[...] = y

    acc = jnp.dot(xs_ref[...], w_ref[...], preferred_element_type=jnp.float32,
                  precision=precision)
    if act:
        acc = _gelu(acc)
    o_ref[...] = acc.astype(o_ref.dtype)


def norm_matmul(x, g, w, *, out_dtype, res=None, act=False, emit_xn=False, tm=512, tn=None,
                precision=None):
    T, D = x.shape
    N = w.shape[1]
    tn = tn or N
    tm = min(tm, T)
    row = pl.BlockSpec((tm, D), lambda i, j: (i, 0))
    out_shape = [jax.ShapeDtypeStruct((T, N), out_dtype)]
    out_specs = [pl.BlockSpec((tm, tn), lambda i, j: (i, j))]
    for flag in (emit_xn, res is not None):
        if flag:
            out_shape.append(jax.ShapeDtypeStruct((T, D), jnp.float32))
            out_specs.append(row)
    args = [x] + ([res] if res is not None else []) + [g.reshape(1, D), w]
    in_specs = [row] * (len(args) - 2) + [pl.BlockSpec((1, D), lambda i, j: (0, 0)),
                                          pl.BlockSpec((D, tn), lambda i, j: (0, j))]
    return pl.pallas_call(
        functools.partial(_norm_matmul_kernel, act=act, emit_xn=emit_xn, add_res=res is not None,
                          precision=precision),
        grid=(T // tm, N // tn),
        in_specs=in_specs,
        out_specs=out_specs,
        out_shape=out_shape,
        scratch_shapes=[pltpu.VMEM((tm, D), w.dtype)],
        compiler_params=_cparams(("parallel", "arbitrary")),
        name="norm_matmul",
    )(*args)


def _matmul_res_kernel(a_ref, w_ref, h_ref, o_ref):
    o_ref[...] = h_ref[...] + jnp.dot(a_ref[...], w_ref[...],
                                      preferred_element_type=jnp.float32)


def matmul_residual(a, w, h, *, tm=512):
    T, K = a.shape
    N = w.shape[1]
    tm = min(tm, T)
    return pl.pallas_call(
        _matmul_res_kernel,
        grid=(T // tm,),
        in_specs=[pl.BlockSpec((tm, K), lambda i: (i, 0)),
                  pl.BlockSpec((K, N), lambda i: (0, 0)),
                  pl.BlockSpec((tm, N), lambda i: (i, 0))],
        out_specs=pl.BlockSpec((tm, N), lambda i: (i, 0)),
        out_shape=jax.ShapeDtypeStruct((T, N), jnp.float32),
        compiler_params=_cparams(("parallel",)),
        name="matmul_residual",
    )(a, w, h)


def _t5_bucket(rel):
    nb = REL_BUCKETS // 2
    max_exact = nb // 2
    ret = jnp.where(rel > 0, nb, 0)
    n = jnp.abs(rel)
    nf = jnp.maximum(n, 1).astype(jnp.float32)
    large = max_exact + (jnp.log(nf / max_exact) / math.log(REL_MAX_DIST / max_exact)
                         * (nb - max_exact)).astype(jnp.int32)
    large = jnp.minimum(large, nb - 1)
    return (ret + jnp.where(n < max_exact, n, large)).astype(jnp.int32)


def _bias_kernel(bucket_ref, window_ref, table_ref, o_ref):
    bucket = bucket_ref[...]
    in_window = window_ref[...] > 0
    for hq in range(N_Q_HEADS):
        acc = jnp.zeros(bucket.shape, jnp.float32)
        for b in range(REL_BUCKETS):
            acc = jnp.where(bucket == b, table_ref[b, hq], acc)
        o_ref[hq] = jnp.where(in_window, acc, -jnp.inf)


def attn_bias(rel_table):
    qi = jnp.arange(BLOCK)[:, None]
    kj = jnp.arange(3 * BLOCK)[None, :]
    rel = kj - BLOCK - qi
    bucket = _t5_bucket(rel)
    window = (jnp.abs(rel) <= WINDOW).astype(jnp.int32)
    return pl.pallas_call(
        _bias_kernel,
        in_specs=[pl.BlockSpec(memory_space=pltpu.VMEM),
                  pl.BlockSpec(memory_space=pltpu.VMEM),
                  pl.BlockSpec(memory_space=pltpu.SMEM)],
        out_specs=pl.BlockSpec(memory_space=pltpu.VMEM),
        out_shape=jax.ShapeDtypeStruct((N_Q_HEADS, BLOCK, 3 * BLOCK), jnp.float32),
        name="attn_bias",
    )(bucket, window, rel_table)


def _attn_kernel(cur_ref, prev_ref, next_ref, bias_ref, sink_ref, o_ref):
    i = pl.program_id(1)
    nb = pl.num_programs(1)
    q = cur_ref[0, :, 0:Q_DIM]
    kband = jnp.concatenate([prev_ref[0, :, 0:KV_DIM], cur_ref[0, :, Q_DIM:Q_DIM + KV_DIM],
                             next_ref[0, :, 0:KV_DIM]], axis=0)
    vband = jnp.concatenate([prev_ref[0, :, KV_DIM:2 * KV_DIM], cur_ref[0, :, Q_DIM + KV_DIM:QKV_DIM],
                             next_ref[0, :, KV_DIM:2 * KV_DIM]], axis=0)
    col = lax.broadcasted_iota(jnp.int32, (1, 3 * BLOCK), 1)
    valid = jnp.logical_and(jnp.logical_or(col >= BLOCK, i > 0),
                            jnp.logical_or(col < 2 * BLOCK, i < nb - 1))
    heads = range(N_Q_HEADS)
    kv = [(kband[:, hk * HEAD_DIM:(hk + 1) * HEAD_DIM], vband[:, hk * HEAD_DIM:(hk + 1) * HEAD_DIM])
          for hk in range(N_KV_HEADS)]
    s = []
    for hq in heads:
        qh = q[:, hq * HEAD_DIM:(hq + 1) * HEAD_DIM]
        sc = lax.dot_general(qh, kv[hq // GQA_GROUP][0], (((1,), (1,)), ((), ())),
                             preferred_element_type=jnp.float32)
        s.append(jnp.where(valid, sc * (HEAD_DIM ** -0.5) + bias_ref[hq], -jnp.inf))
    mx = [jnp.maximum(jnp.max(s[hq], axis=-1, keepdims=True), sink_ref[hq]) for hq in heads]
    p = [jnp.exp(s[hq] - mx[hq]) for hq in heads]
    denom = [jnp.sum(p[hq], axis=-1, keepdims=True) + jnp.exp(sink_ref[hq] - mx[hq]) for hq in heads]
    outs = [jnp.dot((p[hq] / denom[hq]).astype(jnp.bfloat16), kv[hq // GQA_GROUP][1],
                    preferred_element_type=jnp.float32) for hq in heads]
    o_ref[0] = jnp.concatenate(outs, axis=-1).astype(o_ref.dtype)


def window_attention(qkv, bias, sink):
    B, S, _ = qkv.shape
    nb = S // BLOCK
    kv_col = Q_DIM // (2 * KV_DIM)
    return pl.pallas_call(
        _attn_kernel,
        grid=(B, nb),
        in_specs=[pl.BlockSpec((1, BLOCK, QKV_DIM), lambda b, i: (b, i, 0)),
                  pl.BlockSpec((1, BLOCK, 2 * KV_DIM),
                               lambda b, i: (b, jnp.maximum(i - 1, 0), kv_col)),
                  pl.BlockSpec((1, BLOCK, 2 * KV_DIM),
                               lambda b, i: (b, jnp.minimum(i + 1, nb - 1), kv_col)),
                  pl.BlockSpec((N_Q_HEADS, BLOCK, 3 * BLOCK), lambda b, i: (0, 0, 0)),
                  pl.BlockSpec(memory_space=pltpu.SMEM)],
        out_specs=pl.BlockSpec((1, BLOCK, Q_DIM), lambda b, i: (b, i, 0)),
        out_shape=jax.ShapeDtypeStruct((B, S, Q_DIM), jnp.bfloat16),
        compiler_params=_cparams(("parallel", "arbitrary")),
        name="window_attention",
    )(qkv, qkv, qkv, bias, sink)


def _spatial_gate_kernel(z_ref, g_ref, b_ref, wsp_ref, bsp_ref, o_ref):
    v = z_ref[:, D_GATE:2 * D_GATE].astype(jnp.float32)
    mu = jnp.mean(v, axis=-1, keepdims=True)
    vc = v - mu
    var = jnp.mean(vc * vc, axis=-1, keepdims=True)
    vn = (vc * lax.rsqrt(var + LN_EPS) * g_ref[...] + b_ref[...]).astype(jnp.bfloat16)
    for grp in range(N_SG_GROUPS):
        lo, hi = grp * SG_GROUP_DIM, (grp + 1) * SG_GROUP_DIM
        mixed = jnp.dot(wsp_ref[grp], vn[:, lo:hi], preferred_element_type=jnp.float32)
        mixed = mixed + bsp_ref[:, grp:grp + 1]
        u = z_ref[:, lo:hi].astype(jnp.float32)
        o_ref[:, lo:hi] = (u * mixed).astype(o_ref.dtype)


def spatial_gate(z, ln_g, ln_b, w_sp, b_sp):
    T = z.shape[0]
    return pl.pallas_call(
        _spatial_gate_kernel,
        grid=(T // CHUNK,),
        in_specs=[pl.BlockSpec((CHUNK, 2 * D_GATE), lambda i: (i, 0)),
                  pl.BlockSpec((1, D_GATE), lambda i: (0, 0)),
                  pl.BlockSpec((1, D_GATE), lambda i: (0, 0)),
                  pl.BlockSpec((N_SG_GROUPS, CHUNK, CHUNK), lambda i: (0, 0, 0)),
                  pl.BlockSpec((CHUNK, N_SG_GROUPS), lambda i: (0, 0))],
        out_specs=pl.BlockSpec((CHUNK, D_GATE), lambda i: (i, 0)),
        out_shape=jax.ShapeDtypeStruct((T, D_GATE), jnp.bfloat16),
        compiler_params=_cparams(("parallel",)),
        name="spatial_gate",
    )(z, ln_g.reshape(1, D_GATE), ln_b.reshape(1, D_GATE), w_sp.astype(jnp.bfloat16), b_sp.T)


def _oddeven_merge_sort_pairs(n):
    pairs = []
    p = 1
    while p < n:
        k = p
        while k >= 1:
            for j in range(k % p, n - k, 2 * k):
                for i in range(min(k, n - j - k)):
                    if (i + j) // (2 * p) == (i + j + k) // (2 * p):
                        pairs.append((i + j, i + j + k))
            k //= 2
        p *= 2
    return pairs


_SORT16 = _oddeven_merge_sort_pairs(N_KEYS // SUBLANES)


def _top16_of_keys(s):
    nv = N_KEYS // SUBLANES
    L = s.shape[1]
    sub = lax.broadcasted_iota(jnp.int32, (SUBLANES, L), 0)
    v = [s[j * SUBLANES:(j + 1) * SUBLANES] for j in range(nv)]
    ids = [sub + j * SUBLANES for j in range(nv)]
    for i, j in _SORT16:
        swap = jnp.logical_or(v[j] > v[i], jnp.logical_and(v[j] == v[i], ids[j] < ids[i]))
        v[i], v[j] = jnp.where(swap, v[j], v[i]), jnp.where(swap, v[i], v[j])
        ids[i], ids[j] = jnp.where(swap, ids[j], ids[i]), jnp.where(swap, ids[i], ids[j])
    vals, picks = [], []
    for it in range(PEER_TOPK):
        m = jnp.max(v[0], axis=0, keepdims=True)
        am = jnp.min(jnp.where(v[0] == m, ids[0], N_KEYS), axis=0, keepdims=True)
        hit = ids[0] == am
        vals.append(m)
        picks.append(am)
        last = PEER_TOPK - 1 - it
        for j in range(last):
            v[j] = jnp.where(hit, v[j + 1], v[j])
            ids[j] = jnp.where(hit, ids[j + 1], ids[j])
        v[last] = jnp.where(hit, -jnp.inf, v[last])
    return jnp.concatenate(vals, axis=0), jnp.concatenate(picks, axis=0)


_CAND_ROWS = (
    [(a, 0) for a in range(16)]
    + [None] + [(0, b) for b in range(1, 16)]
    + [None] + [(a, 1) for a in range(1, 8)]
    + [None, None] + [(1, b) for b in range(2, 8)]
    + [(2, 2), (3, 2), (4, 2), (2, 3), (2, 4), (3, 3), None, None]
)
assert sorted(x for x in _CAND_ROWS if x) == sorted(
    (a, b) for a in range(16) for b in range(16) if (a + 1) * (b + 1) <= 16)


def _rows(x, picks):
    pieces, i = [], 0
    while i < len(picks):
        j = i
        while j + 1 < len(picks) and picks[j + 1] == picks[j] + 1:
            j += 1
        pieces.append(x[picks[i]:picks[j] + 1])
        i = j + 1
    return pieces[0] if len(pieces) == 1 else jnp.concatenate(pieces, axis=0)


def _cand_positions(L):
    row = lax.broadcasted_iota(jnp.int32, (len(_CAND_ROWS), L), 0)
    pos = jnp.full(row.shape, _PAD_POS, jnp.int32)
    for r, c in enumerate(_CAND_ROWS):
        if c is not None:
            pos = jnp.where(row == r, c[0] * PEER_TOPK + c[1], pos)
    return pos


_PAD_POS = 1 << 20


def _joint_top16(v1, i1, v2, i2, pos):
    a_of = [c[0] if c else 0 for c in _CAND_ROWS]
    b_of = [c[1] if c else 0 for c in _CAND_ROWS]
    cand = jnp.where(pos < _PAD_POS, _rows(v1, a_of) + _rows(v2, b_of), -jnp.inf)
    cidx = _rows(i1, a_of) * N_KEYS + _rows(i2, b_of)
    vals, picks = [], []
    for _ in range(PEER_TOPK):
        m = jnp.max(cand, axis=0, keepdims=True)
        pm = jnp.min(jnp.where(cand == m, pos, 1 << 21), axis=0, keepdims=True)
        hit = pos == pm
        vals.append(m)
        picks.append(jnp.max(jnp.where(hit, cidx, -1), axis=0, keepdims=True))
        cand = jnp.where(hit, -jnp.inf, cand)
    return jnp.concatenate(vals, axis=0), jnp.concatenate(picks, axis=0)


def _route_kernel(q_ref, keys_ref, idx_ref, gate_ref, *, precision):
    nt = (((1,), (1,)), ((), ()))
    pos = _cand_positions(LANES)
    for j in range(q_ref.shape[0] // LANES):
        q = q_ref[j * LANES:(j + 1) * LANES, :]
        s1 = lax.dot_general(keys_ref[0, 0], q[:, 0:D_HALF], nt,
                             preferred_element_type=jnp.float32, precision=precision)
        s2 = lax.dot_general(keys_ref[0, 1], q[:, D_HALF:2 * D_HALF], nt,
                             preferred_element_type=jnp.float32, precision=precision)
        v1, i1 = _top16_of_keys(s1)
        v2, i2 = _top16_of_keys(s2)
        top_s, top_i = _joint_top16(v1, i1, v2, i2, pos)
        e = jnp.exp(top_s - top_s[0:1])
        gate_ref[:, j * LANES:(j + 1) * LANES] = e / jnp.sum(e, axis=0, keepdims=True)
        idx_ref[:, j * LANES:(j + 1) * LANES] = top_i * ROW_SUB


def peer_route(q, sub_keys, *, tb=512, precision=None):
    T = q.shape[0]
    tb = min(tb, T)
    return pl.pallas_call(
        functools.partial(_route_kernel, precision=precision),
        grid=(T // tb, PEER_HEADS),
        in_specs=[pl.BlockSpec((tb, 2 * D_HALF), lambda i, h: (i, h)),
                  pl.BlockSpec((1, 2, N_KEYS, D_HALF), lambda i, h: (h, 0, 0, 0))],
        out_specs=[pl.BlockSpec((PEER_TOPK, tb), lambda i, h: (h, i)),
                   pl.BlockSpec((PEER_TOPK, tb), lambda i, h: (h, i))],
        out_shape=[jax.ShapeDtypeStruct((N_SLOTS, T), jnp.int32),
                   jax.ShapeDtypeStruct((N_SLOTS, T), jnp.float32)],
        compiler_params=_cparams(("parallel", "arbitrary")),
        name="peer_route",
    )(q, sub_keys)


def _pack_kernel(x_ref, *o_refs, wide, split):
    x = x_ref[...]
    half = x.shape[1] // 2

    def bf16_bits(v):
        return pltpu.bitcast(v.astype(jnp.bfloat16).astype(jnp.float32), jnp.uint32)

    words = (bf16_bits(x[:, :half]) >> 16) | (bf16_bits(x[:, half:]) & jnp.uint32(0xFFFF0000))
    o_refs = list(o_refs)
    if wide:
        o_refs.pop(0)[...] = words
    if split:
        o_ref, n = o_refs.pop(0), x.shape[0]
        for c in range(ROW_SUB):
            o_ref[pl.ds(c, n, stride=ROW_SUB), :] = words[:, c * LANES:(c + 1) * LANES]


def pack_table(tbls, layer, *, wide, split, te=512):
    _, E, D = tbls.shape
    specs, shapes = [], []
    if wide:
        specs.append(pl.BlockSpec((te, D // 2), lambda i: (i, 0)))
        shapes.append(jax.ShapeDtypeStruct((E, D // 2), jnp.uint32))
    if split:
        specs.append(pl.BlockSpec((te * ROW_SUB, LANES), lambda i: (i, 0)))
        shapes.append(jax.ShapeDtypeStruct((E * ROW_SUB, LANES), jnp.uint32))
    return pl.pallas_call(
        functools.partial(_pack_kernel, wide=wide, split=split),
        grid=(E // te,),
        in_specs=[pl.BlockSpec((None, te, D), lambda i: (layer, i, 0))],
        out_specs=specs,
        out_shape=shapes,
        compiler_params=_cparams(("parallel",)),
        name="pack_table",
    )(tbls)


def _unpack(words):
    lo = pltpu.bitcast(words << 16, jnp.float32)
    hi = pltpu.bitcast(words & jnp.uint32(0xFFFF0000), jnp.float32)
    return lo, hi


def _gather_rows(idx_ref, tbl_ref, t, planes_ref):
    for k in range(N_SLOTS):
        off = pl.multiple_of(idx_ref[t, k], ROW_SUB)
        planes_ref[pl.ds(k, ROW_SUB, stride=PLANE_STRIDE), :] = tbl_ref[pl.ds(off, ROW_SUB), :]


def _plane(planes_ref, c):
    return _unpack(planes_ref[c * PLANE_STRIDE:c * PLANE_STRIDE + N_SLOTS, :])


def _peer_down_kernel(idx_ref, x_ref, gate_ref, tbl_ref, o_ref, planes_a, planes_b, part_ref):
    tb = x_ref.shape[0]
    lane = lax.broadcasted_iota(jnp.int32, (N_SLOTS, tb), 1)

    def compute(t, planes_ref, slot):
        acc = jnp.zeros((N_SLOTS, LANES), jnp.float32)
        for c in range(ROW_SUB):
            lo, hi = _plane(planes_ref, c)
            acc = acc + lo * x_ref[t, c:c + 1, :] + hi * x_ref[t, ROW_SUB + c:ROW_SUB + c + 1, :]
        part_ref[slot] = acc

    def place_pair(t0):
        col_a = jnp.sum(part_ref[0], axis=1, keepdims=True)
        col_b = jnp.sum(part_ref[1], axis=1, keepdims=True)
        o_ref[...] = jnp.where(lane == t0 - 2, col_a, jnp.where(lane == t0 - 1, col_b, o_ref[...]))

    o_ref[...] = jnp.zeros(o_ref.shape, jnp.float32)
    part_ref[...] = jnp.zeros(part_ref.shape, jnp.float32)
    _gather_rows(idx_ref, tbl_ref, 0, planes_a)

    def pair(i, carry):
        t0 = 2 * i
        place_pair(t0)
        _gather_rows(idx_ref, tbl_ref, t0 + 1, planes_b)
        compute(t0, planes_a, 0)
        _gather_rows(idx_ref, tbl_ref, jnp.minimum(t0 + 2, tb - 1), planes_a)
        compute(t0 + 1, planes_b, 1)
        return carry

    lax.fori_loop(0, tb // 2, pair, 0)
    place_pair(tb)
    o_ref[...] = _gelu(o_ref[...]) * gate_ref[...]


def peer_down(idx, xn3, gate_t, tbl, n_tokens, *, tb=128):
    T = n_tokens
    tb = min(tb, T)
    assert tb % 2 == 0 and T % tb == 0
    planes = pltpu.VMEM((ROW_SUB * PLANE_STRIDE, LANES), jnp.uint32)
    return pl.pallas_call(
        _peer_down_kernel,
        grid=(T // tb,),
        in_specs=[pl.BlockSpec((tb, N_SLOTS), lambda i: (i, 0), memory_space=pltpu.SMEM),
                  pl.BlockSpec((tb, SUBLANES, LANES), lambda i: (i, 0, 0)),
                  pl.BlockSpec((N_SLOTS, tb), lambda i: (0, i)),
                  pl.BlockSpec(tbl.shape, lambda i: (0, 0), pipeline_mode=pl.Buffered(1))],
        out_specs=pl.BlockSpec((N_SLOTS, tb), lambda i: (0, i)),
        out_shape=jax.ShapeDtypeStruct((N_SLOTS, T), jnp.float32),
        scratch_shapes=[planes, planes, pltpu.VMEM((2, N_SLOTS, LANES), jnp.float32)],
        compiler_params=_cparams(("arbitrary",)),
        name="peer_down",
    )(idx, xn3, gate_t, tbl)


SC_CORES = 2
SC_SUBCORES = 16
SC_WORKERS = SC_CORES * SC_SUBCORES
SC_LANES = 16
SC_GATHER_ROWS = N_SLOTS // 2
SC_ROW_BLOCK = 8
SC_TOKEN_BLOCK = 32
SC_COL_GROUP = 8


def _sc_params():
    cp = pltpu.CompilerParams()
    if "needs_layout_passes" in pltpu.CompilerParams.__dataclass_fields__:
        cp = dataclasses.replace(cp, needs_layout_passes=False)
    return cp


def _sc_token_block(tokens_per_worker):
    return next(b for b in range(SC_TOKEN_BLOCK, 0, -SUBLANES) if tokens_per_worker % b == 0)


def peer_up_sc(ids, w, tbl, first):
    T = w.shape[0]
    assert T % SC_WORKERS == 0
    tpw = T // SC_WORKERS
    L, CH, RB, TBK = SC_LANES, SC_GATHER_ROWS, SC_ROW_BLOCK, _sc_token_block(tpw)
    mesh = plsc.VectorSubcoreMesh(core_axis_name="c", subcore_axis_name="s")
    rows_buf = pltpu.VMEM((CH, ROW_WORDS), jnp.uint32)
    y_buf = pltpu.VMEM((D_MODEL,), jnp.float32)

    @functools.partial(
        pl.kernel, mesh=mesh, compiler_params=_sc_params(),
        out_type=jax.ShapeDtypeStruct((T, D_MODEL), jnp.float32),
        scratch_types=[pltpu.VMEM((TBK, N_SLOTS), jnp.int32), pltpu.VMEM((TBK, N_SLOTS), jnp.float32),
                       rows_buf, rows_buf, y_buf, y_buf] + [pltpu.SemaphoreType.DMA] * 4,
    )
    def up_kernel(ids_hbm, w_hbm, tbl_hbm, out_hbm, ids_v, w_v, rows0, rows1, y0, y1, g0, g1, o0, o1):
        base = (lax.axis_index("s") * SC_CORES + lax.axis_index("c")) * tpw
        rows, gsem, ys, osem = (rows0, rows1), (g0, g1), (y0, y1), (o0, o1)

        def gather(tl, half):
            return pltpu.make_async_copy(tbl_hbm.at[ids_v.at[tl, pl.ds(half * CH, CH)]], rows[half], gsem[half])

        def out_copy(tok, slot):
            return pltpu.make_async_copy(ys[slot], out_hbm.at[tok], osem[slot])

        def accumulate(tl, half, y_v):
            rows_v = rows[half]

            def block(b, carry):
                r0 = b * RB
                tlv = jnp.full((L,), tl, jnp.int32)
                wks = [plsc.load_gather(w_v, [tlv, jnp.full((L,), half * CH + r0 + r, jnp.int32)])
                       for r in range(RB)]
                for q in range(ROW_WORDS // (SC_COL_GROUP * L)):
                    acc = [None] * (2 * SC_COL_GROUP)
                    for r in range(RB):
                        for i in range(SC_COL_GROUP):
                            v = rows_v[r0 + r, pl.ds((q * SC_COL_GROUP + i) * L, L)]
                            lo = plsc.bitcast(v << 16, jnp.float32) * wks[r]
                            hi = plsc.bitcast(v & jnp.uint32(0xFFFF0000), jnp.float32) * wks[r]
                            acc[2 * i] = lo if r == 0 else acc[2 * i] + lo
                            acc[2 * i + 1] = hi if r == 0 else acc[2 * i + 1] + hi
                    for i in range(SC_COL_GROUP):
                        col = (q * SC_COL_GROUP + i) * L
                        plsc.addupdate(y_v.at[pl.ds(col, L)], acc[2 * i])
                        plsc.addupdate(y_v.at[pl.ds(ROW_WORDS + col, L)], acc[2 * i + 1])
                return carry

            lax.fori_loop(0, CH // RB, block, 0)

        def token_block(bi, carry):
            tok0 = base + bi * TBK
            pltpu.sync_copy(ids_hbm.at[pl.ds(first + tok0, TBK)], ids_v)
            pltpu.sync_copy(w_hbm.at[pl.ds(tok0, TBK)], w_v)
            gather(0, 0).start()

            def token_pair(pi, carry2):
                for slot in range(2):
                    tl = 2 * pi + slot
                    y_v = ys[slot]

                    @pl.when(pi > 0)
                    def _():
                        out_copy(tok0 + tl - 2, slot).wait()

                    zero = jnp.zeros((L,), jnp.float32)
                    for j in range(D_MODEL // L):
                        y_v[pl.ds(j * L, L)] = zero
                    gather(tl, 1).start()
                    gather(tl, 0).wait()
                    accumulate(tl, 0, y_v)

                    @pl.when(tl + 1 < TBK)
                    def _():
                        gather(tl + 1, 0).start()

                    gather(tl, 1).wait()
                    accumulate(tl, 1, y_v)
                    out_copy(tok0 + tl, slot).start()
                return carry2

            lax.fori_loop(0, TBK // 2, token_pair, 0)
            out_copy(tok0 + TBK - 2, 0).wait()
            out_copy(tok0 + TBK - 1, 1).wait()
            return carry

        lax.fori_loop(0, tpw // TBK, token_block, 0)

    return up_kernel(ids, w, tbl)


def peer_down_sc(ids, x, gate, tbl, first):
    T = gate.shape[0]
    L, CH, RB, CG = SC_LANES, SC_GATHER_ROWS, SC_ROW_BLOCK, SC_COL_GROUP
    assert T % SC_WORKERS == 0
    tpw = T // SC_WORKERS
    TBK = _sc_token_block(tpw)
    mesh = plsc.VectorSubcoreMesh(core_axis_name="c", subcore_axis_name="s")
    rows_buf = pltpu.VMEM((CH, ROW_WORDS), jnp.uint32)
    x_buf = pltpu.VMEM((D_MODEL,), jnp.float32)
    w_buf = pltpu.VMEM((N_SLOTS,), jnp.float32)

    @functools.partial(
        pl.kernel, mesh=mesh, compiler_params=_sc_params(),
        out_type=jax.ShapeDtypeStruct((T, N_SLOTS), jnp.float32),
        scratch_types=[pltpu.VMEM((TBK, N_SLOTS), jnp.int32), pltpu.VMEM((TBK, N_SLOTS), jnp.float32),
                       rows_buf, rows_buf, x_buf, x_buf, pltpu.VMEM((N_SLOTS, L), jnp.float32), w_buf, w_buf]
        + [pltpu.SemaphoreType.DMA] * 6,
    )
    def down_kernel(ids_hbm, x_hbm, gate_hbm, tbl_hbm, out_hbm, ids_v, gate_v, rows0, rows1, x0, x1, part_v,
                    w0, w1, g0, g1, xs0, xs1, o0, o1):
        base = (lax.axis_index("s") * SC_CORES + lax.axis_index("c")) * tpw
        rows, gsem = (rows0, rows1), (g0, g1)
        xs, xsem = (x0, x1), (xs0, xs1)
        ws, osem = (w0, w1), (o0, o1)

        def gather(tl, half):
            return pltpu.make_async_copy(tbl_hbm.at[ids_v.at[tl, pl.ds(half * CH, CH)]], rows[half], gsem[half])

        def x_copy(tok, slot):
            return pltpu.make_async_copy(x_hbm.at[first + tok], xs[slot], xsem[slot])

        def out_copy(tok, slot):
            return pltpu.make_async_copy(ws[slot], out_hbm.at[tok], osem[slot])

        def dots(half, x_v):
            rows_v = rows[half]

            def block(b, carry):
                r0 = b * RB
                acc = [None] * RB
                for q in range(ROW_WORDS // (CG * L)):
                    xlo = [x_v[pl.ds((q * CG + i) * L, L)] for i in range(CG)]
                    xhi = [x_v[pl.ds(ROW_WORDS + (q * CG + i) * L, L)] for i in range(CG)]
                    for r in range(RB):
                        for i in range(CG):
                            v = rows_v[r0 + r, pl.ds((q * CG + i) * L, L)]
                            p = (plsc.bitcast(v << 16, jnp.float32) * xlo[i]
                                 + plsc.bitcast(v & jnp.uint32(0xFFFF0000), jnp.float32) * xhi[i])
                            acc[r] = p if acc[r] is None else acc[r] + p
                for r in range(RB):
                    part_v[half * CH + r0 + r, pl.ds(0, L)] = acc[r]
                return carry

            lax.fori_loop(0, CH // RB, block, 0)

        def finish(tl, w_v):
            lanes = lax.iota(jnp.int32, L)
            for grp in range(N_SLOTS // L):
                slots = lanes + grp * L
                a = plsc.load_gather(part_v, [slots, jnp.zeros((L,), jnp.int32)])
                for j in range(1, L):
                    a = a + plsc.load_gather(part_v, [slots, jnp.full((L,), j, jnp.int32)])
                z = _GELU_C * (a + 0.044715 * (a * a * a))
                th = 1.0 - 2.0 / (jnp.exp(2.0 * z) + 1.0)
                w_v[pl.ds(grp * L, L)] = 0.5 * a * (1.0 + th) * gate_v[tl, pl.ds(grp * L, L)]

        def token_block(bi, carry):
            tok0 = base + bi * TBK
            pltpu.sync_copy(ids_hbm.at[pl.ds(first + tok0, TBK)], ids_v)
            pltpu.sync_copy(gate_hbm.at[pl.ds(tok0, TBK)], gate_v)
            gather(0, 0).start()
            x_copy(tok0, 0).start()

            def token_pair(pi, carry2):
                for slot in range(2):
                    tl = 2 * pi + slot

                    @pl.when(pi > 0)
                    def _():
                        out_copy(tok0 + tl - 2, slot).wait()

                    gather(tl, 1).start()

                    @pl.when(tl + 1 < TBK)
                    def _():
                        x_copy(tok0 + tl + 1, 1 - slot).start()

                    x_copy(tok0 + tl, slot).wait()
                    gather(tl, 0).wait()
                    dots(0, xs[slot])

                    @pl.when(tl + 1 < TBK)
                    def _():
                        gather(tl + 1, 0).start()

                    gather(tl, 1).wait()
                    dots(1, xs[slot])
                    finish(tl, ws[slot])
                    out_copy(tok0 + tl, slot).start()
                return carry2

            lax.fori_loop(0, TBK // 2, token_pair, 0)
            out_copy(tok0 + TBK - 2, 0).wait()
            out_copy(tok0 + TBK - 1, 1).wait()
            return carry

        lax.fori_loop(0, tpw // TBK, token_block, 0)

    return down_kernel(ids, x, gate, tbl)


SC_DOWN_SHARE = 8


def peer_layer(h, g, w_query, sub_keys, down_wide, down_rows, up_wide):
    T = h.shape[0]
    q, xn = norm_matmul(h, g, w_query, out_dtype=jnp.float32, emit_xn=True)
    idx_t, gate_t = peer_route(q, sub_keys, precision=lax.Precision.HIGHEST)
    idx = idx_t.T
    ids = idx // ROW_SUB
    t_sc = T // SC_DOWN_SHARE
    t_tc = T - t_sc
    w_t = peer_down(idx, xn.reshape(T, SUBLANES, LANES), gate_t, down_rows, t_tc)
    w_sc = peer_down_sc(ids, xn, gate_t[:, t_tc:].T, down_wide, t_tc)
    return jnp.concatenate([peer_up_sc(ids, w_t.T, up_wide, 0), peer_up_sc(ids, w_sc, up_wide, t_tc)], axis=0)


def _rms_kernel(x_ref, r_ref, g_ref, o_ref):
    x = x_ref[...] + r_ref[...]
    o_ref[...] = x * lax.rsqrt(jnp.mean(x * x, axis=-1, keepdims=True) + NORM_EPS) * g_ref[...]


def rms_norm_sum(x, r, g, *, tm=512):
    T, D = x.shape
    tm = min(tm, T)
    row = pl.BlockSpec((tm, D), lambda i: (i, 0))
    return pl.pallas_call(
        _rms_kernel,
        grid=(T // tm,),
        in_specs=[row, row, pl.BlockSpec((1, D), lambda i: (0, 0))],
        out_specs=row,
        out_shape=jax.ShapeDtypeStruct((T, D), jnp.float32),
        compiler_params=_cparams(("parallel",)),
        name="final_norm",
    )(x, r, g.reshape(1, D))


PIECE_SEQS = 2


def _piece_bounds(B):
    sizes = [PIECE_SEQS] * (B // PIECE_SEQS) + ([B % PIECE_SEQS] if B % PIECE_SEQS else [])
    if sizes[-1] > 1:
        sizes[-1:] = [sizes[-1] - sizes[-1] // 2, sizes[-1] // 2]
    starts = [sum(sizes[:i]) for i in range(len(sizes))]
    return list(zip(starts, sizes))


def kernel(x, rel_table, mix_norm_g, attn_w_in, attn_w_out, attn_sink, sg_w_in, sg_ln_g, sg_ln_b, sg_w_spatial, sg_b_spatial, sg_w_out, ffn_norm_g, peer_w_query, peer_sub_keys, peer_down, peer_up, final_norm_g):
    B, S, D = x.shape
    bf16 = jnp.bfloat16

    bias = attn_bias(rel_table)
    attn_in, attn_out = attn_w_in[0].astype(bf16), attn_w_out[0].astype(bf16)
    sg_in, sg_out = sg_w_in[0].astype(bf16), sg_w_out[0].astype(bf16)
    w_query = peer_w_query.astype(bf16)
    down = [pack_table(peer_down, i, wide=True, split=True) for i in range(2)]
    up = [pack_table(peer_up, i, wide=True, split=False)[0] for i in range(2)]

    mid = []
    for b0, nb in _piece_bounds(B):
        h = x[b0:b0 + nb].reshape(nb * S, D)
        qkv, = norm_matmul(h, mix_norm_g[0], attn_in, out_dtype=bf16)
        att = window_attention(qkv.reshape(nb, S, QKV_DIM), bias, attn_sink[0])
        h = matmul_residual(att.reshape(nb * S, Q_DIM), attn_out, h)
        mid.append((h, peer_layer(h, ffn_norm_g[0], w_query[0], peer_sub_keys[0], *down[0], up[0])))

    outs = []
    for h, y in mid:
        z, h = norm_matmul(h, mix_norm_g[1], sg_in, res=y, out_dtype=bf16, act=True, tn=1536)
        gated = spatial_gate(z, sg_ln_g[0], sg_ln_b[0], sg_w_spatial[0], sg_b_spatial[0])
        h = matmul_residual(gated, sg_out, h)
        y = peer_layer(h, ffn_norm_g[1], w_query[1], peer_sub_keys[1], *down[1], up[1])
        outs.append(rms_norm_sum(h, y, final_norm_g).reshape(-1, S, D))
    return jnp.concatenate(outs, axis=0)
```

```python
import dataclasses
import functools
import math

import jax
import jax.numpy as jnp
from jax import lax
from jax.experimental import pallas as pl
from jax.experimental.pallas import tpu as pltpu
from jax.experimental.pallas import tpu_sc as plsc

D_MODEL = 1024
HEAD_DIM = 64
N_Q_HEADS = 16
N_KV_HEADS = 4
GQA_GROUP = 4
WINDOW = 128
BLOCK = 128
REL_BUCKETS = 32
REL_MAX_DIST = 128
Q_DIM = N_Q_HEADS * HEAD_DIM
KV_DIM = N_KV_HEADS * HEAD_DIM
QKV_DIM = Q_DIM + 2 * KV_DIM
CHUNK = 128
D_GATE = 3072
N_SG_GROUPS = 8
SG_GROUP_DIM = D_GATE // N_SG_GROUPS
N_KEYS = 128
PEER_HEADS = 8
PEER_TOPK = 16
D_HALF = 128
N_SLOTS = PEER_HEADS * PEER_TOPK
NORM_EPS = 1e-6
LN_EPS = 1e-5

LANES = 128
SUBLANES = 8
ROW_WORDS = D_MODEL // 2
ROW_SUB = ROW_WORDS // LANES
PLANE_STRIDE = 136
VMEM_LIMIT = 56 * 1024 * 1024

_GELU_C = math.sqrt(2.0 / math.pi)


def _gelu(x):
    return 0.5 * x * (1.0 + jnp.tanh(_GELU_C * (x + 0.044715 * (x * x * x))))


def _cparams(sem):
    return pltpu.CompilerParams(dimension_semantics=sem, vmem_limit_bytes=VMEM_LIMIT)


def _norm_matmul_kernel(*refs, act, emit_xn, add_res, precision):
    refs = list(refs)
    x_ref = refs.pop(0)
    r_ref = refs.pop(0) if add_res else None
    g_ref, w_ref, o_ref = refs.pop(0), refs.pop(0), refs.pop(0)
    xn_ref = refs.pop(0) if emit_xn else None
    sum_ref = refs.pop(0) if add_res else None
    xs_ref = refs.pop(0)

    @pl.when(pl.program_id(1) == 0)
    def _():
        x = x_ref[...]
        if add_res:
            x = x + r_ref[...]
            sum_ref[...] = x
        y = x * lax.rsqrt(jnp.mean(x * x, axis=-1, keepdims=True) + NORM_EPS) * g_ref[...]
        xs_ref[...] = y.astype(xs_ref.dtype)
        if emit_xn:
            xn_ref[...] = y

    acc = jnp.dot(xs_ref[...], w_ref[...], preferred_element_type=jnp.float32,
                  precision=precision)
    if act:
        acc = _gelu(acc)
    o_ref[...] = acc.astype(o_ref.dtype)


def norm_matmul(x, g, w, *, out_dtype, res=None, act=False, emit_xn=False, tm=512, tn=None,
                precision=None):
    T, D = x.shape
    N = w.shape[1]
    tn = tn or N
    tm = min(tm, T)
    row = pl.BlockSpec((tm, D), lambda i, j: (i, 0))
    out_shape = [jax.ShapeDtypeStruct((T, N), out_dtype)]
    out_specs = [pl.BlockSpec((tm, tn), lambda i, j: (i, j))]
    for flag in (emit_xn, res is not None):
        if flag:
            out_shape.append(jax.ShapeDtypeStruct((T, D), jnp.float32))
            out_specs.append(row)
    args = [x] + ([res] if res is not None else []) + [g.reshape(1, D), w]
    in_specs = [row] * (len(args) - 2) + [pl.BlockSpec((1, D), lambda i, j: (0, 0)),
                                          pl.BlockSpec((D, tn), lambda i, j: (0, j))]
    return pl.pallas_call(
        functools.partial(_norm_matmul_kernel, act=act, emit_xn=emit_xn, add_res=res is not None,
                          precision=precision),
        grid=(T // tm, N // tn),
        in_specs=in_specs,
        out_specs=out_specs,
        out_shape=out_shape,
        scratch_shapes=[pltpu.VMEM((tm, D), w.dtype)],
        compiler_params=_cparams(("parallel", "arbitrary")),
        name="norm_matmul",
    )(*args)


def _matmul_res_kernel(a_ref, w_ref, h_ref, o_ref):
    o_ref[...] = h_ref[...] + jnp.dot(a_ref[...], w_ref[...],
                                      preferred_element_type=jnp.float32)


def matmul_residual(a, w, h, *, tm=512):
    T, K = a.shape
    N = w.shape[1]
    tm = min(tm, T)
    return pl.pallas_call(
        _matmul_res_kernel,
        grid=(T // tm,),
        in_specs=[pl.BlockSpec((tm, K), lambda i: (i, 0)),
                  pl.BlockSpec((K, N), lambda i: (0, 0)),
                  pl.BlockSpec((tm, N), lambda i: (i, 0))],
        out_specs=pl.BlockSpec((tm, N), lambda i: (i, 0)),
        out_shape=jax.ShapeDtypeStruct((T, N), jnp.float32),
        compiler_params=_cparams(("parallel",)),
        name="matmul_residual",
    )(a, w, h)


def _t5_bucket(rel):
    nb = REL_BUCKETS // 2
    max_exact = nb // 2
    ret = jnp.where(rel > 0, nb, 0)
    n = jnp.abs(rel)
    nf = jnp.maximum(n, 1).astype(jnp.float32)
    large = max_exact + (jnp.log(nf / max_exact) / math.log(REL_MAX_DIST / max_exact)
                         * (nb - max_exact)).astype(jnp.int32)
    large = jnp.minimum(large, nb - 1)
    return (ret + jnp.where(n < max_exact, n, large)).astype(jnp.int32)


def _bias_kernel(bucket_ref, window_ref, table_ref, o_ref):
    bucket = bucket_ref[...]
    in_window = window_ref[...] > 0
    for hq in range(N_Q_HEADS):
        acc = jnp.zeros(bucket.shape, jnp.float32)
        for b in range(REL_BUCKETS):
            acc = jnp.where(bucket == b, table_ref[b, hq], acc)
        o_ref[hq] = jnp.where(in_window, acc, -jnp.inf)


def attn_bias(rel_table):
    qi = jnp.arange(BLOCK)[:, None]
    kj = jnp.arange(3 * BLOCK)[None, :]
    rel = kj - BLOCK - qi
    bucket = _t5_bucket(rel)
    window = (jnp.abs(rel) <= WINDOW).astype(jnp.int32)
    return pl.pallas_call(
        _bias_kernel,
        in_specs=[pl.BlockSpec(memory_space=pltpu.VMEM),
                  pl.BlockSpec(memory_space=pltpu.VMEM),
                  pl.BlockSpec(memory_space=pltpu.SMEM)],
        out_specs=pl.BlockSpec(memory_space=pltpu.VMEM),
        out_shape=jax.ShapeDtypeStruct((N_Q_HEADS, BLOCK, 3 * BLOCK), jnp.float32),
        name="attn_bias",
    )(bucket, window, rel_table)


def _attn_kernel(cur_ref, prev_ref, next_ref, bias_ref, sink_ref, o_ref):
    i = pl.program_id(1)
    nb = pl.num_programs(1)
    q = cur_ref[0, :, 0:Q_DIM]
    kband = jnp.concatenate([prev_ref[0, :, 0:KV_DIM], cur_ref[0, :, Q_DIM:Q_DIM + KV_DIM],
                             next_ref[0, :, 0:KV_DIM]], axis=0)
    vband = jnp.concatenate([prev_ref[0, :, KV_DIM:2 * KV_DIM], cur_ref[0, :, Q_DIM + KV_DIM:QKV_DIM],
                             next_ref[0, :, KV_DIM:2 * KV_DIM]], axis=0)
    col = lax.broadcasted_iota(jnp.int32, (1, 3 * BLOCK), 1)
    valid = jnp.logical_and(jnp.logical_or(col >= BLOCK, i > 0),
                            jnp.logical_or(col < 2 * BLOCK, i < nb - 1))
    heads = range(N_Q_HEADS)
    kv = [(kband[:, hk * HEAD_DIM:(hk + 1) * HEAD_DIM], vband[:, hk * HEAD_DIM:(hk + 1) * HEAD_DIM])
          for hk in range(N_KV_HEADS)]
    s = []
    for hq in heads:
        qh = q[:, hq * HEAD_DIM:(hq + 1) * HEAD_DIM]
        sc = lax.dot_general(qh, kv[hq // GQA_GROUP][0], (((1,), (1,)), ((), ())),
                             preferred_element_type=jnp.float32)
        s.append(jnp.where(valid, sc * (HEAD_DIM ** -0.5) + bias_ref[hq], -jnp.inf))
    mx = [jnp.maximum(jnp.max(s[hq], axis=-1, keepdims=True), sink_ref[hq]) for hq in heads]
    p = [jnp.exp(s[hq] - mx[hq]) for hq in heads]
    denom = [jnp.sum(p[hq], axis=-1, keepdims=True) + jnp.exp(sink_ref[hq] - mx[hq]) for hq in heads]
    outs = [jnp.dot((p[hq] / denom[hq]).astype(jnp.bfloat16), kv[hq // GQA_GROUP][1],
                    preferred_element_type=jnp.float32) for hq in heads]
    o_ref[0] = jnp.concatenate(outs, axis=-1).astype(o_ref.dtype)


def window_attention(qkv, bias, sink):
    B, S, _ = qkv.shape
    nb = S // BLOCK
    kv_col = Q_DIM // (2 * KV_DIM)
    return pl.pallas_call(
        _attn_kernel,
        grid=(B, nb),
        in_specs=[pl.BlockSpec((1, BLOCK, QKV_DIM), lambda b, i: (b, i, 0)),
                  pl.BlockSpec((1, BLOCK, 2 * KV_DIM),
                               lambda b, i: (b, jnp.maximum(i - 1, 0), kv_col)),
                  pl.BlockSpec((1, BLOCK, 2 * KV_DIM),
                               lambda b, i: (b, jnp.minimum(i + 1, nb - 1), kv_col)),
                  pl.BlockSpec((N_Q_HEADS, BLOCK, 3 * BLOCK), lambda b, i: (0, 0, 0)),
                  pl.BlockSpec(memory_space=pltpu.SMEM)],
        out_specs=pl.BlockSpec((1, BLOCK, Q_DIM), lambda b, i: (b, i, 0)),
        out_shape=jax.ShapeDtypeStruct((B, S, Q_DIM), jnp.bfloat16),
        compiler_params=_cparams(("parallel", "arbitrary")),
        name="window_attention",
    )(qkv, qkv, qkv, bias, sink)


def _spatial_gate_kernel(z_ref, g_ref, b_ref, wsp_ref, bsp_ref, o_ref):
    v = z_ref[:, D_GATE:2 * D_GATE].astype(jnp.float32)
    mu = jnp.mean(v, axis=-1, keepdims=True)
    vc = v - mu
    var = jnp.mean(vc * vc, axis=-1, keepdims=True)
    vn = (vc * lax.rsqrt(var + LN_EPS) * g_ref[...] + b_ref[...]).astype(jnp.bfloat16)
    cols = [(grp * SG_GROUP_DIM, (grp + 1) * SG_GROUP_DIM) for grp in range(N_SG_GROUPS)]
    mixed = [jnp.dot(wsp_ref[grp], vn[:, lo:hi], preferred_element_type=jnp.float32)
             for grp, (lo, hi) in enumerate(cols)]
    for grp, (lo, hi) in enumerate(cols):
        u = z_ref[:, lo:hi].astype(jnp.float32)
        o_ref[:, lo:hi] = (u * (mixed[grp] + bsp_ref[:, grp:grp + 1])).astype(o_ref.dtype)


def spatial_gate(z, ln_g, ln_b, w_sp, b_sp):
    T = z.shape[0]
    return pl.pallas_call(
        _spatial_gate_kernel,
        grid=(T // CHUNK,),
        in_specs=[pl.BlockSpec((CHUNK, 2 * D_GATE), lambda i: (i, 0)),
                  pl.BlockSpec((1, D_GATE), lambda i: (0, 0)),
                  pl.BlockSpec((1, D_GATE), lambda i: (0, 0)),
                  pl.BlockSpec((N_SG_GROUPS, CHUNK, CHUNK), lambda i: (0, 0, 0)),
                  pl.BlockSpec((CHUNK, N_SG_GROUPS), lambda i: (0, 0))],
        out_specs=pl.BlockSpec((CHUNK, D_GATE), lambda i: (i, 0)),
        out_shape=jax.ShapeDtypeStruct((T, D_GATE), jnp.bfloat16),
        compiler_params=_cparams(("parallel",)),
        name="spatial_gate",
    )(z, ln_g.reshape(1, D_GATE), ln_b.reshape(1, D_GATE), w_sp.astype(jnp.bfloat16), b_sp.T)


def _oddeven_merge_sort_pairs(n):
    pairs = []
    p = 1
    while p < n:
        k = p
        while k >= 1:
            for j in range(k % p, n - k, 2 * k):
                for i in range(min(k, n - j - k)):
                    if (i + j) // (2 * p) == (i + j + k) // (2 * p):
                        pairs.append((i + j, i + j + k))
            k //= 2
        p *= 2
    return pairs


_SORT16 = _oddeven_merge_sort_pairs(N_KEYS // SUBLANES)


def _top16_of_keys(s):
    nv = N_KEYS // SUBLANES
    L = s.shape[1]
    sub = lax.broadcasted_iota(jnp.int32, (SUBLANES, L), 0)
    v = [s[j * SUBLANES:(j + 1) * SUBLANES] for j in range(nv)]
    ids = [sub + j * SUBLANES for j in range(nv)]
    for i, j in _SORT16:
        swap = jnp.logical_or(v[j] > v[i], jnp.logical_and(v[j] == v[i], ids[j] < ids[i]))
        v[i], v[j] = jnp.where(swap, v[j], v[i]), jnp.where(swap, v[i], v[j])
        ids[i], ids[j] = jnp.where(swap, ids[j], ids[i]), jnp.where(swap, ids[i], ids[j])
    vals, picks = [], []
    for it in range(PEER_TOPK):
        m = jnp.max(v[0], axis=0, keepdims=True)
        am = jnp.min(jnp.where(v[0] == m, ids[0], N_KEYS), axis=0, keepdims=True)
        hit = ids[0] == am
        vals.append(m)
        picks.append(am)
        last = PEER_TOPK - 1 - it
        for j in range(last):
            v[j] = jnp.where(hit, v[j + 1], v[j])
            ids[j] = jnp.where(hit, ids[j + 1], ids[j])
        v[last] = jnp.where(hit, -jnp.inf, v[last])
    return jnp.concatenate(vals, axis=0), jnp.concatenate(picks, axis=0)


_CAND_ROWS = (
    [(a, 0) for a in range(16)]
    + [None] + [(0, b) for b in range(1, 16)]
    + [None] + [(a, 1) for a in range(1, 8)]
    + [None, None] + [(1, b) for b in range(2, 8)]
    + [(2, 2), (3, 2), (4, 2), (2, 3), (2, 4), (3, 3), None, None]
)
assert sorted(x for x in _CAND_ROWS if x) == sorted(
    (a, b) for a in range(16) for b in range(16) if (a + 1) * (b + 1) <= 16)


def _rows(x, picks):
    pieces, i = [], 0
    while i < len(picks):
        j = i
        while j + 1 < len(picks) and picks[j + 1] == picks[j] + 1:
            j += 1
        pieces.append(x[picks[i]:picks[j] + 1])
        i = j + 1
    return pieces[0] if len(pieces) == 1 else jnp.concatenate(pieces, axis=0)


def _cand_positions(L):
    row = lax.broadcasted_iota(jnp.int32, (len(_CAND_ROWS), L), 0)
    pos = jnp.full(row.shape, _PAD_POS, jnp.int32)
    for r, c in enumerate(_CAND_ROWS):
        if c is not None:
            pos = jnp.where(row == r, c[0] * PEER_TOPK + c[1], pos)
    return pos


_PAD_POS = 1 << 20


def _joint_top16(v1, i1, v2, i2, pos):
    a_of = [c[0] if c else 0 for c in _CAND_ROWS]
    b_of = [c[1] if c else 0 for c in _CAND_ROWS]
    cand = jnp.where(pos < _PAD_POS, _rows(v1, a_of) + _rows(v2, b_of), -jnp.inf)
    cidx = _rows(i1, a_of) * N_KEYS + _rows(i2, b_of)
    vals, picks = [], []
    for _ in range(PEER_TOPK):
        m = jnp.max(cand, axis=0, keepdims=True)
        pm = jnp.min(jnp.where(cand == m, pos, 1 << 21), axis=0, keepdims=True)
        hit = pos == pm
        vals.append(m)
        picks.append(jnp.max(jnp.where(hit, cidx, -1), axis=0, keepdims=True))
        cand = jnp.where(hit, -jnp.inf, cand)
    return jnp.concatenate(vals, axis=0), jnp.concatenate(picks, axis=0)


def _route_kernel(q_ref, keys_ref, idx_ref, gate_ref, *, precision):
    nt = (((1,), (1,)), ((), ()))
    pos = _cand_positions(LANES)
    for j in range(q_ref.shape[0] // LANES):
        q = q_ref[j * LANES:(j + 1) * LANES, :]
        s1 = lax.dot_general(keys_ref[0, 0], q[:, 0:D_HALF], nt,
                             preferred_element_type=jnp.float32, precision=precision)
        s2 = lax.dot_general(keys_ref[0, 1], q[:, D_HALF:2 * D_HALF], nt,
                             preferred_element_type=jnp.float32, precision=precision)
        v1, i1 = _top16_of_keys(s1)
        v2, i2 = _top16_of_keys(s2)
        top_s, top_i = _joint_top16(v1, i1, v2, i2, pos)
        e = jnp.exp(top_s - top_s[0:1])
        gate_ref[:, j * LANES:(j + 1) * LANES] = e / jnp.sum(e, axis=0, keepdims=True)
        idx_ref[:, j * LANES:(j + 1) * LANES] = top_i * ROW_SUB


def peer_route(q, sub_keys, *, tb=512, precision=None):
    T = q.shape[0]
    tb = min(tb, T)
    return pl.pallas_call(
        functools.partial(_route_kernel, precision=precision),
        grid=(T // tb, PEER_HEADS),
        in_specs=[pl.BlockSpec((tb, 2 * D_HALF), lambda i, h: (i, h)),
                  pl.BlockSpec((1, 2, N_KEYS, D_HALF), lambda i, h: (h, 0, 0, 0))],
        out_specs=[pl.BlockSpec((PEER_TOPK, tb), lambda i, h: (h, i)),
                   pl.BlockSpec((PEER_TOPK, tb), lambda i, h: (h, i))],
        out_shape=[jax.ShapeDtypeStruct((N_SLOTS, T), jnp.int32),
                   jax.ShapeDtypeStruct((N_SLOTS, T), jnp.float32)],
        compiler_params=_cparams(("parallel", "arbitrary")),
        name="peer_route",
    )(q, sub_keys)


def _pack_kernel(x_ref, o_ref, *, split_rows):
    x = x_ref[...]
    half = x.shape[1] // 2

    def bf16_bits(v):
        return pltpu.bitcast(v.astype(jnp.bfloat16).astype(jnp.float32), jnp.uint32)

    words = (bf16_bits(x[:, :half]) >> 16) | (bf16_bits(x[:, half:]) & jnp.uint32(0xFFFF0000))
    if split_rows:
        n = x.shape[0]
        for c in range(ROW_SUB):
            o_ref[pl.ds(c, n, stride=ROW_SUB), :] = words[:, c * LANES:(c + 1) * LANES]
    else:
        o_ref[...] = words


def pack_table(tbls, layer, *, split_rows, te=512):
    _, E, D = tbls.shape
    out_block, out_shape = ((te * ROW_SUB, LANES), (E * ROW_SUB, LANES)) if split_rows else ((te, D // 2), (E, D // 2))
    return pl.pallas_call(
        functools.partial(_pack_kernel, split_rows=split_rows),
        grid=(E // te,),
        in_specs=[pl.BlockSpec((None, te, D), lambda i: (layer, i, 0))],
        out_specs=pl.BlockSpec(out_block, lambda i: (i, 0)),
        out_shape=jax.ShapeDtypeStruct(out_shape, jnp.uint32),
        compiler_params=_cparams(("parallel",)),
        name="pack_table",
    )(tbls)


def _unpack(words):
    lo = pltpu.bitcast(words << 16, jnp.float32)
    hi = pltpu.bitcast(words & jnp.uint32(0xFFFF0000), jnp.float32)
    return lo, hi


def _gather_rows(idx_ref, tbl_ref, t, planes_ref):
    for k in range(N_SLOTS):
        off = pl.multiple_of(idx_ref[t, k], ROW_SUB)
        planes_ref[pl.ds(k, ROW_SUB, stride=PLANE_STRIDE), :] = tbl_ref[pl.ds(off, ROW_SUB), :]


def _plane(planes_ref, c):
    return _unpack(planes_ref[c * PLANE_STRIDE:c * PLANE_STRIDE + N_SLOTS, :])


def _peer_down_kernel(idx_ref, x_ref, gate_ref, tbl_ref, o_ref, planes_a, planes_b, part_ref):
    tb = x_ref.shape[0]
    lane = lax.broadcasted_iota(jnp.int32, (N_SLOTS, tb), 1)

    def compute(t, planes_ref, slot):
        acc = jnp.zeros((N_SLOTS, LANES), jnp.float32)
        for c in range(ROW_SUB):
            lo, hi = _plane(planes_ref, c)
            acc = acc + lo * x_ref[t, c:c + 1, :] + hi * x_ref[t, ROW_SUB + c:ROW_SUB + c + 1, :]
        part_ref[slot] = acc

    def place_pair(t0):
        col_a = jnp.sum(part_ref[0], axis=1, keepdims=True)
        col_b = jnp.sum(part_ref[1], axis=1, keepdims=True)
        o_ref[...] = jnp.where(lane == t0 - 2, col_a, jnp.where(lane == t0 - 1, col_b, o_ref[...]))

    o_ref[...] = jnp.zeros(o_ref.shape, jnp.float32)
    part_ref[...] = jnp.zeros(part_ref.shape, jnp.float32)
    _gather_rows(idx_ref, tbl_ref, 0, planes_a)

    def pair(i, carry):
        t0 = 2 * i
        place_pair(t0)
        _gather_rows(idx_ref, tbl_ref, t0 + 1, planes_b)
        compute(t0, planes_a, 0)
        _gather_rows(idx_ref, tbl_ref, jnp.minimum(t0 + 2, tb - 1), planes_a)
        compute(t0 + 1, planes_b, 1)
        return carry

    lax.fori_loop(0, tb // 2, pair, 0)
    place_pair(tb)
    o_ref[...] = _gelu(o_ref[...]) * gate_ref[...]


def peer_down(idx, xn3, gate_t, tbl, *, tb=128):
    T = xn3.shape[0]
    tb = min(tb, T)
    assert tb % 2 == 0 and T % tb == 0
    planes = pltpu.VMEM((ROW_SUB * PLANE_STRIDE, LANES), jnp.uint32)
    return pl.pallas_call(
        _peer_down_kernel,
        grid=(T // tb,),
        in_specs=[pl.BlockSpec((tb, N_SLOTS), lambda i: (i, 0), memory_space=pltpu.SMEM),
                  pl.BlockSpec((tb, SUBLANES, LANES), lambda i: (i, 0, 0)),
                  pl.BlockSpec((N_SLOTS, tb), lambda i: (0, i)),
                  pl.BlockSpec(tbl.shape, lambda i: (0, 0), pipeline_mode=pl.Buffered(1))],
        out_specs=pl.BlockSpec((N_SLOTS, tb), lambda i: (0, i)),
        out_shape=jax.ShapeDtypeStruct((N_SLOTS, T), jnp.float32),
        scratch_shapes=[planes, planes, pltpu.VMEM((2, N_SLOTS, LANES), jnp.float32)],
        compiler_params=_cparams(("arbitrary",)),
        name="peer_down",
    )(idx, xn3, gate_t, tbl)


SC_CORES = 2
SC_SUBCORES = 16
SC_WORKERS = SC_CORES * SC_SUBCORES
SC_LANES = 16
SC_GATHER_ROWS = N_SLOTS // 2
SC_ROW_BLOCK = 8
SC_TOKEN_BLOCK = 32
SC_COL_GROUP = 8


def _sc_params():
    cp = pltpu.CompilerParams()
    if "needs_layout_passes" in pltpu.CompilerParams.__dataclass_fields__:
        cp = dataclasses.replace(cp, needs_layout_passes=False)
    return cp


def peer_up_sc(ids, w, tbl):
    T = ids.shape[0]
    L, CH, RB, TBK = SC_LANES, SC_GATHER_ROWS, SC_ROW_BLOCK, SC_TOKEN_BLOCK
    assert T % (SC_WORKERS * TBK) == 0
    tpw = T // SC_WORKERS
    mesh = plsc.VectorSubcoreMesh(core_axis_name="c", subcore_axis_name="s")
    rows_buf = pltpu.VMEM((CH, ROW_WORDS), jnp.uint32)
    y_buf = pltpu.VMEM((D_MODEL,), jnp.float32)

    @functools.partial(
        pl.kernel, mesh=mesh, compiler_params=_sc_params(),
        out_type=jax.ShapeDtypeStruct((T, D_MODEL), jnp.float32),
        scratch_types=[pltpu.VMEM((TBK, N_SLOTS), jnp.int32), pltpu.VMEM((TBK, N_SLOTS), jnp.float32),
                       rows_buf, rows_buf, y_buf, y_buf] + [pltpu.SemaphoreType.DMA] * 4,
    )
    def up_kernel(ids_hbm, w_hbm, tbl_hbm, out_hbm, ids_v, w_v, rows0, rows1, y0, y1, g0, g1, o0, o1):
        base = (lax.axis_index("s") * SC_CORES + lax.axis_index("c")) * tpw
        rows, gsem, ys, osem = (rows0, rows1), (g0, g1), (y0, y1), (o0, o1)

        def gather(tl, half):
            return pltpu.make_async_copy(tbl_hbm.at[ids_v.at[tl, pl.ds(half * CH, CH)]], rows[half], gsem[half])

        def out_copy(tok, slot):
            return pltpu.make_async_copy(ys[slot], out_hbm.at[tok], osem[slot])

        def accumulate(tl, half, y_v):
            rows_v = rows[half]

            def block(b, carry):
                r0 = b * RB
                tlv = jnp.full((L,), tl, jnp.int32)
                wks = [plsc.load_gather(w_v, [tlv, jnp.full((L,), half * CH + r0 + r, jnp.int32)])
                       for r in range(RB)]
                for q in range(ROW_WORDS // (SC_COL_GROUP * L)):
                    acc = [None] * (2 * SC_COL_GROUP)
                    for r in range(RB):
                        for i in range(SC_COL_GROUP):
                            v = rows_v[r0 + r, pl.ds((q * SC_COL_GROUP + i) * L, L)]
                            lo = plsc.bitcast(v << 16, jnp.float32) * wks[r]
                            hi = plsc.bitcast(v & jnp.uint32(0xFFFF0000), jnp.float32) * wks[r]
                            acc[2 * i] = lo if r == 0 else acc[2 * i] + lo
                            acc[2 * i + 1] = hi if r == 0 else acc[2 * i + 1] + hi
                    for i in range(SC_COL_GROUP):
                        col = (q * SC_COL_GROUP + i) * L
                        plsc.addupdate(y_v.at[pl.ds(col, L)], acc[2 * i])
                        plsc.addupdate(y_v.at[pl.ds(ROW_WORDS + col, L)], acc[2 * i + 1])
                return carry

            lax.fori_loop(0, CH // RB, block, 0)

        def token_block(bi, carry):
            tok0 = base + bi * TBK
            pltpu.sync_copy(ids_hbm.at[pl.ds(tok0, TBK)], ids_v)
            pltpu.sync_copy(w_hbm.at[pl.ds(tok0, TBK)], w_v)
            gather(0, 0).start()

            def token_pair(pi, carry2):
                for slot in range(2):
                    tl = 2 * pi + slot
                    y_v = ys[slot]

                    @pl.when(pi > 0)
                    def _():
                        out_copy(tok0 + tl - 2, slot).wait()

                    zero = jnp.zeros((L,), jnp.float32)
                    for j in range(D_MODEL // L):
                        y_v[pl.ds(j * L, L)] = zero
                    gather(tl, 1).start()
                    gather(tl, 0).wait()
                    accumulate(tl, 0, y_v)

                    @pl.when(tl + 1 < TBK)
                    def _():
                        gather(tl + 1, 0).start()

                    gather(tl, 1).wait()
                    accumulate(tl, 1, y_v)
                    out_copy(tok0 + tl, slot).start()
                return carry2

            lax.fori_loop(0, TBK // 2, token_pair, 0)
            out_copy(tok0 + TBK - 2, 0).wait()
            out_copy(tok0 + TBK - 1, 1).wait()
            return carry

        lax.fori_loop(0, tpw // TBK, token_block, 0)

    return up_kernel(ids, w, tbl)


def peer_layer(h, g, w_query, sub_keys, down_rows, up_rows):
    T = h.shape[0]
    q, xn = norm_matmul(h, g, w_query, out_dtype=jnp.float32, emit_xn=True)
    idx_t, gate_t = peer_route(q, sub_keys, precision=lax.Precision.HIGHEST)
    idx = idx_t.T
    w_t = peer_down(idx, xn.reshape(T, SUBLANES, LANES), gate_t, down_rows)
    return peer_up_sc(idx // ROW_SUB, w_t.T, up_rows)


def _rms_kernel(x_ref, r_ref, g_ref, o_ref):
    x = x_ref[...] + r_ref[...]
    o_ref[...] = x * lax.rsqrt(jnp.mean(x * x, axis=-1, keepdims=True) + NORM_EPS) * g_ref[...]


def rms_norm_sum(x, r, g, *, tm=512):
    T, D = x.shape
    tm = min(tm, T)
    row = pl.BlockSpec((tm, D), lambda i: (i, 0))
    return pl.pallas_call(
        _rms_kernel,
        grid=(T // tm,),
        in_specs=[row, row, pl.BlockSpec((1, D), lambda i: (0, 0))],
        out_specs=row,
        out_shape=jax.ShapeDtypeStruct((T, D), jnp.float32),
        compiler_params=_cparams(("parallel",)),
        name="final_norm",
    )(x, r, g.reshape(1, D))


PIECE_SEQS = 2


def _piece_bounds(B):
    sizes = [PIECE_SEQS] * (B // PIECE_SEQS) + ([B % PIECE_SEQS] if B % PIECE_SEQS else [])
    if sizes[-1] > 1:
        sizes[-1:] = [sizes[-1] - sizes[-1] // 2, sizes[-1] // 2]
    starts = [sum(sizes[:i]) for i in range(len(sizes))]
    return list(zip(starts, sizes))


def kernel(x, rel_table, mix_norm_g, attn_w_in, attn_w_out, attn_sink, sg_w_in, sg_ln_g, sg_ln_b, sg_w_spatial, sg_b_spatial, sg_w_out, ffn_norm_g, peer_w_query, peer_sub_keys, peer_down, peer_up, final_norm_g):
    B, S, D = x.shape
    bf16 = jnp.bfloat16

    bias = attn_bias(rel_table)
    attn_in, attn_out = attn_w_in[0].astype(bf16), attn_w_out[0].astype(bf16)
    sg_in, sg_out = sg_w_in[0].astype(bf16), sg_w_out[0].astype(bf16)
    w_query = peer_w_query.astype(bf16)
    down_rows = [pack_table(peer_down, i, split_rows=True) for i in range(2)]
    up_rows = [pack_table(peer_up, i, split_rows=False) for i in range(2)]

    mid = []
    for b0, nb in _piece_bounds(B):
        h = x[b0:b0 + nb].reshape(nb * S, D)
        qkv, = norm_matmul(h, mix_norm_g[0], attn_in, out_dtype=bf16)
        att = window_attention(qkv.reshape(nb, S, QKV_DIM), bias, attn_sink[0])
        h = matmul_residual(att.reshape(nb * S, Q_DIM), attn_out, h)
        mid.append((h, peer_layer(h, ffn_norm_g[0], w_query[0], peer_sub_keys[0], down_rows[0], up_rows[0])))

    outs = []
    for h, y in mid:
        z, h = norm_matmul(h, mix_norm_g[1], sg_in, res=y, out_dtype=bf16, act=True, tn=1536)
        gated = spatial_gate(z, sg_ln_g[0], sg_ln_b[0], sg_w_spatial[0], sg_b_spatial[0])
        h = matmul_residual(gated, sg_out, h)
        y = peer_layer(h, ffn_norm_g[1], w_query[1], peer_sub_keys[1], down_rows[1], up_rows[1])
        outs.append(rms_norm_sum(h, y, final_norm_g).reshape(-1, S, D))
    return jnp.concatenate(outs, axis=0)
```

```python
import dataclasses
import functools
import math

import jax
import jax.numpy as jnp
from jax import lax
from jax.experimental import pallas as pl
from jax.experimental.pallas import tpu as pltpu
from jax.experimental.pallas import tpu_sc as plsc

D_MODEL = 1024
HEAD_DIM = 64
N_Q_HEADS = 16
N_KV_HEADS = 4
GQA_GROUP = 4
WINDOW = 128
BLOCK = 128
REL_BUCKETS = 32
REL_MAX_DIST = 128
Q_DIM = N_Q_HEADS * HEAD_DIM
KV_DIM = N_KV_HEADS * HEAD_DIM
QKV_DIM = Q_DIM + 2 * KV_DIM
CHUNK = 128
D_GATE = 3072
N_SG_GROUPS = 8
SG_GROUP_DIM = D_GATE // N_SG_GROUPS
N_KEYS = 128
PEER_HEADS = 8
PEER_TOPK = 16
D_HALF = 128
N_SLOTS = PEER_HEADS * PEER_TOPK
NORM_EPS = 1e-6
LN_EPS = 1e-5

LANES = 128
SUBLANES = 8
ROW_WORDS = D_MODEL // 2
ROW_SUB = ROW_WORDS // LANES
PLANE_STRIDE = 136
VMEM_LIMIT = 56 * 1024 * 1024

_GELU_C = math.sqrt(2.0 / math.pi)


def _gelu(x):
    return 0.5 * x * (1.0 + jnp.tanh(_GELU_C * (x + 0.044715 * (x * x * x))))


def _cparams(sem):
    return pltpu.CompilerParams(dimension_semantics=sem, vmem_limit_bytes=VMEM_LIMIT)


def _norm_matmul_kernel(*refs, act, emit_xn, add_res, precision):
    refs = list(refs)
    x_ref = refs.pop(0)
    r_ref = refs.pop(0) if add_res else None
    g_ref, w_ref, o_ref = refs.pop(0), refs.pop(0), refs.pop(0)
    xn_ref = refs.pop(0) if emit_xn else None
    sum_ref = refs.pop(0) if add_res else None
    xs_ref = refs.pop(0)

    @pl.when(pl.program_id(1) == 0)
    def _():
        x = x_ref[...]
        if add_res:
            x = x + r_ref[...]
            sum_ref[...] = x
        y = x * lax.rsqrt(jnp.mean(x * x, axis=-1, keepdims=True) + NORM_EPS) * g_ref[...]
        xs_ref[...] = y.astype(xs_ref.dtype)
        if emit_xn:
            xn_ref[...] = y

    acc = jnp.dot(xs_ref[...], w_ref[...], preferred_element_type=jnp.float32,
                  precision=precision)
    if act:
        acc = _gelu(acc)
    o_ref[...] = acc.astype(o_ref.dtype)


def norm_matmul(x, g, w, *, out_dtype, res=None, act=False, emit_xn=False, tm=512, tn=None,
                precision=None):
    T, D = x.shape
    N = w.shape[1]
    tn = tn or N
    tm = min(tm, T)
    row = pl.BlockSpec((tm, D), lambda i, j: (i, 0))
    out_shape = [jax.ShapeDtypeStruct((T, N), out_dtype)]
    out_specs = [pl.BlockSpec((tm, tn), lambda i, j: (i, j))]
    for flag in (emit_xn, res is not None):
        if flag:
            out_shape.append(jax.ShapeDtypeStruct((T, D), jnp.float32))
            out_specs.append(row)
    args = [x] + ([res] if res is not None else []) + [g.reshape(1, D), w]
    in_specs = [row] * (len(args) - 2) + [pl.BlockSpec((1, D), lambda i, j: (0, 0)),
                                          pl.BlockSpec((D, tn), lambda i, j: (0, j))]
    return pl.pallas_call(
        functools.partial(_norm_matmul_kernel, act=act, emit_xn=emit_xn, add_res=res is not None,
                          precision=precision),
        grid=(T // tm, N // tn),
        in_specs=in_specs,
        out_specs=out_specs,
        out_shape=out_shape,
        scratch_shapes=[pltpu.VMEM((tm, D), w.dtype)],
        compiler_params=_cparams(("parallel", "arbitrary")),
        name="norm_matmul",
    )(*args)


def _matmul_res_kernel(a_ref, w_ref, h_ref, o_ref):
    o_ref[...] = h_ref[...] + jnp.dot(a_ref[...], w_ref[...],
                                      preferred_element_type=jnp.float32)


def matmul_residual(a, w, h, *, tm=512):
    T, K = a.shape
    N = w.shape[1]
    tm = min(tm, T)
    return pl.pallas_call(
        _matmul_res_kernel,
        grid=(T // tm,),
        in_specs=[pl.BlockSpec((tm, K), lambda i: (i, 0)),
                  pl.BlockSpec((K, N), lambda i: (0, 0)),
                  pl.BlockSpec((tm, N), lambda i: (i, 0))],
        out_specs=pl.BlockSpec((tm, N), lambda i: (i, 0)),
        out_shape=jax.ShapeDtypeStruct((T, N), jnp.float32),
        compiler_params=_cparams(("parallel",)),
        name="matmul_residual",
    )(a, w, h)


def _t5_bucket(rel):
    nb = REL_BUCKETS // 2
    max_exact = nb // 2
    ret = jnp.where(rel > 0, nb, 0)
    n = jnp.abs(rel)
    nf = jnp.maximum(n, 1).astype(jnp.float32)
    large = max_exact + (jnp.log(nf / max_exact) / math.log(REL_MAX_DIST / max_exact)
                         * (nb - max_exact)).astype(jnp.int32)
    large = jnp.minimum(large, nb - 1)
    return (ret + jnp.where(n < max_exact, n, large)).astype(jnp.int32)


def _bias_kernel(bucket_ref, window_ref, table_ref, o_ref):
    bucket = bucket_ref[...]
    in_window = window_ref[...] > 0
    for hq in range(N_Q_HEADS):
        acc = jnp.zeros(bucket.shape, jnp.float32)
        for b in range(REL_BUCKETS):
            acc = jnp.where(bucket == b, table_ref[b, hq], acc)
        o_ref[hq] = jnp.where(in_window, acc, -jnp.inf)


def attn_bias(rel_table):
    qi = jnp.arange(BLOCK)[:, None]
    kj = jnp.arange(3 * BLOCK)[None, :]
    rel = kj - BLOCK - qi
    bucket = _t5_bucket(rel)
    window = (jnp.abs(rel) <= WINDOW).astype(jnp.int32)
    return pl.pallas_call(
        _bias_kernel,
        in_specs=[pl.BlockSpec(memory_space=pltpu.VMEM),
                  pl.BlockSpec(memory_space=pltpu.VMEM),
                  pl.BlockSpec(memory_space=pltpu.SMEM)],
        out_specs=pl.BlockSpec(memory_space=pltpu.VMEM),
        out_shape=jax.ShapeDtypeStruct((N_Q_HEADS, BLOCK, 3 * BLOCK), jnp.float32),
        name="attn_bias",
    )(bucket, window, rel_table)


def _attn_kernel(cur_ref, prev_ref, next_ref, bias_ref, sink_ref, o_ref):
    i = pl.program_id(1)
    nb = pl.num_programs(1)
    q = cur_ref[0, :, 0:Q_DIM]
    kband = jnp.concatenate([prev_ref[0, :, 0:KV_DIM], cur_ref[0, :, Q_DIM:Q_DIM + KV_DIM],
                             next_ref[0, :, 0:KV_DIM]], axis=0)
    vband = jnp.concatenate([prev_ref[0, :, KV_DIM:2 * KV_DIM], cur_ref[0, :, Q_DIM + KV_DIM:QKV_DIM],
                             next_ref[0, :, KV_DIM:2 * KV_DIM]], axis=0)
    col = lax.broadcasted_iota(jnp.int32, (1, 3 * BLOCK), 1)
    valid = jnp.logical_and(jnp.logical_or(col >= BLOCK, i > 0),
                            jnp.logical_or(col < 2 * BLOCK, i < nb - 1))
    heads = range(N_Q_HEADS)
    kv = [(kband[:, hk * HEAD_DIM:(hk + 1) * HEAD_DIM], vband[:, hk * HEAD_DIM:(hk + 1) * HEAD_DIM])
          for hk in range(N_KV_HEADS)]
    s = []
    for hq in heads:
        qh = q[:, hq * HEAD_DIM:(hq + 1) * HEAD_DIM]
        sc = lax.dot_general(qh, kv[hq // GQA_GROUP][0], (((1,), (1,)), ((), ())),
                             preferred_element_type=jnp.float32)
        s.append(jnp.where(valid, sc * (HEAD_DIM ** -0.5) + bias_ref[hq], -jnp.inf))
    mx = [jnp.maximum(jnp.max(s[hq], axis=-1, keepdims=True), sink_ref[hq]) for hq in heads]
    p = [jnp.exp(s[hq] - mx[hq]) for hq in heads]
    denom = [jnp.sum(p[hq], axis=-1, keepdims=True) + jnp.exp(sink_ref[hq] - mx[hq]) for hq in heads]
    outs = [jnp.dot((p[hq] / denom[hq]).astype(jnp.bfloat16), kv[hq // GQA_GROUP][1],
                    preferred_element_type=jnp.float32) for hq in heads]
    o_ref[0] = jnp.concatenate(outs, axis=-1).astype(o_ref.dtype)


def window_attention(qkv, bias, sink):
    B, S, _ = qkv.shape
    nb = S // BLOCK
    kv_col = Q_DIM // (2 * KV_DIM)
    return pl.pallas_call(
        _attn_kernel,
        grid=(B, nb),
        in_specs=[pl.BlockSpec((1, BLOCK, QKV_DIM), lambda b, i: (b, i, 0)),
                  pl.BlockSpec((1, BLOCK, 2 * KV_DIM),
                               lambda b, i: (b, jnp.maximum(i - 1, 0), kv_col)),
                  pl.BlockSpec((1, BLOCK, 2 * KV_DIM),
                               lambda b, i: (b, jnp.minimum(i + 1, nb - 1), kv_col)),
                  pl.BlockSpec((N_Q_HEADS, BLOCK, 3 * BLOCK), lambda b, i: (0, 0, 0)),
                  pl.BlockSpec(memory_space=pltpu.SMEM)],
        out_specs=pl.BlockSpec((1, BLOCK, Q_DIM), lambda b, i: (b, i, 0)),
        out_shape=jax.ShapeDtypeStruct((B, S, Q_DIM), jnp.bfloat16),
        compiler_params=_cparams(("parallel", "arbitrary")),
        name="window_attention",
    )(qkv, qkv, qkv, bias, sink)


def _spatial_gate_kernel(z_ref, g_ref, b_ref, wsp_ref, bsp_ref, o_ref):
    v = z_ref[:, D_GATE:2 * D_GATE].astype(jnp.float32)
    mu = jnp.mean(v, axis=-1, keepdims=True)
    vc = v - mu
    var = jnp.mean(vc * vc, axis=-1, keepdims=True)
    vn = (vc * lax.rsqrt(var + LN_EPS) * g_ref[...] + b_ref[...]).astype(jnp.bfloat16)
    for grp in range(N_SG_GROUPS):
        lo, hi = grp * SG_GROUP_DIM, (grp + 1) * SG_GROUP_DIM
        mixed = jnp.dot(wsp_ref[grp], vn[:, lo:hi], preferred_element_type=jnp.float32)
        mixed = mixed + bsp_ref[:, grp:grp + 1]
        u = z_ref[:, lo:hi].astype(jnp.float32)
        o_ref[:, lo:hi] = (u * mixed).astype(o_ref.dtype)


def spatial_gate(z, ln_g, ln_b, w_sp, b_sp):
    T = z.shape[0]
    return pl.pallas_call(
        _spatial_gate_kernel,
        grid=(T // CHUNK,),
        in_specs=[pl.BlockSpec((CHUNK, 2 * D_GATE), lambda i: (i, 0)),
                  pl.BlockSpec((1, D_GATE), lambda i: (0, 0)),
                  pl.BlockSpec((1, D_GATE), lambda i: (0, 0)),
                  pl.BlockSpec((N_SG_GROUPS, CHUNK, CHUNK), lambda i: (0, 0, 0)),
                  pl.BlockSpec((CHUNK, N_SG_GROUPS), lambda i: (0, 0))],
        out_specs=pl.BlockSpec((CHUNK, D_GATE), lambda i: (i, 0)),
        out_shape=jax.ShapeDtypeStruct((T, D_GATE), jnp.bfloat16),
        compiler_params=_cparams(("parallel",)),
        name="spatial_gate",
    )(z, ln_g.reshape(1, D_GATE), ln_b.reshape(1, D_GATE), w_sp.astype(jnp.bfloat16), b_sp.T)


def _oddeven_merge_sort_pairs(n):
    pairs = []
    p = 1
    while p < n:
        k = p
        while k >= 1:
            for j in range(k % p, n - k, 2 * k):
                for i in range(min(k, n - j - k)):
                    if (i + j) // (2 * p) == (i + j + k) // (2 * p):
                        pairs.append((i + j, i + j + k))
            k //= 2
        p *= 2
    return pairs


_SORT16 = _oddeven_merge_sort_pairs(N_KEYS // SUBLANES)


def _top16_of_keys(s):
    nv = N_KEYS // SUBLANES
    L = s.shape[1]
    sub = lax.broadcasted_iota(jnp.int32, (SUBLANES, L), 0)
    v = [s[j * SUBLANES:(j + 1) * SUBLANES] for j in range(nv)]
    ids = [sub + j * SUBLANES for j in range(nv)]
    for i, j in _SORT16:
        swap = jnp.logical_or(v[j] > v[i], jnp.logical_and(v[j] == v[i], ids[j] < ids[i]))
        v[i], v[j] = jnp.where(swap, v[j], v[i]), jnp.where(swap, v[i], v[j])
        ids[i], ids[j] = jnp.where(swap, ids[j], ids[i]), jnp.where(swap, ids[i], ids[j])
    vals, picks = [], []
    for it in range(PEER_TOPK):
        m = jnp.max(v[0], axis=0, keepdims=True)
        am = jnp.min(jnp.where(v[0] == m, ids[0], N_KEYS), axis=0, keepdims=True)
        hit = ids[0] == am
        vals.append(m)
        picks.append(am)
        last = PEER_TOPK - 1 - it
        for j in range(last):
            v[j] = jnp.where(hit, v[j + 1], v[j])
            ids[j] = jnp.where(hit, ids[j + 1], ids[j])
        v[last] = jnp.where(hit, -jnp.inf, v[last])
    return jnp.concatenate(vals, axis=0), jnp.concatenate(picks, axis=0)


_CAND_ROWS = (
    [(a, 0) for a in range(16)]
    + [None] + [(0, b) for b in range(1, 16)]
    + [None] + [(a, 1) for a in range(1, 8)]
    + [None, None] + [(1, b) for b in range(2, 8)]
    + [(2, 2), (3, 2), (4, 2), (2, 3), (2, 4), (3, 3), None, None]
)
assert sorted(x for x in _CAND_ROWS if x) == sorted(
    (a, b) for a in range(16) for b in range(16) if (a + 1) * (b + 1) <= 16)


def _rows(x, picks):
    pieces, i = [], 0
    while i < len(picks):
        j = i
        while j + 1 < len(picks) and picks[j + 1] == picks[j] + 1:
            j += 1
        pieces.append(x[picks[i]:picks[j] + 1])
        i = j + 1
    return pieces[0] if len(pieces) == 1 else jnp.concatenate(pieces, axis=0)


def _cand_positions(L):
    row = lax.broadcasted_iota(jnp.int32, (len(_CAND_ROWS), L), 0)
    pos = jnp.full(row.shape, _PAD_POS, jnp.int32)
    for r, c in enumerate(_CAND_ROWS):
        if c is not None:
            pos = jnp.where(row == r, c[0] * PEER_TOPK + c[1], pos)
    return pos


_PAD_POS = 1 << 20


def _joint_top16(v1, i1, v2, i2, pos):
    a_of = [c[0] if c else 0 for c in _CAND_ROWS]
    b_of = [c[1] if c else 0 for c in _CAND_ROWS]
    cand = jnp.where(pos < _PAD_POS, _rows(v1, a_of) + _rows(v2, b_of), -jnp.inf)
    cidx = _rows(i1, a_of) * N_KEYS + _rows(i2, b_of)
    vals, picks = [], []
    for _ in range(PEER_TOPK):
        m = jnp.max(cand, axis=0, keepdims=True)
        pm = jnp.min(jnp.where(cand == m, pos, 1 << 21), axis=0, keepdims=True)
        hit = pos == pm
        vals.append(m)
        picks.append(jnp.max(jnp.where(hit, cidx, -1), axis=0, keepdims=True))
        cand = jnp.where(hit, -jnp.inf, cand)
    return jnp.concatenate(vals, axis=0), jnp.concatenate(picks, axis=0)


def _route_kernel(q_ref, keys_ref, ids_ref, off_ref, gate_ref, ids_s, *, precision):
    nt = (((1,), (1,)), ((), ()))
    h = pl.program_id(1)
    rows = pl.ds(pl.multiple_of(h * PEER_TOPK, PEER_TOPK), PEER_TOPK)
    pos = _cand_positions(LANES)
    for j in range(q_ref.shape[0] // LANES):
        q = q_ref[j * LANES:(j + 1) * LANES, :]
        s1 = lax.dot_general(keys_ref[0, 0], q[:, 0:D_HALF], nt,
                             preferred_element_type=jnp.float32, precision=precision)
        s2 = lax.dot_general(keys_ref[0, 1], q[:, D_HALF:2 * D_HALF], nt,
                             preferred_element_type=jnp.float32, precision=precision)
        v1, i1 = _top16_of_keys(s1)
        v2, i2 = _top16_of_keys(s2)
        top_s, top_i = _joint_top16(v1, i1, v2, i2, pos)
        e = jnp.exp(top_s - top_s[0:1])
        gate_ref[:, j * LANES:(j + 1) * LANES] = e / jnp.sum(e, axis=0, keepdims=True)
        ids_s[rows, j * LANES:(j + 1) * LANES] = top_i

    @pl.when(h == PEER_HEADS - 1)
    def _():
        ids = ids_s[...].T
        ids_ref[...] = ids
        off_ref[...] = ids * ROW_SUB


def peer_route(q, sub_keys, *, tb=512, precision=None):
    T = q.shape[0]
    tb = min(tb, T)
    return pl.pallas_call(
        functools.partial(_route_kernel, precision=precision),
        grid=(T // tb, PEER_HEADS),
        in_specs=[pl.BlockSpec((tb, 2 * D_HALF), lambda i, h: (i, h)),
                  pl.BlockSpec((1, 2, N_KEYS, D_HALF), lambda i, h: (h, 0, 0, 0))],
        out_specs=[pl.BlockSpec((tb, N_SLOTS), lambda i, h: (i, 0)),
                   pl.BlockSpec((tb, N_SLOTS), lambda i, h: (i, 0)),
                   pl.BlockSpec((PEER_TOPK, tb), lambda i, h: (h, i))],
        out_shape=[jax.ShapeDtypeStruct((T, N_SLOTS), jnp.int32),
                   jax.ShapeDtypeStruct((T, N_SLOTS), jnp.int32),
                   jax.ShapeDtypeStruct((N_SLOTS, T), jnp.float32)],
        scratch_shapes=[pltpu.VMEM((N_SLOTS, tb), jnp.int32)],
        compiler_params=_cparams(("parallel", "arbitrary")),
        name="peer_route",
    )(q, sub_keys)


def _pack_kernel(x_ref, o_ref, *, split_rows):
    x = x_ref[...]
    half = x.shape[1] // 2

    def bf16_bits(v):
        return pltpu.bitcast(v.astype(jnp.bfloat16).astype(jnp.float32), jnp.uint32)

    words = (bf16_bits(x[:, :half]) >> 16) | (bf16_bits(x[:, half:]) & jnp.uint32(0xFFFF0000))
    if split_rows:
        n = x.shape[0]
        for c in range(ROW_SUB):
            o_ref[pl.ds(c, n, stride=ROW_SUB), :] = words[:, c * LANES:(c + 1) * LANES]
    else:
        o_ref[...] = words


def pack_table(tbls, layer, *, split_rows, te=512):
    _, E, D = tbls.shape
    out_block, out_shape = ((te * ROW_SUB, LANES), (E * ROW_SUB, LANES)) if split_rows else ((te, D // 2), (E, D // 2))
    return pl.pallas_call(
        functools.partial(_pack_kernel, split_rows=split_rows),
        grid=(E // te,),
        in_specs=[pl.BlockSpec((None, te, D), lambda i: (layer, i, 0))],
        out_specs=pl.BlockSpec(out_block, lambda i: (i, 0)),
        out_shape=jax.ShapeDtypeStruct(out_shape, jnp.uint32),
        compiler_params=_cparams(("parallel",)),
        name="pack_table",
    )(tbls)


def _unpack(words):
    lo = pltpu.bitcast(words << 16, jnp.float32)
    hi = pltpu.bitcast(words & jnp.uint32(0xFFFF0000), jnp.float32)
    return lo, hi


def _gather_rows(idx_ref, tbl_ref, t, planes_ref):
    for k in range(N_SLOTS):
        off = pl.multiple_of(idx_ref[t, k], ROW_SUB)
        planes_ref[pl.ds(k, ROW_SUB, stride=PLANE_STRIDE), :] = tbl_ref[pl.ds(off, ROW_SUB), :]


def _plane(planes_ref, c):
    return _unpack(planes_ref[c * PLANE_STRIDE:c * PLANE_STRIDE + N_SLOTS, :])


def _peer_down_kernel(idx_ref, x_ref, gate_ref, tbl_ref, out_ref, planes_a, planes_b, part_ref, o_ref):
    tb = x_ref.shape[0]
    lane = lax.broadcasted_iota(jnp.int32, (N_SLOTS, tb), 1)

    def compute(t, planes_ref, slot):
        acc = jnp.zeros((N_SLOTS, LANES), jnp.float32)
        for c in range(ROW_SUB):
            lo, hi = _plane(planes_ref, c)
            acc = acc + lo * x_ref[t, c:c + 1, :] + hi * x_ref[t, ROW_SUB + c:ROW_SUB + c + 1, :]
        part_ref[slot] = acc

    def place_pair(t0):
        col_a = jnp.sum(part_ref[0], axis=1, keepdims=True)
        col_b = jnp.sum(part_ref[1], axis=1, keepdims=True)
        o_ref[...] = jnp.where(lane == t0 - 2, col_a, jnp.where(lane == t0 - 1, col_b, o_ref[...]))

    o_ref[...] = jnp.zeros(o_ref.shape, jnp.float32)
    part_ref[...] = jnp.zeros(part_ref.shape, jnp.float32)
    _gather_rows(idx_ref, tbl_ref, 0, planes_a)

    def pair(i, carry):
        t0 = 2 * i
        place_pair(t0)
        _gather_rows(idx_ref, tbl_ref, t0 + 1, planes_b)
        compute(t0, planes_a, 0)
        _gather_rows(idx_ref, tbl_ref, jnp.minimum(t0 + 2, tb - 1), planes_a)
        compute(t0 + 1, planes_b, 1)
        return carry

    lax.fori_loop(0, tb // 2, pair, 0)
    place_pair(tb)
    out_ref[...] = (_gelu(o_ref[...]) * gate_ref[...]).T


def peer_down(idx, xn3, gate_t, tbl, *, tb=128):
    T = xn3.shape[0]
    tb = min(tb, T)
    assert tb % 2 == 0 and T % tb == 0
    planes = pltpu.VMEM((ROW_SUB * PLANE_STRIDE, LANES), jnp.uint32)
    return pl.pallas_call(
        _peer_down_kernel,
        grid=(T // tb,),
        in_specs=[pl.BlockSpec((tb, N_SLOTS), lambda i: (i, 0), memory_space=pltpu.SMEM),
                  pl.BlockSpec((tb, SUBLANES, LANES), lambda i: (i, 0, 0)),
                  pl.BlockSpec((N_SLOTS, tb), lambda i: (0, i)),
                  pl.BlockSpec(tbl.shape, lambda i: (0, 0), pipeline_mode=pl.Buffered(1))],
        out_specs=pl.BlockSpec((tb, N_SLOTS), lambda i: (i, 0)),
        out_shape=jax.ShapeDtypeStruct((T, N_SLOTS), jnp.float32),
        scratch_shapes=[planes, planes, pltpu.VMEM((2, N_SLOTS, LANES), jnp.float32),
                        pltpu.VMEM((N_SLOTS, tb), jnp.float32)],
        compiler_params=_cparams(("arbitrary",)),
        name="peer_down",
    )(idx, xn3, gate_t, tbl)


SC_CORES = 2
SC_SUBCORES = 16
SC_WORKERS = SC_CORES * SC_SUBCORES
SC_LANES = 16
SC_GATHER_ROWS = N_SLOTS // 2
SC_ROW_BLOCK = 8
SC_TOKEN_BLOCK = 32
SC_COL_GROUP = 8


def _sc_params():
    cp = pltpu.CompilerParams()
    if "needs_layout_passes" in pltpu.CompilerParams.__dataclass_fields__:
        cp = dataclasses.replace(cp, needs_layout_passes=False)
    return cp


def peer_up_sc(ids, w, tbl):
    T = ids.shape[0]
    L, CH, RB, TBK = SC_LANES, SC_GATHER_ROWS, SC_ROW_BLOCK, SC_TOKEN_BLOCK
    assert T % (SC_WORKERS * TBK) == 0
    tpw = T // SC_WORKERS
    mesh = plsc.VectorSubcoreMesh(core_axis_name="c", subcore_axis_name="s")
    rows_buf = pltpu.VMEM((CH, ROW_WORDS), jnp.uint32)
    y_buf = pltpu.VMEM((D_MODEL,), jnp.float32)

    @functools.partial(
        pl.kernel, mesh=mesh, compiler_params=_sc_params(),
        out_type=jax.ShapeDtypeStruct((T, D_MODEL), jnp.float32),
        scratch_types=[pltpu.VMEM((TBK, N_SLOTS), jnp.int32), pltpu.VMEM((TBK, N_SLOTS), jnp.float32),
                       rows_buf, rows_buf, y_buf, y_buf] + [pltpu.SemaphoreType.DMA] * 4,
    )
    def up_kernel(ids_hbm, w_hbm, tbl_hbm, out_hbm, ids_v, w_v, rows0, rows1, y0, y1, g0, g1, o0, o1):
        base = (lax.axis_index("s") * SC_CORES + lax.axis_index("c")) * tpw
        rows, gsem, ys, osem = (rows0, rows1), (g0, g1), (y0, y1), (o0, o1)

        def gather(tl, half):
            return pltpu.make_async_copy(tbl_hbm.at[ids_v.at[tl, pl.ds(half * CH, CH)]], rows[half], gsem[half])

        def out_copy(tok, slot):
            return pltpu.make_async_copy(ys[slot], out_hbm.at[tok], osem[slot])

        def accumulate(tl, half, y_v):
            rows_v = rows[half]

            def block(b, carry):
                r0 = b * RB
                tlv = jnp.full((L,), tl, jnp.int32)
                wks = [plsc.load_gather(w_v, [tlv, jnp.full((L,), half * CH + r0 + r, jnp.int32)])
                       for r in range(RB)]
                for q in range(ROW_WORDS // (SC_COL_GROUP * L)):
                    acc = [None] * (2 * SC_COL_GROUP)
                    for r in range(RB):
                        for i in range(SC_COL_GROUP):
                            v = rows_v[r0 + r, pl.ds((q * SC_COL_GROUP + i) * L, L)]
                            lo = plsc.bitcast(v << 16, jnp.float32) * wks[r]
                            hi = plsc.bitcast(v & jnp.uint32(0xFFFF0000), jnp.float32) * wks[r]
                            acc[2 * i] = lo if r == 0 else acc[2 * i] + lo
                            acc[2 * i + 1] = hi if r == 0 else acc[2 * i + 1] + hi
                    for i in range(SC_COL_GROUP):
                        col = (q * SC_COL_GROUP + i) * L
                        plsc.addupdate(y_v.at[pl.ds(col, L)], acc[2 * i])
                        plsc.addupdate(y_v.at[pl.ds(ROW_WORDS + col, L)], acc[2 * i + 1])
                return carry

            lax.fori_loop(0, CH // RB, block, 0)

        def token_block(bi, carry):
            tok0 = base + bi * TBK
            pltpu.sync_copy(ids_hbm.at[pl.ds(tok0, TBK)], ids_v)
            pltpu.sync_copy(w_hbm.at[pl.ds(tok0, TBK)], w_v)
            gather(0, 0).start()

            def token_pair(pi, carry2):
                for slot in range(2):
                    tl = 2 * pi + slot
                    y_v = ys[slot]

                    @pl.when(pi > 0)
                    def _():
                        out_copy(tok0 + tl - 2, slot).wait()

                    zero = jnp.zeros((L,), jnp.float32)
                    for j in range(D_MODEL // L):
                        y_v[pl.ds(j * L, L)] = zero
                    gather(tl, 1).start()
                    gather(tl, 0).wait()
                    accumulate(tl, 0, y_v)

                    @pl.when(tl + 1 < TBK)
                    def _():
                        gather(tl + 1, 0).start()

                    gather(tl, 1).wait()
                    accumulate(tl, 1, y_v)
                    out_copy(tok0 + tl, slot).start()
                return carry2

            lax.fori_loop(0, TBK // 2, token_pair, 0)
            out_copy(tok0 + TBK - 2, 0).wait()
            out_copy(tok0 + TBK - 1, 1).wait()
            return carry

        lax.fori_loop(0, tpw // TBK, token_block, 0)

    return up_kernel(ids, w, tbl)


def peer_layer(h, g, w_query, sub_keys, down_rows, up_rows):
    T = h.shape[0]
    q, xn = norm_matmul(h, g, w_query, out_dtype=jnp.float32, emit_xn=True)
    ids, offsets, gate_t = peer_route(q, sub_keys, precision=lax.Precision.HIGHEST)
    w = peer_down(offsets, xn.reshape(T, SUBLANES, LANES), gate_t, down_rows)
    return peer_up_sc(ids, w, up_rows)


def _rms_kernel(x_ref, r_ref, g_ref, o_ref):
    x = x_ref[...] + r_ref[...]
    o_ref[...] = x * lax.rsqrt(jnp.mean(x * x, axis=-1, keepdims=True) + NORM_EPS) * g_ref[...]


def rms_norm_sum(x, r, g, *, tm=512):
    T, D = x.shape
    tm = min(tm, T)
    row = pl.BlockSpec((tm, D), lambda i: (i, 0))
    return pl.pallas_call(
        _rms_kernel,
        grid=(T // tm,),
        in_specs=[row, row, pl.BlockSpec((1, D), lambda i: (0, 0))],
        out_specs=row,
        out_shape=jax.ShapeDtypeStruct((T, D), jnp.float32),
        compiler_params=_cparams(("parallel",)),
        name="final_norm",
    )(x, r, g.reshape(1, D))


PIECE_SEQS = 2


def _piece_bounds(B):
    sizes = [PIECE_SEQS] * (B // PIECE_SEQS) + ([B % PIECE_SEQS] if B % PIECE_SEQS else [])
    if sizes[-1] > 1:
        sizes[-1:] = [sizes[-1] - sizes[-1] // 2, sizes[-1] // 2]
    starts = [sum(sizes[:i]) for i in range(len(sizes))]
    return list(zip(starts, sizes))


def kernel(x, rel_table, mix_norm_g, attn_w_in, attn_w_out, attn_sink, sg_w_in, sg_ln_g, sg_ln_b, sg_w_spatial, sg_b_spatial, sg_w_out, ffn_norm_g, peer_w_query, peer_sub_keys, peer_down, peer_up, final_norm_g):
    B, S, D = x.shape
    bf16 = jnp.bfloat16

    bias = attn_bias(rel_table)
    attn_in, attn_out = attn_w_in[0].astype(bf16), attn_w_out[0].astype(bf16)
    sg_in, sg_out = sg_w_in[0].astype(bf16), sg_w_out[0].astype(bf16)
    w_query = peer_w_query.astype(bf16)
    down_rows = [pack_table(peer_down, i, split_rows=True) for i in range(2)]
    up_rows = [pack_table(peer_up, i, split_rows=False) for i in range(2)]

    mid = []
    for b0, nb in _piece_bounds(B):
        h = x[b0:b0 + nb].reshape(nb * S, D)
        qkv, = norm_matmul(h, mix_norm_g[0], attn_in, out_dtype=bf16)
        att = window_attention(qkv.reshape(nb, S, QKV_DIM), bias, attn_sink[0])
        h = matmul_residual(att.reshape(nb * S, Q_DIM), attn_out, h)
        mid.append((h, peer_layer(h, ffn_norm_g[0], w_query[0], peer_sub_keys[0], down_rows[0], up_rows[0])))

    outs = []
    for h, y in mid:
        z, h = norm_matmul(h, mix_norm_g[1], sg_in, res=y, out_dtype=bf16, act=True, tn=1536)
        gated = spatial_gate(z, sg_ln_g[0], sg_ln_b[0], sg_w_spatial[0], sg_b_spatial[0])
        h = matmul_residual(gated, sg_out, h)
        y = peer_layer(h, ffn_norm_g[1], w_query[1], peer_sub_keys[1], down_rows[1], up_rows[1])
        outs.append(rms_norm_sum(h, y, final_norm_g).reshape(-1, S, D))
    return jnp.concatenate(outs, axis=0)
```

```python
import dataclasses
import functools
import math

import jax
import jax.numpy as jnp
from jax import lax
from jax.experimental import pallas as pl
from jax.experimental.pallas import tpu as pltpu
from jax.experimental.pallas import tpu_sc as plsc

D_MODEL = 1024
HEAD_DIM = 64
N_Q_HEADS = 16
N_KV_HEADS = 4
GQA_GROUP = 4
WINDOW = 128
BLOCK = 128
REL_BUCKETS = 32
REL_MAX_DIST = 128
Q_DIM = N_Q_HEADS * HEAD_DIM
KV_DIM = N_KV_HEADS * HEAD_DIM
QKV_DIM = Q_DIM + 2 * KV_DIM
CHUNK = 128
D_GATE = 3072
N_SG_GROUPS = 8
SG_GROUP_DIM = D_GATE // N_SG_GROUPS
N_KEYS = 128
PEER_HEADS = 8
PEER_TOPK = 16
D_HALF = 128
N_SLOTS = PEER_HEADS * PEER_TOPK
NORM_EPS = 1e-6
LN_EPS = 1e-5

LANES = 128
SUBLANES = 8
ROW_WORDS = D_MODEL // 2
ROW_SUB = ROW_WORDS // LANES
PLANE_STRIDE = 136
VMEM_LIMIT = 56 * 1024 * 1024

_GELU_C = math.sqrt(2.0 / math.pi)


def _gelu(x):
    return 0.5 * x * (1.0 + jnp.tanh(_GELU_C * (x + 0.044715 * (x * x * x))))


def _cparams(sem):
    return pltpu.CompilerParams(dimension_semantics=sem, vmem_limit_bytes=VMEM_LIMIT)


def _norm_matmul_kernel(*refs, act, emit_xn, add_res, precision):
    refs = list(refs)
    x_ref = refs.pop(0)
    r_ref = refs.pop(0) if add_res else None
    g_ref, w_ref, o_ref = refs.pop(0), refs.pop(0), refs.pop(0)
    xn_ref = refs.pop(0) if emit_xn else None
    sum_ref = refs.pop(0) if add_res else None
    xs_ref = refs.pop(0)

    @pl.when(pl.program_id(1) == 0)
    def _():
        x = x_ref[...]
        if add_res:
            x = x + r_ref[...]
            sum_ref[...] = x
        y = x * lax.rsqrt(jnp.mean(x * x, axis=-1, keepdims=True) + NORM_EPS) * g_ref[...]
        xs_ref[...] = y.astype(xs_ref.dtype)
        if emit_xn:
            xn_ref[...] = y

    acc = jnp.dot(xs_ref[...], w_ref[...], preferred_element_type=jnp.float32,
                  precision=precision)
    if act:
        acc = _gelu(acc)
    o_ref[...] = acc.astype(o_ref.dtype)


def norm_matmul(x, g, w, *, out_dtype, res=None, act=False, emit_xn=False, tm=512, tn=None,
                precision=None):
    T, D = x.shape
    N = w.shape[1]
    tn = tn or N
    tm = min(tm, T)
    row = pl.BlockSpec((tm, D), lambda i, j: (i, 0))
    out_shape = [jax.ShapeDtypeStruct((T, N), out_dtype)]
    out_specs = [pl.BlockSpec((tm, tn), lambda i, j: (i, j))]
    for flag in (emit_xn, res is not None):
        if flag:
            out_shape.append(jax.ShapeDtypeStruct((T, D), jnp.float32))
            out_specs.append(row)
    args = [x] + ([res] if res is not None else []) + [g.reshape(1, D), w]
    in_specs = [row] * (len(args) - 2) + [pl.BlockSpec((1, D), lambda i, j: (0, 0)),
                                          pl.BlockSpec((D, tn), lambda i, j: (0, j))]
    return pl.pallas_call(
        functools.partial(_norm_matmul_kernel, act=act, emit_xn=emit_xn, add_res=res is not None,
                          precision=precision),
        grid=(T // tm, N // tn),
        in_specs=in_specs,
        out_specs=out_specs,
        out_shape=out_shape,
        scratch_shapes=[pltpu.VMEM((tm, D), w.dtype)],
        compiler_params=_cparams(("parallel", "arbitrary")),
        name="norm_matmul",
    )(*args)


def _matmul_res_kernel(a_ref, w_ref, h_ref, o_ref):
    o_ref[...] = h_ref[...] + jnp.dot(a_ref[...], w_ref[...],
                                      preferred_element_type=jnp.float32)


def matmul_residual(a, w, h, *, tm=512):
    T, K = a.shape
    N = w.shape[1]
    tm = min(tm, T)
    return pl.pallas_call(
        _matmul_res_kernel,
        grid=(T // tm,),
        in_specs=[pl.BlockSpec((tm, K), lambda i: (i, 0)),
                  pl.BlockSpec((K, N), lambda i: (0, 0)),
                  pl.BlockSpec((tm, N), lambda i: (i, 0))],
        out_specs=pl.BlockSpec((tm, N), lambda i: (i, 0)),
        out_shape=jax.ShapeDtypeStruct((T, N), jnp.float32),
        compiler_params=_cparams(("parallel",)),
        name="matmul_residual",
    )(a, w, h)


def _t5_bucket(rel):
    nb = REL_BUCKETS // 2
    max_exact = nb // 2
    ret = jnp.where(rel > 0, nb, 0)
    n = jnp.abs(rel)
    nf = jnp.maximum(n, 1).astype(jnp.float32)
    large = max_exact + (jnp.log(nf / max_exact) / math.log(REL_MAX_DIST / max_exact)
                         * (nb - max_exact)).astype(jnp.int32)
    large = jnp.minimum(large, nb - 1)
    return (ret + jnp.where(n < max_exact, n, large)).astype(jnp.int32)


def _bias_kernel(bucket_ref, window_ref, table_ref, o_ref):
    bucket = bucket_ref[...]
    in_window = window_ref[...] > 0
    for hq in range(N_Q_HEADS):
        acc = jnp.zeros(bucket.shape, jnp.float32)
        for b in range(REL_BUCKETS):
            acc = jnp.where(bucket == b, table_ref[b, hq], acc)
        o_ref[hq] = jnp.where(in_window, acc, -jnp.inf)


def attn_bias(rel_table):
    qi = jnp.arange(BLOCK)[:, None]
    kj = jnp.arange(3 * BLOCK)[None, :]
    rel = kj - BLOCK - qi
    bucket = _t5_bucket(rel)
    window = (jnp.abs(rel) <= WINDOW).astype(jnp.int32)
    return pl.pallas_call(
        _bias_kernel,
        in_specs=[pl.BlockSpec(memory_space=pltpu.VMEM),
                  pl.BlockSpec(memory_space=pltpu.VMEM),
                  pl.BlockSpec(memory_space=pltpu.SMEM)],
        out_specs=pl.BlockSpec(memory_space=pltpu.VMEM),
        out_shape=jax.ShapeDtypeStruct((N_Q_HEADS, BLOCK, 3 * BLOCK), jnp.float32),
        name="attn_bias",
    )(bucket, window, rel_table)


def _attn_kernel(cur_ref, prev_ref, next_ref, bias_ref, sink_ref, o_ref):
    i = pl.program_id(1)
    nb = pl.num_programs(1)
    q = cur_ref[0, :, 0:Q_DIM] * (HEAD_DIM ** -0.5)
    kband = jnp.concatenate([prev_ref[0, :, 0:KV_DIM], cur_ref[0, :, Q_DIM:Q_DIM + KV_DIM],
                             next_ref[0, :, 0:KV_DIM]], axis=0)
    vband = jnp.concatenate([prev_ref[0, :, KV_DIM:2 * KV_DIM], cur_ref[0, :, Q_DIM + KV_DIM:QKV_DIM],
                             next_ref[0, :, KV_DIM:2 * KV_DIM]], axis=0)
    col = lax.broadcasted_iota(jnp.int32, (1, 3 * BLOCK), 1)
    valid = jnp.logical_and(jnp.logical_or(col >= BLOCK, i > 0),
                            jnp.logical_or(col < 2 * BLOCK, i < nb - 1))
    heads = range(N_Q_HEADS)
    kv = [(kband[:, hk * HEAD_DIM:(hk + 1) * HEAD_DIM], vband[:, hk * HEAD_DIM:(hk + 1) * HEAD_DIM])
          for hk in range(N_KV_HEADS)]
    s = []
    for hq in heads:
        qh = q[:, hq * HEAD_DIM:(hq + 1) * HEAD_DIM]
        sc = lax.dot_general(qh, kv[hq // GQA_GROUP][0], (((1,), (1,)), ((), ())),
                             preferred_element_type=jnp.float32)
        s.append(jnp.where(valid, sc + bias_ref[hq], -jnp.inf))
    mx = [jnp.maximum(jnp.max(s[hq], axis=-1, keepdims=True), sink_ref[hq]) for hq in heads]
    p = [jnp.exp(s[hq] - mx[hq]) for hq in heads]
    denom = [jnp.sum(p[hq], axis=-1, keepdims=True) + jnp.exp(sink_ref[hq] - mx[hq]) for hq in heads]
    outs = [jnp.dot((p[hq] / denom[hq]).astype(jnp.bfloat16), kv[hq // GQA_GROUP][1],
                    preferred_element_type=jnp.float32) for hq in heads]
    o_ref[0] = jnp.concatenate(outs, axis=-1).astype(o_ref.dtype)


def window_attention(qkv, bias, sink):
    B, S, _ = qkv.shape
    nb = S // BLOCK
    kv_col = Q_DIM // (2 * KV_DIM)
    return pl.pallas_call(
        _attn_kernel,
        grid=(B, nb),
        in_specs=[pl.BlockSpec((1, BLOCK, QKV_DIM), lambda b, i: (b, i, 0)),
                  pl.BlockSpec((1, BLOCK, 2 * KV_DIM),
                               lambda b, i: (b, jnp.maximum(i - 1, 0), kv_col)),
                  pl.BlockSpec((1, BLOCK, 2 * KV_DIM),
                               lambda b, i: (b, jnp.minimum(i + 1, nb - 1), kv_col)),
                  pl.BlockSpec((N_Q_HEADS, BLOCK, 3 * BLOCK), lambda b, i: (0, 0, 0)),
                  pl.BlockSpec(memory_space=pltpu.SMEM)],
        out_specs=pl.BlockSpec((1, BLOCK, Q_DIM), lambda b, i: (b, i, 0)),
        out_shape=jax.ShapeDtypeStruct((B, S, Q_DIM), jnp.bfloat16),
        compiler_params=_cparams(("parallel", "arbitrary")),
        name="window_attention",
    )(qkv, qkv, qkv, bias, sink)


def _spatial_gate_kernel(z_ref, g_ref, b_ref, wsp_ref, bsp_ref, o_ref):
    v = z_ref[:, D_GATE:2 * D_GATE].astype(jnp.float32)
    mu = jnp.mean(v, axis=-1, keepdims=True)
    vc = v - mu
    var = jnp.mean(vc * vc, axis=-1, keepdims=True)
    vn = (vc * lax.rsqrt(var + LN_EPS) * g_ref[...] + b_ref[...]).astype(jnp.bfloat16)
    for grp in range(N_SG_GROUPS):
        lo, hi = grp * SG_GROUP_DIM, (grp + 1) * SG_GROUP_DIM
        mixed = jnp.dot(wsp_ref[grp], vn[:, lo:hi], preferred_element_type=jnp.float32)
        mixed = mixed + bsp_ref[:, grp:grp + 1]
        u = z_ref[:, lo:hi].astype(jnp.float32)
        o_ref[:, lo:hi] = (u * mixed).astype(o_ref.dtype)


def spatial_gate(z, ln_g, ln_b, w_sp, b_sp):
    T = z.shape[0]
    return pl.pallas_call(
        _spatial_gate_kernel,
        grid=(T // CHUNK,),
        in_specs=[pl.BlockSpec((CHUNK, 2 * D_GATE), lambda i: (i, 0)),
                  pl.BlockSpec((1, D_GATE), lambda i: (0, 0)),
                  pl.BlockSpec((1, D_GATE), lambda i: (0, 0)),
                  pl.BlockSpec((N_SG_GROUPS, CHUNK, CHUNK), lambda i: (0, 0, 0)),
                  pl.BlockSpec((CHUNK, N_SG_GROUPS), lambda i: (0, 0))],
        out_specs=pl.BlockSpec((CHUNK, D_GATE), lambda i: (i, 0)),
        out_shape=jax.ShapeDtypeStruct((T, D_GATE), jnp.bfloat16),
        compiler_params=_cparams(("parallel",)),
        name="spatial_gate",
    )(z, ln_g.reshape(1, D_GATE), ln_b.reshape(1, D_GATE), w_sp.astype(jnp.bfloat16), b_sp.T)


def _oddeven_merge_sort_pairs(n):
    pairs = []
    p = 1
    while p < n:
        k = p
        while k >= 1:
            for j in range(k % p, n - k, 2 * k):
                for i in range(min(k, n - j - k)):
                    if (i + j) // (2 * p) == (i + j + k) // (2 * p):
                        pairs.append((i + j, i + j + k))
            k //= 2
        p *= 2
    return pairs


_SORT16 = _oddeven_merge_sort_pairs(N_KEYS // SUBLANES)


def _top16_of_keys(s):
    nv = N_KEYS // SUBLANES
    L = s.shape[1]
    sub = lax.broadcasted_iota(jnp.int32, (SUBLANES, L), 0)
    v = [s[j * SUBLANES:(j + 1) * SUBLANES] for j in range(nv)]
    ids = [sub + j * SUBLANES for j in range(nv)]
    for i, j in _SORT16:
        swap = jnp.logical_or(v[j] > v[i], jnp.logical_and(v[j] == v[i], ids[j] < ids[i]))
        v[i], v[j] = jnp.where(swap, v[j], v[i]), jnp.where(swap, v[i], v[j])
        ids[i], ids[j] = jnp.where(swap, ids[j], ids[i]), jnp.where(swap, ids[i], ids[j])
    vals, picks = [], []
    for it in range(PEER_TOPK):
        m = jnp.max(v[0], axis=0, keepdims=True)
        am = jnp.min(jnp.where(v[0] == m, ids[0], N_KEYS), axis=0, keepdims=True)
        hit = ids[0] == am
        vals.append(m)
        picks.append(am)
        last = PEER_TOPK - 1 - it
        for j in range(last):
            v[j] = jnp.where(hit, v[j + 1], v[j])
            ids[j] = jnp.where(hit, ids[j + 1], ids[j])
        v[last] = jnp.where(hit, -jnp.inf, v[last])
    return jnp.concatenate(vals, axis=0), jnp.concatenate(picks, axis=0)


_CAND_ROWS = (
    [(a, 0) for a in range(16)]
    + [None] + [(0, b) for b in range(1, 16)]
    + [None] + [(a, 1) for a in range(1, 8)]
    + [None, None] + [(1, b) for b in range(2, 8)]
    + [(2, 2), (3, 2), (4, 2), (2, 3), (2, 4), (3, 3), None, None]
)
assert sorted(x for x in _CAND_ROWS if x) == sorted(
    (a, b) for a in range(16) for b in range(16) if (a + 1) * (b + 1) <= 16)


def _rows(x, picks):
    pieces, i = [], 0
    while i < len(picks):
        j = i
        while j + 1 < len(picks) and picks[j + 1] == picks[j] + 1:
            j += 1
        pieces.append(x[picks[i]:picks[j] + 1])
        i = j + 1
    return pieces[0] if len(pieces) == 1 else jnp.concatenate(pieces, axis=0)


def _cand_positions(L):
    row = lax.broadcasted_iota(jnp.int32, (len(_CAND_ROWS), L), 0)
    pos = jnp.full(row.shape, _PAD_POS, jnp.int32)
    for r, c in enumerate(_CAND_ROWS):
        if c is not None:
            pos = jnp.where(row == r, c[0] * PEER_TOPK + c[1], pos)
    return pos


_PAD_POS = 1 << 20


def _joint_top16(v1, i1, v2, i2, pos):
    a_of = [c[0] if c else 0 for c in _CAND_ROWS]
    b_of = [c[1] if c else 0 for c in _CAND_ROWS]
    cand = jnp.where(pos < _PAD_POS, _rows(v1, a_of) + _rows(v2, b_of), -jnp.inf)
    cidx = _rows(i1, a_of) * N_KEYS + _rows(i2, b_of)
    vals, picks = [], []
    for _ in range(PEER_TOPK):
        m = jnp.max(cand, axis=0, keepdims=True)
        pm = jnp.min(jnp.where(cand == m, pos, 1 << 21), axis=0, keepdims=True)
        hit = pos == pm
        vals.append(m)
        picks.append(jnp.max(jnp.where(hit, cidx, -1), axis=0, keepdims=True))
        cand = jnp.where(hit, -jnp.inf, cand)
    return jnp.concatenate(vals, axis=0), jnp.concatenate(picks, axis=0)


def _route_kernel(q_ref, keys_ref, idx_ref, gate_ref, *, precision):
    nt = (((1,), (1,)), ((), ()))
    pos = _cand_positions(LANES)
    for j in range(q_ref.shape[0] // LANES):
        q = q_ref[j * LANES:(j + 1) * LANES, :]
        s1 = lax.dot_general(keys_ref[0, 0], q[:, 0:D_HALF], nt,
                             preferred_element_type=jnp.float32, precision=precision)
        s2 = lax.dot_general(keys_ref[0, 1], q[:, D_HALF:2 * D_HALF], nt,
                             preferred_element_type=jnp.float32, precision=precision)
        v1, i1 = _top16_of_keys(s1)
        v2, i2 = _top16_of_keys(s2)
        top_s, top_i = _joint_top16(v1, i1, v2, i2, pos)
        e = jnp.exp(top_s - top_s[0:1])
        gate_ref[:, j * LANES:(j + 1) * LANES] = e / jnp.sum(e, axis=0, keepdims=True)
        idx_ref[:, j * LANES:(j + 1) * LANES] = top_i * ROW_SUB


def peer_route(q, sub_keys, *, tb=1024, precision=None):
    T = q.shape[0]
    tb = min(tb, T)
    return pl.pallas_call(
        functools.partial(_route_kernel, precision=precision),
        grid=(T // tb, PEER_HEADS),
        in_specs=[pl.BlockSpec((tb, 2 * D_HALF), lambda i, h: (i, h)),
                  pl.BlockSpec((1, 2, N_KEYS, D_HALF), lambda i, h: (h, 0, 0, 0))],
        out_specs=[pl.BlockSpec((PEER_TOPK, tb), lambda i, h: (h, i)),
                   pl.BlockSpec((PEER_TOPK, tb), lambda i, h: (h, i))],
        out_shape=[jax.ShapeDtypeStruct((N_SLOTS, T), jnp.int32),
                   jax.ShapeDtypeStruct((N_SLOTS, T), jnp.float32)],
        compiler_params=_cparams(("parallel", "arbitrary")),
        name="peer_route",
    )(q, sub_keys)


def _pack_kernel(x_ref, o_ref, *, split_rows):
    x = x_ref[...]
    half = x.shape[1] // 2

    def bf16_bits(v):
        return pltpu.bitcast(v.astype(jnp.bfloat16).astype(jnp.float32), jnp.uint32)

    words = (bf16_bits(x[:, :half]) >> 16) | (bf16_bits(x[:, half:]) & jnp.uint32(0xFFFF0000))
    if split_rows:
        n = x.shape[0]
        for c in range(ROW_SUB):
            o_ref[pl.ds(c, n, stride=ROW_SUB), :] = words[:, c * LANES:(c + 1) * LANES]
    else:
        o_ref[...] = words


def pack_table(tbls, layer, *, split_rows, te=512):
    _, E, D = tbls.shape
    out_block, out_shape = ((te * ROW_SUB, LANES), (E * ROW_SUB, LANES)) if split_rows else ((te, D // 2), (E, D // 2))
    return pl.pallas_call(
        functools.partial(_pack_kernel, split_rows=split_rows),
        grid=(E // te,),
        in_specs=[pl.BlockSpec((None, te, D), lambda i: (layer, i, 0))],
        out_specs=pl.BlockSpec(out_block, lambda i: (i, 0)),
        out_shape=jax.ShapeDtypeStruct(out_shape, jnp.uint32),
        compiler_params=_cparams(("parallel",)),
        name="pack_table",
    )(tbls)


def _unpack(words):
    lo = pltpu.bitcast(words << 16, jnp.float32)
    hi = pltpu.bitcast(words & jnp.uint32(0xFFFF0000), jnp.float32)
    return lo, hi


def _gather_rows(idx_ref, tbl_ref, t, planes_ref):
    for k in range(N_SLOTS):
        off = pl.multiple_of(idx_ref[t, k], ROW_SUB)
        planes_ref[pl.ds(k, ROW_SUB, stride=PLANE_STRIDE), :] = tbl_ref[pl.ds(off, ROW_SUB), :]


def _plane(planes_ref, c):
    return _unpack(planes_ref[c * PLANE_STRIDE:c * PLANE_STRIDE + N_SLOTS, :])


def _peer_down_kernel(idx_ref, x_ref, gate_ref, tbl_ref, o_ref, planes_a, planes_b, part_ref):
    tb = x_ref.shape[0]
    lane = lax.broadcasted_iota(jnp.int32, (N_SLOTS, tb), 1)

    def compute(t, planes_ref, slot):
        acc = jnp.zeros((N_SLOTS, LANES), jnp.float32)
        for c in range(ROW_SUB):
            lo, hi = _plane(planes_ref, c)
            acc = acc + lo * x_ref[t, c:c + 1, :] + hi * x_ref[t, ROW_SUB + c:ROW_SUB + c + 1, :]
        part_ref[slot] = acc

    def place_pair(t0):
        col_a = jnp.sum(part_ref[0], axis=1, keepdims=True)
        col_b = jnp.sum(part_ref[1], axis=1, keepdims=True)
        o_ref[...] = jnp.where(lane == t0 - 2, col_a, jnp.where(lane == t0 - 1, col_b, o_ref[...]))

    o_ref[...] = jnp.zeros(o_ref.shape, jnp.float32)
    part_ref[...] = jnp.zeros(part_ref.shape, jnp.float32)
    _gather_rows(idx_ref, tbl_ref, 0, planes_a)

    def pair(i, carry):
        t0 = 2 * i
        place_pair(t0)
        _gather_rows(idx_ref, tbl_ref, t0 + 1, planes_b)
        compute(t0, planes_a, 0)
        _gather_rows(idx_ref, tbl_ref, jnp.minimum(t0 + 2, tb - 1), planes_a)
        compute(t0 + 1, planes_b, 1)
        return carry

    lax.fori_loop(0, tb // 2, pair, 0)
    place_pair(tb)
    o_ref[...] = _gelu(o_ref[...]) * gate_ref[...]


def peer_down(idx, xn3, gate_t, tbl, *, tb=256):
    T = xn3.shape[0]
    tb = min(tb, T)
    assert tb % 2 == 0 and T % tb == 0
    planes = pltpu.VMEM((ROW_SUB * PLANE_STRIDE, LANES), jnp.uint32)
    return pl.pallas_call(
        _peer_down_kernel,
        grid=(T // tb,),
        in_specs=[pl.BlockSpec((tb, N_SLOTS), lambda i: (i, 0), memory_space=pltpu.SMEM),
                  pl.BlockSpec((tb, SUBLANES, LANES), lambda i: (i, 0, 0)),
                  pl.BlockSpec((N_SLOTS, tb), lambda i: (0, i)),
                  pl.BlockSpec(tbl.shape, lambda i: (0, 0), pipeline_mode=pl.Buffered(1))],
        out_specs=pl.BlockSpec((N_SLOTS, tb), lambda i: (0, i)),
        out_shape=jax.ShapeDtypeStruct((N_SLOTS, T), jnp.float32),
        scratch_shapes=[planes, planes, pltpu.VMEM((2, N_SLOTS, LANES), jnp.float32)],
        compiler_params=_cparams(("arbitrary",)),
        name="peer_down",
    )(idx, xn3, gate_t, tbl)


SC_CORES = 2
SC_SUBCORES = 16
SC_WORKERS = SC_CORES * SC_SUBCORES
SC_LANES = 16
SC_GATHER_ROWS = N_SLOTS // 2
SC_ROW_BLOCK = 8
SC_TOKEN_BLOCK = 32
SC_COL_GROUP = 8


def _sc_params():
    cp = pltpu.CompilerParams()
    if "needs_layout_passes" in pltpu.CompilerParams.__dataclass_fields__:
        cp = dataclasses.replace(cp, needs_layout_passes=False)
    return cp


def peer_up_sc(ids, w, tbl):
    T = ids.shape[0]
    L, CH, RB, TBK = SC_LANES, SC_GATHER_ROWS, SC_ROW_BLOCK, SC_TOKEN_BLOCK
    assert T % (SC_WORKERS * TBK) == 0
    tpw = T // SC_WORKERS
    mesh = plsc.VectorSubcoreMesh(core_axis_name="c", subcore_axis_name="s")
    rows_buf = pltpu.VMEM((CH, ROW_WORDS), jnp.uint32)
    y_buf = pltpu.VMEM((D_MODEL,), jnp.float32)

    @functools.partial(
        pl.kernel, mesh=mesh, compiler_params=_sc_params(),
        out_type=jax.ShapeDtypeStruct((T, D_MODEL), jnp.float32),
        scratch_types=[pltpu.VMEM((TBK, N_SLOTS), jnp.int32), pltpu.VMEM((TBK, N_SLOTS), jnp.float32),
                       rows_buf, rows_buf, y_buf, y_buf] + [pltpu.SemaphoreType.DMA] * 4,
    )
    def up_kernel(ids_hbm, w_hbm, tbl_hbm, out_hbm, ids_v, w_v, rows0, rows1, y0, y1, g0, g1, o0, o1):
        base = (lax.axis_index("s") * SC_CORES + lax.axis_index("c")) * tpw
        rows, gsem, ys, osem = (rows0, rows1), (g0, g1), (y0, y1), (o0, o1)

        def gather(tl, half):
            return pltpu.make_async_copy(tbl_hbm.at[ids_v.at[tl, pl.ds(half * CH, CH)]], rows[half], gsem[half])

        def out_copy(tok, slot):
            return pltpu.make_async_copy(ys[slot], out_hbm.at[tok], osem[slot])

        def accumulate(tl, half, y_v):
            rows_v = rows[half]

            def block(b, carry):
                r0 = b * RB
                tlv = jnp.full((L,), tl, jnp.int32)
                wks = [plsc.load_gather(w_v, [tlv, jnp.full((L,), half * CH + r0 + r, jnp.int32)])
                       for r in range(RB)]
                for q in range(ROW_WORDS // (SC_COL_GROUP * L)):
                    acc = [None] * (2 * SC_COL_GROUP)
                    for r in range(RB):
                        for i in range(SC_COL_GROUP):
                            v = rows_v[r0 + r, pl.ds((q * SC_COL_GROUP + i) * L, L)]
                            lo = plsc.bitcast(v << 16, jnp.float32) * wks[r]
                            hi = plsc.bitcast(v & jnp.uint32(0xFFFF0000), jnp.float32) * wks[r]
                            acc[2 * i] = lo if r == 0 else acc[2 * i] + lo
                            acc[2 * i + 1] = hi if r == 0 else acc[2 * i + 1] + hi
                    for i in range(SC_COL_GROUP):
                        col = (q * SC_COL_GROUP + i) * L
                        plsc.addupdate(y_v.at[pl.ds(col, L)], acc[2 * i])
                        plsc.addupdate(y_v.at[pl.ds(ROW_WORDS + col, L)], acc[2 * i + 1])
                return carry

            lax.fori_loop(0, CH // RB, block, 0)

        def token_block(bi, carry):
            tok0 = base + bi * TBK
            pltpu.sync_copy(ids_hbm.at[pl.ds(tok0, TBK)], ids_v)
            pltpu.sync_copy(w_hbm.at[pl.ds(tok0, TBK)], w_v)
            gather(0, 0).start()

            def token_pair(pi, carry2):
                for slot in range(2):
                    tl = 2 * pi + slot
                    y_v = ys[slot]

                    @pl.when(pi > 0)
                    def _():
                        out_copy(tok0 + tl - 2, slot).wait()

                    zero = jnp.zeros((L,), jnp.float32)
                    for j in range(D_MODEL // L):
                        y_v[pl.ds(j * L, L)] = zero
                    gather(tl, 1).start()
                    gather(tl, 0).wait()
                    accumulate(tl, 0, y_v)

                    @pl.when(tl + 1 < TBK)
                    def _():
                        gather(tl + 1, 0).start()

                    gather(tl, 1).wait()
                    accumulate(tl, 1, y_v)
                    out_copy(tok0 + tl, slot).start()
                return carry2

            lax.fori_loop(0, TBK // 2, token_pair, 0)
            out_copy(tok0 + TBK - 2, 0).wait()
            out_copy(tok0 + TBK - 1, 1).wait()
            return carry

        lax.fori_loop(0, tpw // TBK, token_block, 0)

    return up_kernel(ids, w, tbl)


def peer_layer(h, g, w_query, sub_keys, down_rows, up_rows):
    T = h.shape[0]
    q, xn = norm_matmul(h, g, w_query, out_dtype=jnp.float32, emit_xn=True)
    idx_t, gate_t = peer_route(q, sub_keys, precision=lax.Precision.HIGHEST)
    idx = idx_t.T
    w_t = peer_down(idx, xn.reshape(T, SUBLANES, LANES), gate_t, down_rows)
    return peer_up_sc(idx // ROW_SUB, w_t.T, up_rows)


def _rms_kernel(x_ref, r_ref, g_ref, o_ref):
    x = x_ref[...] + r_ref[...]
    o_ref[...] = x * lax.rsqrt(jnp.mean(x * x, axis=-1, keepdims=True) + NORM_EPS) * g_ref[...]


def rms_norm_sum(x, r, g, *, tm=512):
    T, D = x.shape
    tm = min(tm, T)
    row = pl.BlockSpec((tm, D), lambda i: (i, 0))
    return pl.pallas_call(
        _rms_kernel,
        grid=(T // tm,),
        in_specs=[row, row, pl.BlockSpec((1, D), lambda i: (0, 0))],
        out_specs=row,
        out_shape=jax.ShapeDtypeStruct((T, D), jnp.float32),
        compiler_params=_cparams(("parallel",)),
        name="final_norm",
    )(x, r, g.reshape(1, D))


PIECE_SEQS = 2


def _piece_bounds(B):
    sizes = [PIECE_SEQS] * (B // PIECE_SEQS) + ([B % PIECE_SEQS] if B % PIECE_SEQS else [])
    if sizes[-1] > 1:
        sizes[-1:] = [sizes[-1] - sizes[-1] // 2, sizes[-1] // 2]
    starts = [sum(sizes[:i]) for i in range(len(sizes))]
    return list(zip(starts, sizes))


def kernel(x, rel_table, mix_norm_g, attn_w_in, attn_w_out, attn_sink, sg_w_in, sg_ln_g, sg_ln_b, sg_w_spatial, sg_b_spatial, sg_w_out, ffn_norm_g, peer_w_query, peer_sub_keys, peer_down, peer_up, final_norm_g):
    B, S, D = x.shape
    bf16 = jnp.bfloat16

    bias = attn_bias(rel_table)
    attn_in, attn_out = attn_w_in[0].astype(bf16), attn_w_out[0].astype(bf16)
    sg_in, sg_out = sg_w_in[0].astype(bf16), sg_w_out[0].astype(bf16)
    w_query = peer_w_query.astype(bf16)
    down_rows = [pack_table(peer_down, i, split_rows=True) for i in range(2)]
    up_rows = [pack_table(peer_up, i, split_rows=False) for i in range(2)]

    mid = []
    for b0, nb in _piece_bounds(B):
        h = x[b0:b0 + nb].reshape(nb * S, D)
        qkv, = norm_matmul(h, mix_norm_g[0], attn_in, out_dtype=bf16)
        att = window_attention(qkv.reshape(nb, S, QKV_DIM), bias, attn_sink[0])
        h = matmul_residual(att.reshape(nb * S, Q_DIM), attn_out, h)
        mid.append((h, peer_layer(h, ffn_norm_g[0], w_query[0], peer_sub_keys[0], down_rows[0], up_rows[0])))

    outs = []
    for h, y in mid:
        z, h = norm_matmul(h, mix_norm_g[1], sg_in, res=y, out_dtype=bf16, act=True, tn=1536)
        gated = spatial_gate(z, sg_ln_g[0], sg_ln_b[0], sg_w_spatial[0], sg_b_spatial[0])
        h = matmul_residual(gated, sg_out, h)
        y = peer_layer(h, ffn_norm_g[1], w_query[1], peer_sub_keys[1], down_rows[1], up_rows[1])
        outs.append(rms_norm_sum(h, y, final_norm_g).reshape(-1, S, D))
    return jnp.concatenate(outs, axis=0)
```

```python
import dataclasses
import functools
import math

import jax
import jax.numpy as jnp
from jax import lax
from jax.experimental import pallas as pl
from jax.experimental.pallas import tpu as pltpu
from jax.experimental.pallas import tpu_sc as plsc

D_MODEL = 1024
HEAD_DIM = 64
N_Q_HEADS = 16
N_KV_HEADS = 4
GQA_GROUP = 4
WINDOW = 128
BLOCK = 128
REL_BUCKETS = 32
REL_MAX_DIST = 128
Q_DIM = N_Q_HEADS * HEAD_DIM
KV_DIM = N_KV_HEADS * HEAD_DIM
QKV_DIM = Q_DIM + 2 * KV_DIM
CHUNK = 128
D_GATE = 3072
N_SG_GROUPS = 8
SG_GROUP_DIM = D_GATE // N_SG_GROUPS
N_KEYS = 128
PEER_HEADS = 8
PEER_TOPK = 16
D_HALF = 128
N_SLOTS = PEER_HEADS * PEER_TOPK
NORM_EPS = 1e-6
LN_EPS = 1e-5

LANES = 128
SUBLANES = 8
ROW_WORDS = D_MODEL // 2
ROW_SUB = ROW_WORDS // LANES
PLANE_STRIDE = 136
VMEM_LIMIT = 56 * 1024 * 1024

_GELU_C = math.sqrt(2.0 / math.pi)


def _gelu(x):
    return 0.5 * x * (1.0 + jnp.tanh(_GELU_C * (x + 0.044715 * (x * x * x))))


def _cparams(sem):
    return pltpu.CompilerParams(dimension_semantics=sem, vmem_limit_bytes=VMEM_LIMIT)


def _norm_matmul_kernel(*refs, act, emit_xn, add_res, precision):
    refs = list(refs)
    x_ref = refs.pop(0)
    r_ref = refs.pop(0) if add_res else None
    g_ref, w_ref, o_ref = refs.pop(0), refs.pop(0), refs.pop(0)
    xn_ref = refs.pop(0) if emit_xn else None
    sum_ref = refs.pop(0) if add_res else None
    xs_ref = refs.pop(0)

    @pl.when(pl.program_id(1) == 0)
    def _():
        x = x_ref[...]
        if add_res:
            x = x + r_ref[...]
            sum_ref[...] = x
        y = x * lax.rsqrt(jnp.mean(x * x, axis=-1, keepdims=True) + NORM_EPS) * g_ref[...]
        xs_ref[...] = y.astype(xs_ref.dtype)
        if emit_xn:
            xn_ref[...] = y

    acc = jnp.dot(xs_ref[...], w_ref[...], preferred_element_type=jnp.float32,
                  precision=precision)
    if act:
        acc = _gelu(acc)
    o_ref[...] = acc.astype(o_ref.dtype)


def norm_matmul(x, g, w, *, out_dtype, res=None, act=False, emit_xn=False, tm=512, tn=None,
                precision=None):
    T, D = x.shape
    N = w.shape[1]
    tn = tn or N
    tm = min(tm, T)
    row = pl.BlockSpec((tm, D), lambda i, j: (i, 0))
    out_shape = [jax.ShapeDtypeStruct((T, N), out_dtype)]
    out_specs = [pl.BlockSpec((tm, tn), lambda i, j: (i, j))]
    for flag in (emit_xn, res is not None):
        if flag:
            out_shape.append(jax.ShapeDtypeStruct((T, D), jnp.float32))
            out_specs.append(row)
    args = [x] + ([res] if res is not None else []) + [g.reshape(1, D), w]
    in_specs = [row] * (len(args) - 2) + [pl.BlockSpec((1, D), lambda i, j: (0, 0)),
                                          pl.BlockSpec((D, tn), lambda i, j: (0, j))]
    return pl.pallas_call(
        functools.partial(_norm_matmul_kernel, act=act, emit_xn=emit_xn, add_res=res is not None,
                          precision=precision),
        grid=(T // tm, N // tn),
        in_specs=in_specs,
        out_specs=out_specs,
        out_shape=out_shape,
        scratch_shapes=[pltpu.VMEM((tm, D), w.dtype)],
        compiler_params=_cparams(("parallel", "arbitrary")),
        name="norm_matmul",
    )(*args)


def _matmul_res_kernel(a_ref, w_ref, h_ref, o_ref):
    o_ref[...] = h_ref[...] + jnp.dot(a_ref[...], w_ref[...],
                                      preferred_element_type=jnp.float32)


def matmul_residual(a, w, h, *, tm=512):
    T, K = a.shape
    N = w.shape[1]
    tm = min(tm, T)
    return pl.pallas_call(
        _matmul_res_kernel,
        grid=(T // tm,),
        in_specs=[pl.BlockSpec((tm, K), lambda i: (i, 0)),
                  pl.BlockSpec((K, N), lambda i: (0, 0)),
                  pl.BlockSpec((tm, N), lambda i: (i, 0))],
        out_specs=pl.BlockSpec((tm, N), lambda i: (i, 0)),
        out_shape=jax.ShapeDtypeStruct((T, N), jnp.float32),
        compiler_params=_cparams(("parallel",)),
        name="matmul_residual",
    )(a, w, h)


def _t5_bucket(rel):
    nb = REL_BUCKETS // 2
    max_exact = nb // 2
    ret = jnp.where(rel > 0, nb, 0)
    n = jnp.abs(rel)
    nf = jnp.maximum(n, 1).astype(jnp.float32)
    large = max_exact + (jnp.log(nf / max_exact) / math.log(REL_MAX_DIST / max_exact)
                         * (nb - max_exact)).astype(jnp.int32)
    large = jnp.minimum(large, nb - 1)
    return (ret + jnp.where(n < max_exact, n, large)).astype(jnp.int32)


def _bias_kernel(bucket_ref, window_ref, table_ref, o_ref):
    bucket = bucket_ref[...]
    in_window = window_ref[...] > 0
    for hq in range(N_Q_HEADS):
        acc = jnp.zeros(bucket.shape, jnp.float32)
        for b in range(REL_BUCKETS):
            acc = jnp.where(bucket == b, table_ref[b, hq], acc)
        o_ref[hq] = jnp.where(in_window, acc, -jnp.inf)


def attn_bias(rel_table):
    qi = jnp.arange(BLOCK)[:, None]
    kj = jnp.arange(3 * BLOCK)[None, :]
    rel = kj - BLOCK - qi
    bucket = _t5_bucket(rel)
    window = (jnp.abs(rel) <= WINDOW).astype(jnp.int32)
    return pl.pallas_call(
        _bias_kernel,
        in_specs=[pl.BlockSpec(memory_space=pltpu.VMEM),
                  pl.BlockSpec(memory_space=pltpu.VMEM),
                  pl.BlockSpec(memory_space=pltpu.SMEM)],
        out_specs=pl.BlockSpec(memory_space=pltpu.VMEM),
        out_shape=jax.ShapeDtypeStruct((N_Q_HEADS, BLOCK, 3 * BLOCK), jnp.float32),
        name="attn_bias",
    )(bucket, window, rel_table)


def _attn_kernel(cur_ref, prev_ref, next_ref, bias_ref, sink_ref, o_ref):
    i = pl.program_id(1)
    nb = pl.num_programs(1)
    q = cur_ref[0, :, 0:Q_DIM] * (HEAD_DIM ** -0.5)
    kband = jnp.concatenate([prev_ref[0, :, 0:KV_DIM], cur_ref[0, :, Q_DIM:Q_DIM + KV_DIM],
                             next_ref[0, :, 0:KV_DIM]], axis=0)
    vband = jnp.concatenate([prev_ref[0, :, KV_DIM:2 * KV_DIM], cur_ref[0, :, Q_DIM + KV_DIM:QKV_DIM],
                             next_ref[0, :, KV_DIM:2 * KV_DIM]], axis=0)
    col = lax.broadcasted_iota(jnp.int32, (1, 3 * BLOCK), 1)
    valid = jnp.logical_and(jnp.logical_or(col >= BLOCK, i > 0),
                            jnp.logical_or(col < 2 * BLOCK, i < nb - 1))
    heads = range(N_Q_HEADS)
    kv = [(kband[:, hk * HEAD_DIM:(hk + 1) * HEAD_DIM], vband[:, hk * HEAD_DIM:(hk + 1) * HEAD_DIM])
          for hk in range(N_KV_HEADS)]
    s = []
    for hq in heads:
        qh = q[:, hq * HEAD_DIM:(hq + 1) * HEAD_DIM]
        sc = lax.dot_general(qh, kv[hq // GQA_GROUP][0], (((1,), (1,)), ((), ())),
                             preferred_element_type=jnp.float32)
        s.append(jnp.where(valid, sc + bias_ref[hq], -jnp.inf))
    mx = [jnp.maximum(jnp.max(s[hq], axis=-1, keepdims=True), sink_ref[hq]) for hq in heads]
    p = [jnp.exp(s[hq] - mx[hq]) for hq in heads]
    denom = [jnp.sum(p[hq], axis=-1, keepdims=True) + jnp.exp(sink_ref[hq] - mx[hq]) for hq in heads]
    outs = [jnp.dot((p[hq] / denom[hq]).astype(jnp.bfloat16), kv[hq // GQA_GROUP][1],
                    preferred_element_type=jnp.float32) for hq in heads]
    o_ref[0] = jnp.concatenate(outs, axis=-1).astype(o_ref.dtype)


def window_attention(qkv, bias, sink):
    B, S, _ = qkv.shape
    nb = S // BLOCK
    kv_col = Q_DIM // (2 * KV_DIM)
    return pl.pallas_call(
        _attn_kernel,
        grid=(B, nb),
        in_specs=[pl.BlockSpec((1, BLOCK, QKV_DIM), lambda b, i: (b, i, 0)),
                  pl.BlockSpec((1, BLOCK, 2 * KV_DIM),
                               lambda b, i: (b, jnp.maximum(i - 1, 0), kv_col)),
                  pl.BlockSpec((1, BLOCK, 2 * KV_DIM),
                               lambda b, i: (b, jnp.minimum(i + 1, nb - 1), kv_col)),
                  pl.BlockSpec((N_Q_HEADS, BLOCK, 3 * BLOCK), lambda b, i: (0, 0, 0)),
                  pl.BlockSpec(memory_space=pltpu.SMEM)],
        out_specs=pl.BlockSpec((1, BLOCK, Q_DIM), lambda b, i: (b, i, 0)),
        out_shape=jax.ShapeDtypeStruct((B, S, Q_DIM), jnp.bfloat16),
        compiler_params=_cparams(("parallel", "arbitrary")),
        name="window_attention",
    )(qkv, qkv, qkv, bias, sink)


def _spatial_gate_kernel(z_ref, g_ref, b_ref, wsp_ref, bsp_ref, o_ref):
    v = z_ref[:, D_GATE:2 * D_GATE].astype(jnp.float32)
    mu = jnp.mean(v, axis=-1, keepdims=True)
    vc = v - mu
    var = jnp.mean(vc * vc, axis=-1, keepdims=True)
    vn = (vc * lax.rsqrt(var + LN_EPS) * g_ref[...] + b_ref[...]).astype(jnp.bfloat16)
    for grp in range(N_SG_GROUPS):
        lo, hi = grp * SG_GROUP_DIM, (grp + 1) * SG_GROUP_DIM
        mixed = jnp.dot(wsp_ref[grp], vn[:, lo:hi], preferred_element_type=jnp.float32)
        mixed = mixed + bsp_ref[:, grp:grp + 1]
        u = z_ref[:, lo:hi].astype(jnp.float32)
        o_ref[:, lo:hi] = (u * mixed).astype(o_ref.dtype)


def spatial_gate(z, ln_g, ln_b, w_sp, b_sp):
    T = z.shape[0]
    return pl.pallas_call(
        _spatial_gate_kernel,
        grid=(T // CHUNK,),
        in_specs=[pl.BlockSpec((CHUNK, 2 * D_GATE), lambda i: (i, 0)),
                  pl.BlockSpec((1, D_GATE), lambda i: (0, 0)),
                  pl.BlockSpec((1, D_GATE), lambda i: (0, 0)),
                  pl.BlockSpec((N_SG_GROUPS, CHUNK, CHUNK), lambda i: (0, 0, 0)),
                  pl.BlockSpec((CHUNK, N_SG_GROUPS), lambda i: (0, 0))],
        out_specs=pl.BlockSpec((CHUNK, D_GATE), lambda i: (i, 0)),
        out_shape=jax.ShapeDtypeStruct((T, D_GATE), jnp.bfloat16),
        compiler_params=_cparams(("parallel",)),
        name="spatial_gate",
    )(z, ln_g.reshape(1, D_GATE), ln_b.reshape(1, D_GATE), w_sp.astype(jnp.bfloat16), b_sp.T)


def _oddeven_merge_sort_pairs(n):
    pairs = []
    p = 1
    while p < n:
        k = p
        while k >= 1:
            for j in range(k % p, n - k, 2 * k):
                for i in range(min(k, n - j - k)):
                    if (i + j) // (2 * p) == (i + j + k) // (2 * p):
                        pairs.append((i + j, i + j + k))
            k //= 2
        p *= 2
    return pairs


_SORT16 = _oddeven_merge_sort_pairs(N_KEYS // SUBLANES)


def _top16_of_keys(s):
    nv = N_KEYS // SUBLANES
    L = s.shape[1]
    sub = lax.broadcasted_iota(jnp.int32, (SUBLANES, L), 0)
    v = [s[j * SUBLANES:(j + 1) * SUBLANES] for j in range(nv)]
    ids = [sub + j * SUBLANES for j in range(nv)]
    for i, j in _SORT16:
        swap = jnp.logical_or(v[j] > v[i], jnp.logical_and(v[j] == v[i], ids[j] < ids[i]))
        v[i], v[j] = jnp.where(swap, v[j], v[i]), jnp.where(swap, v[i], v[j])
        ids[i], ids[j] = jnp.where(swap, ids[j], ids[i]), jnp.where(swap, ids[i], ids[j])
    vals, picks = [], []
    for it in range(PEER_TOPK):
        m = jnp.max(v[0], axis=0, keepdims=True)
        am = jnp.min(jnp.where(v[0] == m, ids[0], N_KEYS), axis=0, keepdims=True)
        hit = ids[0] == am
        vals.append(m)
        picks.append(am)
        last = PEER_TOPK - 1 - it
        for j in range(last):
            v[j] = jnp.where(hit, v[j + 1], v[j])
            ids[j] = jnp.where(hit, ids[j + 1], ids[j])
        v[last] = jnp.where(hit, -jnp.inf, v[last])
    return jnp.concatenate(vals, axis=0), jnp.concatenate(picks, axis=0)


_CAND_ROWS = (
    [(a, 0) for a in range(16)]
    + [None] + [(0, b) for b in range(1, 16)]
    + [None] + [(a, 1) for a in range(1, 8)]
    + [None, None] + [(1, b) for b in range(2, 8)]
    + [(2, 2), (3, 2), (4, 2), (2, 3), (2, 4), (3, 3), None, None]
)
assert sorted(x for x in _CAND_ROWS if x) == sorted(
    (a, b) for a in range(16) for b in range(16) if (a + 1) * (b + 1) <= 16)


def _rows(x, picks):
    pieces, i = [], 0
    while i < len(picks):
        j = i
        while j + 1 < len(picks) and picks[j + 1] == picks[j] + 1:
            j += 1
        pieces.append(x[picks[i]:picks[j] + 1])
        i = j + 1
    return pieces[0] if len(pieces) == 1 else jnp.concatenate(pieces, axis=0)


def _cand_positions(L):
    row = lax.broadcasted_iota(jnp.int32, (len(_CAND_ROWS), L), 0)
    pos = jnp.full(row.shape, _PAD_POS, jnp.int32)
    for r, c in enumerate(_CAND_ROWS):
        if c is not None:
            pos = jnp.where(row == r, c[0] * PEER_TOPK + c[1], pos)
    return pos


_PAD_POS = 1 << 20


def _joint_top16(v1, i1, v2, i2, pos):
    a_of = [c[0] if c else 0 for c in _CAND_ROWS]
    b_of = [c[1] if c else 0 for c in _CAND_ROWS]
    cand = jnp.where(pos < _PAD_POS, _rows(v1, a_of) + _rows(v2, b_of), -jnp.inf)
    cidx = _rows(i1, a_of) * N_KEYS + _rows(i2, b_of)
    vals, picks = [], []
    for _ in range(PEER_TOPK):
        m = jnp.max(cand, axis=0, keepdims=True)
        pm = jnp.min(jnp.where(cand == m, pos, 1 << 21), axis=0, keepdims=True)
        hit = pos == pm
        vals.append(m)
        picks.append(jnp.max(jnp.where(hit, cidx, -1), axis=0, keepdims=True))
        cand = jnp.where(hit, -jnp.inf, cand)
    return jnp.concatenate(vals, axis=0), jnp.concatenate(picks, axis=0)


def _route_kernel(q_ref, keys_ref, idx_ref, gate_ref, *, precision):
    nt = (((1,), (1,)), ((), ()))
    pos = _cand_positions(LANES)
    for j in range(q_ref.shape[0] // LANES):
        q = q_ref[j * LANES:(j + 1) * LANES, :]
        s1 = lax.dot_general(keys_ref[0, 0], q[:, 0:D_HALF], nt,
                             preferred_element_type=jnp.float32, precision=precision)
        s2 = lax.dot_general(keys_ref[0, 1], q[:, D_HALF:2 * D_HALF], nt,
                             preferred_element_type=jnp.float32, precision=precision)
        v1, i1 = _top16_of_keys(s1)
        v2, i2 = _top16_of_keys(s2)
        top_s, top_i = _joint_top16(v1, i1, v2, i2, pos)
        e = jnp.exp(top_s - top_s[0:1])
        gate_ref[:, j * LANES:(j + 1) * LANES] = e / jnp.sum(e, axis=0, keepdims=True)
        idx_ref[:, j * LANES:(j + 1) * LANES] = top_i * ROW_SUB


def peer_route(q, sub_keys, *, tb=1024, precision=None):
    T = q.shape[0]
    tb = min(tb, T)
    return pl.pallas_call(
        functools.partial(_route_kernel, precision=precision),
        grid=(T // tb, PEER_HEADS),
        in_specs=[pl.BlockSpec((tb, 2 * D_HALF), lambda i, h: (i, h)),
                  pl.BlockSpec((1, 2, N_KEYS, D_HALF), lambda i, h: (h, 0, 0, 0))],
        out_specs=[pl.BlockSpec((PEER_TOPK, tb), lambda i, h: (h, i)),
                   pl.BlockSpec((PEER_TOPK, tb), lambda i, h: (h, i))],
        out_shape=[jax.ShapeDtypeStruct((N_SLOTS, T), jnp.int32),
                   jax.ShapeDtypeStruct((N_SLOTS, T), jnp.float32)],
        compiler_params=_cparams(("parallel", "arbitrary")),
        name="peer_route",
    )(q, sub_keys)


def _pack_kernel(x_ref, o_ref, *, split_rows):
    x = x_ref[...]
    half = x.shape[1] // 2

    def bf16_bits(v):
        return pltpu.bitcast(v.astype(jnp.bfloat16).astype(jnp.float32), jnp.uint32)

    words = (bf16_bits(x[:, :half]) >> 16) | (bf16_bits(x[:, half:]) & jnp.uint32(0xFFFF0000))
    if split_rows:
        n = x.shape[0]
        for c in range(ROW_SUB):
            o_ref[pl.ds(c, n, stride=ROW_SUB), :] = words[:, c * LANES:(c + 1) * LANES]
    else:
        o_ref[...] = words


def pack_table(tbls, layer, *, split_rows, te=512):
    _, E, D = tbls.shape
    out_block, out_shape = ((te * ROW_SUB, LANES), (E * ROW_SUB, LANES)) if split_rows else ((te, D // 2), (E, D // 2))
    return pl.pallas_call(
        functools.partial(_pack_kernel, split_rows=split_rows),
        grid=(E // te,),
        in_specs=[pl.BlockSpec((None, te, D), lambda i: (layer, i, 0))],
        out_specs=pl.BlockSpec(out_block, lambda i: (i, 0)),
        out_shape=jax.ShapeDtypeStruct(out_shape, jnp.uint32),
        compiler_params=_cparams(("parallel",)),
        name="pack_table",
    )(tbls)


def _unpack(words):
    lo = pltpu.bitcast(words << 16, jnp.float32)
    hi = pltpu.bitcast(words & jnp.uint32(0xFFFF0000), jnp.float32)
    return lo, hi


def _gather_rows(idx_ref, tbl_ref, t, planes_ref):
    for k in range(N_SLOTS):
        off = pl.multiple_of(idx_ref[t, k], ROW_SUB)
        planes_ref[pl.ds(k, ROW_SUB, stride=PLANE_STRIDE), :] = tbl_ref[pl.ds(off, ROW_SUB), :]


def _plane(planes_ref, c):
    return _unpack(planes_ref[c * PLANE_STRIDE:c * PLANE_STRIDE + N_SLOTS, :])


def _peer_down_kernel(idx_ref, x_ref, gate_ref, tbl_ref, o_ref, planes_a, planes_b, part_ref):
    tb = x_ref.shape[0]
    lane = lax.broadcasted_iota(jnp.int32, (N_SLOTS, tb), 1)

    def compute(t, planes_ref, slot):
        acc = jnp.zeros((N_SLOTS, LANES), jnp.float32)
        for c in range(ROW_SUB):
            lo, hi = _plane(planes_ref, c)
            acc = acc + lo * x_ref[t, c:c + 1, :] + hi * x_ref[t, ROW_SUB + c:ROW_SUB + c + 1, :]
        part_ref[slot] = acc

    def place_pair(t0):
        col_a = jnp.sum(part_ref[0], axis=1, keepdims=True)
        col_b = jnp.sum(part_ref[1], axis=1, keepdims=True)
        o_ref[...] = jnp.where(lane == t0 - 2, col_a, jnp.where(lane == t0 - 1, col_b, o_ref[...]))

    o_ref[...] = jnp.zeros(o_ref.shape, jnp.float32)
    part_ref[...] = jnp.zeros(part_ref.shape, jnp.float32)
    _gather_rows(idx_ref, tbl_ref, 0, planes_a)

    def pair(i, carry):
        t0 = 2 * i
        place_pair(t0)
        _gather_rows(idx_ref, tbl_ref, t0 + 1, planes_b)
        compute(t0, planes_a, 0)
        _gather_rows(idx_ref, tbl_ref, jnp.minimum(t0 + 2, tb - 1), planes_a)
        compute(t0 + 1, planes_b, 1)
        return carry

    lax.fori_loop(0, tb // 2, pair, 0)
    place_pair(tb)
    o_ref[...] = _gelu(o_ref[...]) * gate_ref[...]


def peer_down(idx, xn3, gate_t, tbl, *, tb=128):
    T = xn3.shape[0]
    tb = min(tb, T)
    assert tb % 2 == 0 and T % tb == 0
    planes = pltpu.VMEM((ROW_SUB * PLANE_STRIDE, LANES), jnp.uint32)
    return pl.pallas_call(
        _peer_down_kernel,
        grid=(T // tb,),
        in_specs=[pl.BlockSpec((tb, N_SLOTS), lambda i: (i, 0), memory_space=pltpu.SMEM),
                  pl.BlockSpec((tb, SUBLANES, LANES), lambda i: (i, 0, 0)),
                  pl.BlockSpec((N_SLOTS, tb), lambda i: (0, i)),
                  pl.BlockSpec(tbl.shape, lambda i: (0, 0), pipeline_mode=pl.Buffered(1))],
        out_specs=pl.BlockSpec((N_SLOTS, tb), lambda i: (0, i)),
        out_shape=jax.ShapeDtypeStruct((N_SLOTS, T), jnp.float32),
        scratch_shapes=[planes, planes, pltpu.VMEM((2, N_SLOTS, LANES), jnp.float32)],
        compiler_params=_cparams(("arbitrary",)),
        name="peer_down",
    )(idx, xn3, gate_t, tbl)


SC_CORES = 2
SC_SUBCORES = 16
SC_WORKERS = SC_CORES * SC_SUBCORES
SC_LANES = 16
SC_GATHER_ROWS = N_SLOTS // 2
SC_ROW_BLOCK = 8
SC_TOKEN_BLOCK = 32
SC_COL_GROUP = 8


def _sc_params():
    cp = pltpu.CompilerParams()
    if "needs_layout_passes" in pltpu.CompilerParams.__dataclass_fields__:
        cp = dataclasses.replace(cp, needs_layout_passes=False)
    return cp


def peer_up_sc(ids, w, tbl):
    T = ids.shape[0]
    L, CH, RB, TBK = SC_LANES, SC_GATHER_ROWS, SC_ROW_BLOCK, SC_TOKEN_BLOCK
    assert T % (SC_WORKERS * TBK) == 0
    tpw = T // SC_WORKERS
    mesh = plsc.VectorSubcoreMesh(core_axis_name="c", subcore_axis_name="s")
    rows_buf = pltpu.VMEM((CH, ROW_WORDS), jnp.uint32)
    y_buf = pltpu.VMEM((D_MODEL,), jnp.float32)

    @functools.partial(
        pl.kernel, mesh=mesh, compiler_params=_sc_params(),
        out_type=jax.ShapeDtypeStruct((T, D_MODEL), jnp.float32),
        scratch_types=[pltpu.VMEM((TBK, N_SLOTS), jnp.int32), pltpu.VMEM((TBK, N_SLOTS), jnp.float32),
                       rows_buf, rows_buf, y_buf, y_buf] + [pltpu.SemaphoreType.DMA] * 4,
    )
    def up_kernel(ids_hbm, w_hbm, tbl_hbm, out_hbm, ids_v, w_v, rows0, rows1, y0, y1, g0, g1, o0, o1):
        base = (lax.axis_index("s") * SC_CORES + lax.axis_index("c")) * tpw
        rows, gsem, ys, osem = (rows0, rows1), (g0, g1), (y0, y1), (o0, o1)

        def gather(tl, half):
            return pltpu.make_async_copy(tbl_hbm.at[ids_v.at[tl, pl.ds(half * CH, CH)]], rows[half], gsem[half])

        def out_copy(tok, slot):
            return pltpu.make_async_copy(ys[slot], out_hbm.at[tok], osem[slot])

        def accumulate(tl, half, y_v):
            rows_v = rows[half]

            def block(b, carry):
                r0 = b * RB
                tlv = jnp.full((L,), tl, jnp.int32)
                wks = [plsc.load_gather(w_v, [tlv, jnp.full((L,), half * CH + r0 + r, jnp.int32)])
                       for r in range(RB)]
                for q in range(ROW_WORDS // (SC_COL_GROUP * L)):
                    acc = [None] * (2 * SC_COL_GROUP)
                    for r in range(RB):
                        for i in range(SC_COL_GROUP):
                            v = rows_v[r0 + r, pl.ds((q * SC_COL_GROUP + i) * L, L)]
                            lo = plsc.bitcast(v << 16, jnp.float32) * wks[r]
                            hi = plsc.bitcast(v & jnp.uint32(0xFFFF0000), jnp.float32) * wks[r]
                            acc[2 * i] = lo if r == 0 else acc[2 * i] + lo
                            acc[2 * i + 1] = hi if r == 0 else acc[2 * i + 1] + hi
                    for i in range(SC_COL_GROUP):
                        col = (q * SC_COL_GROUP + i) * L
                        plsc.addupdate(y_v.at[pl.ds(col, L)], acc[2 * i])
                        plsc.addupdate(y_v.at[pl.ds(ROW_WORDS + col, L)], acc[2 * i + 1])
                return carry

            lax.fori_loop(0, CH // RB, block, 0)

        def token_block(bi, carry):
            tok0 = base + bi * TBK
            pltpu.sync_copy(ids_hbm.at[pl.ds(tok0, TBK)], ids_v)
            pltpu.sync_copy(w_hbm.at[pl.ds(tok0, TBK)], w_v)
            gather(0, 0).start()

            def token_pair(pi, carry2):
                for slot in range(2):
                    tl = 2 * pi + slot
                    y_v = ys[slot]

                    @pl.when(pi > 0)
                    def _():
                        out_copy(tok0 + tl - 2, slot).wait()

                    zero = jnp.zeros((L,), jnp.float32)
                    for j in range(D_MODEL // L):
                        y_v[pl.ds(j * L, L)] = zero
                    gather(tl, 1).start()
                    gather(tl, 0).wait()
                    accumulate(tl, 0, y_v)

                    @pl.when(tl + 1 < TBK)
                    def _():
                        gather(tl + 1, 0).start()

                    gather(tl, 1).wait()
                    accumulate(tl, 1, y_v)
                    out_copy(tok0 + tl, slot).start()
                return carry2

            lax.fori_loop(0, TBK // 2, token_pair, 0)
            out_copy(tok0 + TBK - 2, 0).wait()
            out_copy(tok0 + TBK - 1, 1).wait()
            return carry

        lax.fori_loop(0, tpw // TBK, token_block, 0)

    return up_kernel(ids, w, tbl)


def peer_layer(h, g, w_query, sub_keys, down_rows, up_rows):
    T = h.shape[0]
    q, xn = norm_matmul(h, g, w_query, out_dtype=jnp.float32, emit_xn=True)
    idx_t, gate_t = peer_route(q, sub_keys, precision=lax.Precision.HIGHEST)
    idx = idx_t.T
    w_t = peer_down(idx, xn.reshape(T, SUBLANES, LANES), gate_t, down_rows)
    return peer_up_sc(idx // ROW_SUB, w_t.T, up_rows)


def _rms_kernel(x_ref, r_ref, g_ref, o_ref):
    x = x_ref[...] + r_ref[...]
    o_ref[...] = x * lax.rsqrt(jnp.mean(x * x, axis=-1, keepdims=True) + NORM_EPS) * g_ref[...]


def rms_norm_sum(x, r, g, *, tm=512):
    T, D = x.shape
    tm = min(tm, T)
    row = pl.BlockSpec((tm, D), lambda i: (i, 0))
    return pl.pallas_call(
        _rms_kernel,
        grid=(T // tm,),
        in_specs=[row, row, pl.BlockSpec((1, D), lambda i: (0, 0))],
        out_specs=row,
        out_shape=jax.ShapeDtypeStruct((T, D), jnp.float32),
        compiler_params=_cparams(("parallel",)),
        name="final_norm",
    )(x, r, g.reshape(1, D))


PIECE_SEQS = 2


def _piece_bounds(B):
    sizes = [PIECE_SEQS] * (B // PIECE_SEQS) + ([B % PIECE_SEQS] if B % PIECE_SEQS else [])
    if sizes[-1] > 1:
        sizes[-1:] = [sizes[-1] - sizes[-1] // 2, sizes[-1] // 2]
    starts = [sum(sizes[:i]) for i in range(len(sizes))]
    return list(zip(starts, sizes))


def kernel(x, rel_table, mix_norm_g, attn_w_in, attn_w_out, attn_sink, sg_w_in, sg_ln_g, sg_ln_b, sg_w_spatial, sg_b_spatial, sg_w_out, ffn_norm_g, peer_w_query, peer_sub_keys, peer_down, peer_up, final_norm_g):
    B, S, D = x.shape
    bf16 = jnp.bfloat16

    bias = attn_bias(rel_table)
    attn_in, attn_out = attn_w_in[0].astype(bf16), attn_w_out[0].astype(bf16)
    sg_in, sg_out = sg_w_in[0].astype(bf16), sg_w_out[0].astype(bf16)
    w_query = peer_w_query.astype(bf16)
    down_rows = [pack_table(peer_down, i, split_rows=True) for i in range(2)]
    up_rows = [pack_table(peer_up, i, split_rows=False) for i in range(2)]

    mid = []
    for b0, nb in _piece_bounds(B):
        h = x[b0:b0 + nb].reshape(nb * S, D)
        qkv, = norm_matmul(h, mix_norm_g[0], attn_in, out_dtype=bf16)
        att = window_attention(qkv.reshape(nb, S, QKV_DIM), bias, attn_sink[0])
        h = matmul_residual(att.reshape(nb * S, Q_DIM), attn_out, h)
        mid.append((h, peer_layer(h, ffn_norm_g[0], w_query[0], peer_sub_keys[0], down_rows[0], up_rows[0])))

    outs = []
    for h, y in mid:
        z, h = norm_matmul(h, mix_norm_g[1], sg_in, res=y, out_dtype=bf16, act=True, tn=1536)
        gated = spatial_gate(z, sg_ln_g[0], sg_ln_b[0], sg_w_spatial[0], sg_b_spatial[0])
        h = matmul_residual(gated, sg_out, h)
        y = peer_layer(h, ffn_norm_g[1], w_query[1], peer_sub_keys[1], down_rows[1], up_rows[1])
        outs.append(rms_norm_sum(h, y, final_norm_g).reshape(-1, S, D))
    return jnp.concatenate(outs, axis=0)
```

```python
import dataclasses
import functools
import math

import jax
import jax.numpy as jnp
from jax import lax
from jax.experimental import pallas as pl
from jax.experimental.pallas import tpu as pltpu
from jax.experimental.pallas import tpu_sc as plsc

D_MODEL = 1024
HEAD_DIM = 64
N_Q_HEADS = 16
N_KV_HEADS = 4
GQA_GROUP = 4
WINDOW = 128
BLOCK = 128
REL_BUCKETS = 32
REL_MAX_DIST = 128
Q_DIM = N_Q_HEADS * HEAD_DIM
KV_DIM = N_KV_HEADS * HEAD_DIM
QKV_DIM = Q_DIM + 2 * KV_DIM
CHUNK = 128
D_GATE = 3072
N_SG_GROUPS = 8
SG_GROUP_DIM = D_GATE // N_SG_GROUPS
N_KEYS = 128
PEER_HEADS = 8
PEER_TOPK = 16
D_HALF = 128
N_SLOTS = PEER_HEADS * PEER_TOPK
NORM_EPS = 1e-6
LN_EPS = 1e-5

LANES = 128
SUBLANES = 8
ROW_WORDS = D_MODEL // 2
ROW_SUB = ROW_WORDS // LANES
PLANE_STRIDE = 136
VMEM_LIMIT = 56 * 1024 * 1024

_GELU_C = math.sqrt(2.0 / math.pi)


def _gelu(x):
    return 0.5 * x * (1.0 + jnp.tanh(_GELU_C * (x + 0.044715 * (x * x * x))))


def _cparams(sem):
    return pltpu.CompilerParams(dimension_semantics=sem, vmem_limit_bytes=VMEM_LIMIT)


def _norm_matmul_kernel(*refs, act, emit_xn, add_res, precision):
    refs = list(refs)
    x_ref = refs.pop(0)
    r_ref = refs.pop(0) if add_res else None
    g_ref, w_ref, o_ref = refs.pop(0), refs.pop(0), refs.pop(0)
    xn_ref = refs.pop(0) if emit_xn else None
    sum_ref = refs.pop(0) if add_res else None
    xs_ref = refs.pop(0)

    @pl.when(pl.program_id(1) == 0)
    def _():
        x = x_ref[...]
        if add_res:
            x = x + r_ref[...]
            sum_ref[...] = x
        y = x * lax.rsqrt(jnp.mean(x * x, axis=-1, keepdims=True) + NORM_EPS) * g_ref[...]
        xs_ref[...] = y.astype(xs_ref.dtype)
        if emit_xn:
            xn_ref[...] = y

    acc = jnp.dot(xs_ref[...], w_ref[...], preferred_element_type=jnp.float32,
                  precision=precision)
    if act:
        acc = _gelu(acc)
    o_ref[...] = acc.astype(o_ref.dtype)


def norm_matmul(x, g, w, *, out_dtype, res=None, act=False, emit_xn=False, tm=512, tn=None,
                precision=None):
    T, D = x.shape
    N = w.shape[1]
    tn = tn or N
    tm = min(tm, T)
    row = pl.BlockSpec((tm, D), lambda i, j: (i, 0))
    out_shape = [jax.ShapeDtypeStruct((T, N), out_dtype)]
    out_specs = [pl.BlockSpec((tm, tn), lambda i, j: (i, j))]
    for flag in (emit_xn, res is not None):
        if flag:
            out_shape.append(jax.ShapeDtypeStruct((T, D), jnp.float32))
            out_specs.append(row)
    args = [x] + ([res] if res is not None else []) + [g.reshape(1, D), w]
    in_specs = [row] * (len(args) - 2) + [pl.BlockSpec((1, D), lambda i, j: (0, 0)),
                                          pl.BlockSpec((D, tn), lambda i, j: (0, j))]
    return pl.pallas_call(
        functools.partial(_norm_matmul_kernel, act=act, emit_xn=emit_xn, add_res=res is not None,
                          precision=precision),
        grid=(T // tm, N // tn),
        in_specs=in_specs,
        out_specs=out_specs,
        out_shape=out_shape,
        scratch_shapes=[pltpu.VMEM((tm, D), w.dtype)],
        compiler_params=_cparams(("parallel", "arbitrary")),
        name="norm_matmul",
    )(*args)


def _matmul_res_kernel(a_ref, w_ref, h_ref, o_ref):
    o_ref[...] = h_ref[...] + jnp.dot(a_ref[...], w_ref[...],
                                      preferred_element_type=jnp.float32)


def matmul_residual(a, w, h, *, tm=512):
    T, K = a.shape
    N = w.shape[1]
    tm = min(tm, T)
    return pl.pallas_call(
        _matmul_res_kernel,
        grid=(T // tm,),
        in_specs=[pl.BlockSpec((tm, K), lambda i: (i, 0)),
                  pl.BlockSpec((K, N), lambda i: (0, 0)),
                  pl.BlockSpec((tm, N), lambda i: (i, 0))],
        out_specs=pl.BlockSpec((tm, N), lambda i: (i, 0)),
        out_shape=jax.ShapeDtypeStruct((T, N), jnp.float32),
        compiler_params=_cparams(("parallel",)),
        name="matmul_residual",
    )(a, w, h)


def _t5_bucket(rel):
    nb = REL_BUCKETS // 2
    max_exact = nb // 2
    ret = jnp.where(rel > 0, nb, 0)
    n = jnp.abs(rel)
    nf = jnp.maximum(n, 1).astype(jnp.float32)
    large = max_exact + (jnp.log(nf / max_exact) / math.log(REL_MAX_DIST / max_exact)
                         * (nb - max_exact)).astype(jnp.int32)
    large = jnp.minimum(large, nb - 1)
    return (ret + jnp.where(n < max_exact, n, large)).astype(jnp.int32)


def _bias_kernel(bucket_ref, window_ref, table_ref, o_ref):
    bucket = bucket_ref[...]
    in_window = window_ref[...] > 0
    for hq in range(N_Q_HEADS):
        acc = jnp.zeros(bucket.shape, jnp.float32)
        for b in range(REL_BUCKETS):
            acc = jnp.where(bucket == b, table_ref[b, hq], acc)
        o_ref[hq] = jnp.where(in_window, acc, -jnp.inf)


def attn_bias(rel_table):
    qi = jnp.arange(BLOCK)[:, None]
    kj = jnp.arange(3 * BLOCK)[None, :]
    rel = kj - BLOCK - qi
    bucket = _t5_bucket(rel)
    window = (jnp.abs(rel) <= WINDOW).astype(jnp.int32)
    return pl.pallas_call(
        _bias_kernel,
        in_specs=[pl.BlockSpec(memory_space=pltpu.VMEM),
                  pl.BlockSpec(memory_space=pltpu.VMEM),
                  pl.BlockSpec(memory_space=pltpu.SMEM)],
        out_specs=pl.BlockSpec(memory_space=pltpu.VMEM),
        out_shape=jax.ShapeDtypeStruct((N_Q_HEADS, BLOCK, 3 * BLOCK), jnp.float32),
        name="attn_bias",
    )(bucket, window, rel_table)


def _attn_kernel(cur_ref, prev_ref, next_ref, bias_ref, sink_ref, o_ref):
    i = pl.program_id(1)
    nb = pl.num_programs(1)
    q = cur_ref[0, :, 0:Q_DIM] * (HEAD_DIM ** -0.5)
    kband = jnp.concatenate([prev_ref[0, :, 0:KV_DIM], cur_ref[0, :, Q_DIM:Q_DIM + KV_DIM],
                             next_ref[0, :, 0:KV_DIM]], axis=0)
    vband = jnp.concatenate([prev_ref[0, :, KV_DIM:2 * KV_DIM], cur_ref[0, :, Q_DIM + KV_DIM:QKV_DIM],
                             next_ref[0, :, KV_DIM:2 * KV_DIM]], axis=0)
    col = lax.broadcasted_iota(jnp.int32, (1, 3 * BLOCK), 1)
    valid = jnp.logical_and(jnp.logical_or(col >= BLOCK, i > 0),
                            jnp.logical_or(col < 2 * BLOCK, i < nb - 1))
    heads = range(N_Q_HEADS)
    kv = [(kband[:, hk * HEAD_DIM:(hk + 1) * HEAD_DIM], vband[:, hk * HEAD_DIM:(hk + 1) * HEAD_DIM])
          for hk in range(N_KV_HEADS)]
    s = []
    for hq in heads:
        qh = q[:, hq * HEAD_DIM:(hq + 1) * HEAD_DIM]
        sc = lax.dot_general(qh, kv[hq // GQA_GROUP][0], (((1,), (1,)), ((), ())),
                             preferred_element_type=jnp.float32)
        s.append(jnp.where(valid, sc + bias_ref[hq], -jnp.inf))
    mx = [jnp.maximum(jnp.max(s[hq], axis=-1, keepdims=True), sink_ref[hq]) for hq in heads]
    p = [jnp.exp(s[hq] - mx[hq]) for hq in heads]
    denom = [jnp.sum(p[hq], axis=-1, keepdims=True) + jnp.exp(sink_ref[hq] - mx[hq]) for hq in heads]
    outs = [jnp.dot((p[hq] / denom[hq]).astype(jnp.bfloat16), kv[hq // GQA_GROUP][1],
                    preferred_element_type=jnp.float32) for hq in heads]
    o_ref[0] = jnp.concatenate(outs, axis=-1).astype(o_ref.dtype)


def window_attention(qkv, bias, sink):
    B, S, _ = qkv.shape
    nb = S // BLOCK
    kv_col = Q_DIM // (2 * KV_DIM)
    return pl.pallas_call(
        _attn_kernel,
        grid=(B, nb),
        in_specs=[pl.BlockSpec((1, BLOCK, QKV_DIM), lambda b, i: (b, i, 0)),
                  pl.BlockSpec((1, BLOCK, 2 * KV_DIM),
                               lambda b, i: (b, jnp.maximum(i - 1, 0), kv_col)),
                  pl.BlockSpec((1, BLOCK, 2 * KV_DIM),
                               lambda b, i: (b, jnp.minimum(i + 1, nb - 1), kv_col)),
                  pl.BlockSpec((N_Q_HEADS, BLOCK, 3 * BLOCK), lambda b, i: (0, 0, 0)),
                  pl.BlockSpec(memory_space=pltpu.SMEM)],
        out_specs=pl.BlockSpec((1, BLOCK, Q_DIM), lambda b, i: (b, i, 0)),
        out_shape=jax.ShapeDtypeStruct((B, S, Q_DIM), jnp.bfloat16),
        compiler_params=_cparams(("parallel", "arbitrary")),
        name="window_attention",
    )(qkv, qkv, qkv, bias, sink)


def _spatial_gate_kernel(z_ref, g_ref, b_ref, wsp_ref, bsp_ref, o_ref):
    v = z_ref[:, D_GATE:2 * D_GATE].astype(jnp.float32)
    mu = jnp.mean(v, axis=-1, keepdims=True)
    vc = v - mu
    var = jnp.mean(vc * vc, axis=-1, keepdims=True)
    vn = (vc * lax.rsqrt(var + LN_EPS) * g_ref[...] + b_ref[...]).astype(jnp.bfloat16)
    for grp in range(N_SG_GROUPS):
        lo, hi = grp * SG_GROUP_DIM, (grp + 1) * SG_GROUP_DIM
        mixed = jnp.dot(wsp_ref[grp], vn[:, lo:hi], preferred_element_type=jnp.float32)
        mixed = mixed + bsp_ref[:, grp:grp + 1]
        u = z_ref[:, lo:hi].astype(jnp.float32)
        o_ref[:, lo:hi] = (u * mixed).astype(o_ref.dtype)


def spatial_gate(z, ln_g, ln_b, w_sp, b_sp):
    T = z.shape[0]
    return pl.pallas_call(
        _spatial_gate_kernel,
        grid=(T // CHUNK,),
        in_specs=[pl.BlockSpec((CHUNK, 2 * D_GATE), lambda i: (i, 0)),
                  pl.BlockSpec((1, D_GATE), lambda i: (0, 0)),
                  pl.BlockSpec((1, D_GATE), lambda i: (0, 0)),
                  pl.BlockSpec((N_SG_GROUPS, CHUNK, CHUNK), lambda i: (0, 0, 0)),
                  pl.BlockSpec((CHUNK, N_SG_GROUPS), lambda i: (0, 0))],
        out_specs=pl.BlockSpec((CHUNK, D_GATE), lambda i: (i, 0)),
        out_shape=jax.ShapeDtypeStruct((T, D_GATE), jnp.bfloat16),
        compiler_params=_cparams(("parallel",)),
        name="spatial_gate",
    )(z, ln_g.reshape(1, D_GATE), ln_b.reshape(1, D_GATE), w_sp.astype(jnp.bfloat16), b_sp.T)


def _oddeven_merge_sort_pairs(n):
    pairs = []
    p = 1
    while p < n:
        k = p
        while k >= 1:
            for j in range(k % p, n - k, 2 * k):
                for i in range(min(k, n - j - k)):
                    if (i + j) // (2 * p) == (i + j + k) // (2 * p):
                        pairs.append((i + j, i + j + k))
            k //= 2
        p *= 2
    return pairs


_SORT16 = _oddeven_merge_sort_pairs(N_KEYS // SUBLANES)


def _top16_of_keys(s):
    nv = N_KEYS // SUBLANES
    L = s.shape[1]
    sub = lax.broadcasted_iota(jnp.int32, (SUBLANES, L), 0)
    v = [s[j * SUBLANES:(j + 1) * SUBLANES] for j in range(nv)]
    ids = [sub + j * SUBLANES for j in range(nv)]
    for i, j in _SORT16:
        swap = jnp.logical_or(v[j] > v[i], jnp.logical_and(v[j] == v[i], ids[j] < ids[i]))
        v[i], v[j] = jnp.where(swap, v[j], v[i]), jnp.where(swap, v[i], v[j])
        ids[i], ids[j] = jnp.where(swap, ids[j], ids[i]), jnp.where(swap, ids[i], ids[j])
    vals, picks = [], []
    for it in range(PEER_TOPK):
        m = jnp.max(v[0], axis=0, keepdims=True)
        am = jnp.min(jnp.where(v[0] == m, ids[0], N_KEYS), axis=0, keepdims=True)
        hit = ids[0] == am
        vals.append(m)
        picks.append(am)
        last = PEER_TOPK - 1 - it
        for j in range(last):
            v[j] = jnp.where(hit, v[j + 1], v[j])
            ids[j] = jnp.where(hit, ids[j + 1], ids[j])
        v[last] = jnp.where(hit, -jnp.inf, v[last])
    return jnp.concatenate(vals, axis=0), jnp.concatenate(picks, axis=0)


_CAND_ROWS = (
    [(a, 0) for a in range(16)]
    + [None] + [(0, b) for b in range(1, 16)]
    + [None] + [(a, 1) for a in range(1, 8)]
    + [None, None] + [(1, b) for b in range(2, 8)]
    + [(2, 2), (3, 2), (4, 2), (2, 3), (2, 4), (3, 3), None, None]
)
assert sorted(x for x in _CAND_ROWS if x) == sorted(
    (a, b) for a in range(16) for b in range(16) if (a + 1) * (b + 1) <= 16)


def _rows(x, picks):
    pieces, i = [], 0
    while i < len(picks):
        j = i
        while j + 1 < len(picks) and picks[j + 1] == picks[j] + 1:
            j += 1
        pieces.append(x[picks[i]:picks[j] + 1])
        i = j + 1
    return pieces[0] if len(pieces) == 1 else jnp.concatenate(pieces, axis=0)


def _cand_positions(L):
    row = lax.broadcasted_iota(jnp.int32, (len(_CAND_ROWS), L), 0)
    pos = jnp.full(row.shape, _PAD_POS, jnp.int32)
    for r, c in enumerate(_CAND_ROWS):
        if c is not None:
            pos = jnp.where(row == r, c[0] * PEER_TOPK + c[1], pos)
    return pos


_PAD_POS = 1 << 20


def _joint_top16(v1, i1, v2, i2, pos):
    a_of = [c[0] if c else 0 for c in _CAND_ROWS]
    b_of = [c[1] if c else 0 for c in _CAND_ROWS]
    cand = jnp.where(pos < _PAD_POS, _rows(v1, a_of) + _rows(v2, b_of), -jnp.inf)
    cidx = _rows(i1, a_of) * N_KEYS + _rows(i2, b_of)
    vals, picks = [], []
    for _ in range(PEER_TOPK):
        m = jnp.max(cand, axis=0, keepdims=True)
        pm = jnp.min(jnp.where(cand == m, pos, 1 << 21), axis=0, keepdims=True)
        hit = pos == pm
        vals.append(m)
        picks.append(jnp.max(jnp.where(hit, cidx, -1), axis=0, keepdims=True))
        cand = jnp.where(hit, -jnp.inf, cand)
    return jnp.concatenate(vals, axis=0), jnp.concatenate(picks, axis=0)


def _route_kernel(q_ref, keys_ref, idx_ref, gate_ref, *, precision):
    nt = (((1,), (1,)), ((), ()))
    pos = _cand_positions(LANES)
    for j in range(q_ref.shape[0] // LANES):
        q = q_ref[j * LANES:(j + 1) * LANES, :].astype(keys_ref.dtype)
        s1 = lax.dot_general(keys_ref[0, 0], q[:, 0:D_HALF], nt,
                             preferred_element_type=jnp.float32, precision=precision)
        s2 = lax.dot_general(keys_ref[0, 1], q[:, D_HALF:2 * D_HALF], nt,
                             preferred_element_type=jnp.float32, precision=precision)
        v1, i1 = _top16_of_keys(s1)
        v2, i2 = _top16_of_keys(s2)
        top_s, top_i = _joint_top16(v1, i1, v2, i2, pos)
        e = jnp.exp(top_s - top_s[0:1])
        gate_ref[:, j * LANES:(j + 1) * LANES] = e / jnp.sum(e, axis=0, keepdims=True)
        idx_ref[:, j * LANES:(j + 1) * LANES] = top_i * ROW_SUB


def peer_route(q, sub_keys, *, tb=1024, precision=None):
    T = q.shape[0]
    tb = min(tb, T)
    return pl.pallas_call(
        functools.partial(_route_kernel, precision=precision),
        grid=(T // tb, PEER_HEADS),
        in_specs=[pl.BlockSpec((tb, 2 * D_HALF), lambda i, h: (i, h)),
                  pl.BlockSpec((1, 2, N_KEYS, D_HALF), lambda i, h: (h, 0, 0, 0))],
        out_specs=[pl.BlockSpec((PEER_TOPK, tb), lambda i, h: (h, i)),
                   pl.BlockSpec((PEER_TOPK, tb), lambda i, h: (h, i))],
        out_shape=[jax.ShapeDtypeStruct((N_SLOTS, T), jnp.int32),
                   jax.ShapeDtypeStruct((N_SLOTS, T), jnp.float32)],
        compiler_params=_cparams(("parallel", "arbitrary")),
        name="peer_route",
    )(q, sub_keys)


def _pack_kernel(x_ref, o_ref, *, split_rows):
    x = x_ref[...]
    half = x.shape[1] // 2

    def bf16_bits(v):
        return pltpu.bitcast(v.astype(jnp.bfloat16).astype(jnp.float32), jnp.uint32)

    words = (bf16_bits(x[:, :half]) >> 16) | (bf16_bits(x[:, half:]) & jnp.uint32(0xFFFF0000))
    if split_rows:
        n = x.shape[0]
        for c in range(ROW_SUB):
            o_ref[pl.ds(c, n, stride=ROW_SUB), :] = words[:, c * LANES:(c + 1) * LANES]
    else:
        o_ref[...] = words


def pack_table(tbls, layer, *, split_rows, te=512):
    _, E, D = tbls.shape
    out_block, out_shape = ((te * ROW_SUB, LANES), (E * ROW_SUB, LANES)) if split_rows else ((te, D // 2), (E, D // 2))
    return pl.pallas_call(
        functools.partial(_pack_kernel, split_rows=split_rows),
        grid=(E // te,),
        in_specs=[pl.BlockSpec((None, te, D), lambda i: (layer, i, 0))],
        out_specs=pl.BlockSpec(out_block, lambda i: (i, 0)),
        out_shape=jax.ShapeDtypeStruct(out_shape, jnp.uint32),
        compiler_params=_cparams(("parallel",)),
        name="pack_table",
    )(tbls)


def _unpack(words):
    lo = pltpu.bitcast(words << 16, jnp.float32)
    hi = pltpu.bitcast(words & jnp.uint32(0xFFFF0000), jnp.float32)
    return lo, hi


def _gather_rows(idx_ref, tbl_ref, t, planes_ref):
    for k in range(N_SLOTS):
        off = pl.multiple_of(idx_ref[t, k], ROW_SUB)
        planes_ref[pl.ds(k, ROW_SUB, stride=PLANE_STRIDE), :] = tbl_ref[pl.ds(off, ROW_SUB), :]


def _plane(planes_ref, c):
    return _unpack(planes_ref[c * PLANE_STRIDE:c * PLANE_STRIDE + N_SLOTS, :])


def _peer_down_kernel(idx_ref, x_ref, gate_ref, tbl_ref, o_ref, planes_a, planes_b, part_ref):
    tb = x_ref.shape[0]
    lane = lax.broadcasted_iota(jnp.int32, (N_SLOTS, tb), 1)

    def compute(t, planes_ref, slot):
        acc = jnp.zeros((N_SLOTS, LANES), jnp.float32)
        for c in range(ROW_SUB):
            lo, hi = _plane(planes_ref, c)
            acc = acc + lo * x_ref[t, c:c + 1, :] + hi * x_ref[t, ROW_SUB + c:ROW_SUB + c + 1, :]
        part_ref[slot] = acc

    def place_pair(t0):
        col_a = jnp.sum(part_ref[0], axis=1, keepdims=True)
        col_b = jnp.sum(part_ref[1], axis=1, keepdims=True)
        o_ref[...] = jnp.where(lane == t0 - 2, col_a, jnp.where(lane == t0 - 1, col_b, o_ref[...]))

    o_ref[...] = jnp.zeros(o_ref.shape, jnp.float32)
    part_ref[...] = jnp.zeros(part_ref.shape, jnp.float32)
    _gather_rows(idx_ref, tbl_ref, 0, planes_a)

    def pair(i, carry):
        t0 = 2 * i
        place_pair(t0)
        _gather_rows(idx_ref, tbl_ref, t0 + 1, planes_b)
        compute(t0, planes_a, 0)
        _gather_rows(idx_ref, tbl_ref, jnp.minimum(t0 + 2, tb - 1), planes_a)
        compute(t0 + 1, planes_b, 1)
        return carry

    lax.fori_loop(0, tb // 2, pair, 0)
    place_pair(tb)
    o_ref[...] = _gelu(o_ref[...]) * gate_ref[...]


def peer_down(idx, xn3, gate_t, tbl, *, tb=128):
    T = xn3.shape[0]
    tb = min(tb, T)
    assert tb % 2 == 0 and T % tb == 0
    planes = pltpu.VMEM((ROW_SUB * PLANE_STRIDE, LANES), jnp.uint32)
    return pl.pallas_call(
        _peer_down_kernel,
        grid=(T // tb,),
        in_specs=[pl.BlockSpec((tb, N_SLOTS), lambda i: (i, 0), memory_space=pltpu.SMEM),
                  pl.BlockSpec((tb, SUBLANES, LANES), lambda i: (i, 0, 0)),
                  pl.BlockSpec((N_SLOTS, tb), lambda i: (0, i)),
                  pl.BlockSpec(tbl.shape, lambda i: (0, 0), pipeline_mode=pl.Buffered(1))],
        out_specs=pl.BlockSpec((N_SLOTS, tb), lambda i: (0, i)),
        out_shape=jax.ShapeDtypeStruct((N_SLOTS, T), jnp.float32),
        scratch_shapes=[planes, planes, pltpu.VMEM((2, N_SLOTS, LANES), jnp.float32)],
        compiler_params=_cparams(("arbitrary",)),
        name="peer_down",
    )(idx, xn3, gate_t, tbl)


SC_CORES = 2
SC_SUBCORES = 16
SC_WORKERS = SC_CORES * SC_SUBCORES
SC_LANES = 16
SC_GATHER_ROWS = N_SLOTS // 2
SC_ROW_BLOCK = 8
SC_TOKEN_BLOCK = 32
SC_COL_GROUP = 8


def _sc_params():
    cp = pltpu.CompilerParams()
    if "needs_layout_passes" in pltpu.CompilerParams.__dataclass_fields__:
        cp = dataclasses.replace(cp, needs_layout_passes=False)
    return cp


def peer_up_sc(ids, w, tbl):
    T = ids.shape[0]
    L, CH, RB, TBK = SC_LANES, SC_GATHER_ROWS, SC_ROW_BLOCK, SC_TOKEN_BLOCK
    assert T % (SC_WORKERS * TBK) == 0
    tpw = T // SC_WORKERS
    mesh = plsc.VectorSubcoreMesh(core_axis_name="c", subcore_axis_name="s")
    rows_buf = pltpu.VMEM((CH, ROW_WORDS), jnp.uint32)
    y_buf = pltpu.VMEM((D_MODEL,), jnp.float32)

    @functools.partial(
        pl.kernel, mesh=mesh, compiler_params=_sc_params(),
        out_type=jax.ShapeDtypeStruct((T, D_MODEL), jnp.float32),
        scratch_types=[pltpu.VMEM((TBK, N_SLOTS), jnp.int32), pltpu.VMEM((TBK, N_SLOTS), jnp.float32),
                       rows_buf, rows_buf, y_buf, y_buf] + [pltpu.SemaphoreType.DMA] * 4,
    )
    def up_kernel(ids_hbm, w_hbm, tbl_hbm, out_hbm, ids_v, w_v, rows0, rows1, y0, y1, g0, g1, o0, o1):
        base = (lax.axis_index("s") * SC_CORES + lax.axis_index("c")) * tpw
        rows, gsem, ys, osem = (rows0, rows1), (g0, g1), (y0, y1), (o0, o1)

        def gather(tl, half):
            return pltpu.make_async_copy(tbl_hbm.at[ids_v.at[tl, pl.ds(half * CH, CH)]], rows[half], gsem[half])

        def out_copy(tok, slot):
            return pltpu.make_async_copy(ys[slot], out_hbm.at[tok], osem[slot])

        def accumulate(tl, half, y_v):
            rows_v = rows[half]

            def block(b, carry):
                r0 = b * RB
                tlv = jnp.full((L,), tl, jnp.int32)
                wks = [plsc.load_gather(w_v, [tlv, jnp.full((L,), half * CH + r0 + r, jnp.int32)])
                       for r in range(RB)]
                for q in range(ROW_WORDS // (SC_COL_GROUP * L)):
                    acc = [None] * (2 * SC_COL_GROUP)
                    for r in range(RB):
                        for i in range(SC_COL_GROUP):
                            v = rows_v[r0 + r, pl.ds((q * SC_COL_GROUP + i) * L, L)]
                            lo = plsc.bitcast(v << 16, jnp.float32) * wks[r]
                            hi = plsc.bitcast(v & jnp.uint32(0xFFFF0000), jnp.float32) * wks[r]
                            acc[2 * i] = lo if r == 0 else acc[2 * i] + lo
                            acc[2 * i + 1] = hi if r == 0 else acc[2 * i + 1] + hi
                    for i in range(SC_COL_GROUP):
                        col = (q * SC_COL_GROUP + i) * L
                        plsc.addupdate(y_v.at[pl.ds(col, L)], acc[2 * i])
                        plsc.addupdate(y_v.at[pl.ds(ROW_WORDS + col, L)], acc[2 * i + 1])
                return carry

            lax.fori_loop(0, CH // RB, block, 0)

        def token_block(bi, carry):
            tok0 = base + bi * TBK
            pltpu.sync_copy(ids_hbm.at[pl.ds(tok0, TBK)], ids_v)
            pltpu.sync_copy(w_hbm.at[pl.ds(tok0, TBK)], w_v)
            gather(0, 0).start()

            def token_pair(pi, carry2):
                for slot in range(2):
                    tl = 2 * pi + slot
                    y_v = ys[slot]

                    @pl.when(pi > 0)
                    def _():
                        out_copy(tok0 + tl - 2, slot).wait()

                    zero = jnp.zeros((L,), jnp.float32)
                    for j in range(D_MODEL // L):
                        y_v[pl.ds(j * L, L)] = zero
                    gather(tl, 1).start()
                    gather(tl, 0).wait()
                    accumulate(tl, 0, y_v)

                    @pl.when(tl + 1 < TBK)
                    def _():
                        gather(tl + 1, 0).start()

                    gather(tl, 1).wait()
                    accumulate(tl, 1, y_v)
                    out_copy(tok0 + tl, slot).start()
                return carry2

            lax.fori_loop(0, TBK // 2, token_pair, 0)
            out_copy(tok0 + TBK - 2, 0).wait()
            out_copy(tok0 + TBK - 1, 1).wait()
            return carry

        lax.fori_loop(0, tpw // TBK, token_block, 0)

    return up_kernel(ids, w, tbl)


def peer_layer(h, g, w_query, sub_keys, down_rows, up_rows):
    T = h.shape[0]
    q, xn = norm_matmul(h, g, w_query, out_dtype=jnp.float32, emit_xn=True)
    idx_t, gate_t = peer_route(q, sub_keys.astype(jnp.bfloat16))
    idx = idx_t.T
    w_t = peer_down(idx, xn.reshape(T, SUBLANES, LANES), gate_t, down_rows)
    return peer_up_sc(idx // ROW_SUB, w_t.T, up_rows)


def _rms_kernel(x_ref, r_ref, g_ref, o_ref):
    x = x_ref[...] + r_ref[...]
    o_ref[...] = x * lax.rsqrt(jnp.mean(x * x, axis=-1, keepdims=True) + NORM_EPS) * g_ref[...]


def rms_norm_sum(x, r, g, *, tm=512):
    T, D = x.shape
    tm = min(tm, T)
    row = pl.BlockSpec((tm, D), lambda i: (i, 0))
    return pl.pallas_call(
        _rms_kernel,
        grid=(T // tm,),
        in_specs=[row, row, pl.BlockSpec((1, D), lambda i: (0, 0))],
        out_specs=row,
        out_shape=jax.ShapeDtypeStruct((T, D), jnp.float32),
        compiler_params=_cparams(("parallel",)),
        name="final_norm",
    )(x, r, g.reshape(1, D))


PIECE_SEQS = 2


def _piece_bounds(B):
    sizes = [PIECE_SEQS] * (B // PIECE_SEQS) + ([B % PIECE_SEQS] if B % PIECE_SEQS else [])
    if sizes[-1] > 1:
        sizes[-1:] = [sizes[-1] - sizes[-1] // 2, sizes[-1] // 2]
    starts = [sum(sizes[:i]) for i in range(len(sizes))]
    return list(zip(starts, sizes))


def kernel(x, rel_table, mix_norm_g, attn_w_in, attn_w_out, attn_sink, sg_w_in, sg_ln_g, sg_ln_b, sg_w_spatial, sg_b_spatial, sg_w_out, ffn_norm_g, peer_w_query, peer_sub_keys, peer_down, peer_up, final_norm_g):
    B, S, D = x.shape
    bf16 = jnp.bfloat16

    bias = attn_bias(rel_table)
    attn_in, attn_out = attn_w_in[0].astype(bf16), attn_w_out[0].astype(bf16)
    sg_in, sg_out = sg_w_in[0].astype(bf16), sg_w_out[0].astype(bf16)
    w_query = peer_w_query.astype(bf16)
    down_rows = [pack_table(peer_down, i, split_rows=True) for i in range(2)]
    up_rows = [pack_table(peer_up, i, split_rows=False) for i in range(2)]

    mid = []
    for b0, nb in _piece_bounds(B):
        h = x[b0:b0 + nb].reshape(nb * S, D)
        qkv, = norm_matmul(h, mix_norm_g[0], attn_in, out_dtype=bf16)
        att = window_attention(qkv.reshape(nb, S, QKV_DIM), bias, attn_sink[0])
        h = matmul_residual(att.reshape(nb * S, Q_DIM), attn_out, h)
        mid.append((h, peer_layer(h, ffn_norm_g[0], w_query[0], peer_sub_keys[0], down_rows[0], up_rows[0])))

    outs = []
    for h, y in mid:
        z, h = norm_matmul(h, mix_norm_g[1], sg_in, res=y, out_dtype=bf16, act=True, tn=1536)
        gated = spatial_gate(z, sg_ln_g[0], sg_ln_b[0], sg_w_spatial[0], sg_b_spatial[0])
        h = matmul_residual(gated, sg_out, h)
        y = peer_layer(h, ffn_norm_g[1], w_query[1], peer_sub_keys[1], down_rows[1], up_rows[1])
        outs.append(rms_norm_sum(h, y, final_norm_g).reshape(-1, S, D))
    return jnp.concatenate(outs, axis=0)
```

```python
import dataclasses
import functools
import math

import jax
import jax.numpy as jnp
from jax import lax
from jax.experimental import pallas as pl
from jax.experimental.pallas import tpu as pltpu
from jax.experimental.pallas import tpu_sc as plsc

D_MODEL = 1024
HEAD_DIM = 64
N_Q_HEADS = 16
N_KV_HEADS = 4
GQA_GROUP = 4
WINDOW = 128
BLOCK = 128
REL_BUCKETS = 32
REL_MAX_DIST = 128
Q_DIM = N_Q_HEADS * HEAD_DIM
KV_DIM = N_KV_HEADS * HEAD_DIM
QKV_DIM = Q_DIM + 2 * KV_DIM
CHUNK = 128
D_GATE = 3072
N_SG_GROUPS = 8
SG_GROUP_DIM = D_GATE // N_SG_GROUPS
N_KEYS = 128
PEER_HEADS = 8
PEER_TOPK = 16
D_HALF = 128
N_SLOTS = PEER_HEADS * PEER_TOPK
NORM_EPS = 1e-6
LN_EPS = 1e-5

LANES = 128
SUBLANES = 8
ROW_WORDS = D_MODEL // 2
ROW_SUB = ROW_WORDS // LANES
PLANE_STRIDE = 136
VMEM_LIMIT = 56 * 1024 * 1024

_GELU_C = math.sqrt(2.0 / math.pi)


def _gelu(x):
    return 0.5 * x * (1.0 + jnp.tanh(_GELU_C * (x + 0.044715 * (x * x * x))))


def _cparams(sem):
    return pltpu.CompilerParams(dimension_semantics=sem, vmem_limit_bytes=VMEM_LIMIT)


def _norm_matmul_kernel(*refs, act, emit_xn, add_res):
    refs = list(refs)
    x_ref = refs.pop(0)
    r_ref = refs.pop(0) if add_res else None
    g_ref, w_ref, o_ref = refs.pop(0), refs.pop(0), refs.pop(0)
    xn_ref = refs.pop(0) if emit_xn else None
    sum_ref = refs.pop(0) if add_res else None
    xs_ref = refs.pop(0)

    @pl.when(pl.program_id(1) == 0)
    def _():
        x = x_ref[...]
        if add_res:
            x = x + r_ref[...]
            sum_ref[...] = x
        y = x * lax.rsqrt(jnp.mean(x * x, axis=-1, keepdims=True) + NORM_EPS) * g_ref[...]
        xs_ref[...] = y.astype(xs_ref.dtype)
        if emit_xn:
            xn_ref[...] = y

    acc = jnp.dot(xs_ref[...], w_ref[...], preferred_element_type=jnp.float32)
    if act:
        acc = _gelu(acc)
    o_ref[...] = acc.astype(o_ref.dtype)


def norm_matmul(x, g, w, *, out_dtype, res=None, act=False, emit_xn=False, tm=512, tn=None):
    T, D = x.shape
    N = w.shape[1]
    tn = tn or N
    tm = min(tm, T)
    row = pl.BlockSpec((tm, D), lambda i, j: (i, 0))
    out_shape = [jax.ShapeDtypeStruct((T, N), out_dtype)]
    out_specs = [pl.BlockSpec((tm, tn), lambda i, j: (i, j))]
    for flag in (emit_xn, res is not None):
        if flag:
            out_shape.append(jax.ShapeDtypeStruct((T, D), jnp.float32))
            out_specs.append(row)
    args = [x] + ([res] if res is not None else []) + [g.reshape(1, D), w]
    in_specs = [row] * (len(args) - 2) + [pl.BlockSpec((1, D), lambda i, j: (0, 0)),
                                          pl.BlockSpec((D, tn), lambda i, j: (0, j))]
    return pl.pallas_call(
        functools.partial(_norm_matmul_kernel, act=act, emit_xn=emit_xn, add_res=res is not None),
        grid=(T // tm, N // tn),
        in_specs=in_specs,
        out_specs=out_specs,
        out_shape=out_shape,
        scratch_shapes=[pltpu.VMEM((tm, D), w.dtype)],
        compiler_params=_cparams(("parallel", "arbitrary")),
        name="norm_matmul",
    )(*args)


def _matmul_res_kernel(a_ref, w_ref, h_ref, o_ref):
    o_ref[...] = h_ref[...] + jnp.dot(a_ref[...], w_ref[...],
                                      preferred_element_type=jnp.float32)


def matmul_residual(a, w, h, *, tm=512):
    T, K = a.shape
    N = w.shape[1]
    tm = min(tm, T)
    return pl.pallas_call(
        _matmul_res_kernel,
        grid=(T // tm,),
        in_specs=[pl.BlockSpec((tm, K), lambda i: (i, 0)),
                  pl.BlockSpec((K, N), lambda i: (0, 0)),
                  pl.BlockSpec((tm, N), lambda i: (i, 0))],
        out_specs=pl.BlockSpec((tm, N), lambda i: (i, 0)),
        out_shape=jax.ShapeDtypeStruct((T, N), jnp.float32),
        compiler_params=_cparams(("parallel",)),
        name="matmul_residual",
    )(a, w, h)


def _t5_bucket(rel):
    nb = REL_BUCKETS // 2
    max_exact = nb // 2
    ret = jnp.where(rel > 0, nb, 0)
    n = jnp.abs(rel)
    nf = jnp.maximum(n, 1).astype(jnp.float32)
    large = max_exact + (jnp.log(nf / max_exact) / math.log(REL_MAX_DIST / max_exact)
                         * (nb - max_exact)).astype(jnp.int32)
    large = jnp.minimum(large, nb - 1)
    return (ret + jnp.where(n < max_exact, n, large)).astype(jnp.int32)


def _bias_kernel(bucket_ref, window_ref, table_ref, o_ref):
    bucket = bucket_ref[...]
    in_window = window_ref[...] > 0
    for hq in range(N_Q_HEADS):
        acc = jnp.zeros(bucket.shape, jnp.float32)
        for b in range(REL_BUCKETS):
            acc = jnp.where(bucket == b, table_ref[b, hq], acc)
        o_ref[hq] = jnp.where(in_window, acc, -jnp.inf)


def attn_bias(rel_table):
    qi = jnp.arange(BLOCK)[:, None]
    kj = jnp.arange(3 * BLOCK)[None, :]
    rel = kj - BLOCK - qi
    bucket = _t5_bucket(rel)
    window = (jnp.abs(rel) <= WINDOW).astype(jnp.int32)
    return pl.pallas_call(
        _bias_kernel,
        in_specs=[pl.BlockSpec(memory_space=pltpu.VMEM),
                  pl.BlockSpec(memory_space=pltpu.VMEM),
                  pl.BlockSpec(memory_space=pltpu.SMEM)],
        out_specs=pl.BlockSpec(memory_space=pltpu.VMEM),
        out_shape=jax.ShapeDtypeStruct((N_Q_HEADS, BLOCK, 3 * BLOCK), jnp.float32),
        name="attn_bias",
    )(bucket, window, rel_table)


def _attn_kernel(cur_ref, prev_ref, next_ref, bias_ref, sink_ref, o_ref):
    i = pl.program_id(1)
    nb = pl.num_programs(1)
    q = cur_ref[0, :, 0:Q_DIM] * (HEAD_DIM ** -0.5)
    kband = jnp.concatenate([prev_ref[0, :, 0:KV_DIM], cur_ref[0, :, Q_DIM:Q_DIM + KV_DIM],
                             next_ref[0, :, 0:KV_DIM]], axis=0)
    vband = jnp.concatenate([prev_ref[0, :, KV_DIM:2 * KV_DIM], cur_ref[0, :, Q_DIM + KV_DIM:QKV_DIM],
                             next_ref[0, :, KV_DIM:2 * KV_DIM]], axis=0)
    col = lax.broadcasted_iota(jnp.int32, (1, 3 * BLOCK), 1)
    valid = jnp.logical_and(jnp.logical_or(col >= BLOCK, i > 0),
                            jnp.logical_or(col < 2 * BLOCK, i < nb - 1))
    heads = range(N_Q_HEADS)
    kv = [(kband[:, hk * HEAD_DIM:(hk + 1) * HEAD_DIM], vband[:, hk * HEAD_DIM:(hk + 1) * HEAD_DIM])
          for hk in range(N_KV_HEADS)]
    s = []
    for hq in heads:
        qh = q[:, hq * HEAD_DIM:(hq + 1) * HEAD_DIM]
        sc = lax.dot_general(qh, kv[hq // GQA_GROUP][0], (((1,), (1,)), ((), ())),
                             preferred_element_type=jnp.float32)
        s.append(jnp.where(valid, sc + bias_ref[hq], -jnp.inf))
    mx = [jnp.maximum(jnp.max(s[hq], axis=-1, keepdims=True), sink_ref[hq]) for hq in heads]
    p = [jnp.exp(s[hq] - mx[hq]) for hq in heads]
    denom = [jnp.sum(p[hq], axis=-1, keepdims=True) + jnp.exp(sink_ref[hq] - mx[hq]) for hq in heads]
    outs = [jnp.dot((p[hq] / denom[hq]).astype(jnp.bfloat16), kv[hq // GQA_GROUP][1],
                    preferred_element_type=jnp.float32) for hq in heads]
    o_ref[0] = jnp.concatenate(outs, axis=-1).astype(o_ref.dtype)


def window_attention(qkv, bias, sink):
    B, S, _ = qkv.shape
    nb = S // BLOCK
    kv_col = Q_DIM // (2 * KV_DIM)
    return pl.pallas_call(
        _attn_kernel,
        grid=(B, nb),
        in_specs=[pl.BlockSpec((1, BLOCK, QKV_DIM), lambda b, i: (b, i, 0)),
                  pl.BlockSpec((1, BLOCK, 2 * KV_DIM),
                               lambda b, i: (b, jnp.maximum(i - 1, 0), kv_col)),
                  pl.BlockSpec((1, BLOCK, 2 * KV_DIM),
                               lambda b, i: (b, jnp.minimum(i + 1, nb - 1), kv_col)),
                  pl.BlockSpec((N_Q_HEADS, BLOCK, 3 * BLOCK), lambda b, i: (0, 0, 0)),
                  pl.BlockSpec(memory_space=pltpu.SMEM)],
        out_specs=pl.BlockSpec((1, BLOCK, Q_DIM), lambda b, i: (b, i, 0)),
        out_shape=jax.ShapeDtypeStruct((B, S, Q_DIM), jnp.bfloat16),
        compiler_params=_cparams(("parallel", "arbitrary")),
        name="window_attention",
    )(qkv, qkv, qkv, bias, sink)


def _spatial_gate_kernel(z_ref, g_ref, b_ref, wsp_ref, bsp_ref, o_ref):
    v = z_ref[:, D_GATE:2 * D_GATE].astype(jnp.float32)
    mu = jnp.mean(v, axis=-1, keepdims=True)
    vc = v - mu
    var = jnp.mean(vc * vc, axis=-1, keepdims=True)
    vn = (vc * lax.rsqrt(var + LN_EPS) * g_ref[...] + b_ref[...]).astype(jnp.bfloat16)
    for grp in range(N_SG_GROUPS):
        lo, hi = grp * SG_GROUP_DIM, (grp + 1) * SG_GROUP_DIM
        mixed = jnp.dot(wsp_ref[grp], vn[:, lo:hi], preferred_element_type=jnp.float32)
        mixed = mixed + bsp_ref[:, grp:grp + 1]
        u = z_ref[:, lo:hi].astype(jnp.float32)
        o_ref[:, lo:hi] = (u * mixed).astype(o_ref.dtype)


def spatial_gate(z, ln_g, ln_b, w_sp, b_sp):
    T = z.shape[0]
    return pl.pallas_call(
        _spatial_gate_kernel,
        grid=(T // CHUNK,),
        in_specs=[pl.BlockSpec((CHUNK, 2 * D_GATE), lambda i: (i, 0)),
                  pl.BlockSpec((1, D_GATE), lambda i: (0, 0)),
                  pl.BlockSpec((1, D_GATE), lambda i: (0, 0)),
                  pl.BlockSpec((N_SG_GROUPS, CHUNK, CHUNK), lambda i: (0, 0, 0)),
                  pl.BlockSpec((CHUNK, N_SG_GROUPS), lambda i: (0, 0))],
        out_specs=pl.BlockSpec((CHUNK, D_GATE), lambda i: (i, 0)),
        out_shape=jax.ShapeDtypeStruct((T, D_GATE), jnp.bfloat16),
        compiler_params=_cparams(("parallel",)),
        name="spatial_gate",
    )(z, ln_g.reshape(1, D_GATE), ln_b.reshape(1, D_GATE), w_sp.astype(jnp.bfloat16), b_sp.T)


def _oddeven_merge_sort_pairs(n):
    pairs = []
    p = 1
    while p < n:
        k = p
        while k >= 1:
            for j in range(k % p, n - k, 2 * k):
                for i in range(min(k, n - j - k)):
                    if (i + j) // (2 * p) == (i + j + k) // (2 * p):
                        pairs.append((i + j, i + j + k))
            k //= 2
        p *= 2
    return pairs


_SORT16 = _oddeven_merge_sort_pairs(N_KEYS // SUBLANES)


def _top16_of_keys(s):
    nv = N_KEYS // SUBLANES
    L = s.shape[1]
    sub = lax.broadcasted_iota(jnp.int32, (SUBLANES, L), 0)
    v = [s[j * SUBLANES:(j + 1) * SUBLANES] for j in range(nv)]
    ids = [sub + j * SUBLANES for j in range(nv)]
    for i, j in _SORT16:
        swap = jnp.logical_or(v[j] > v[i], jnp.logical_and(v[j] == v[i], ids[j] < ids[i]))
        v[i], v[j] = jnp.where(swap, v[j], v[i]), jnp.where(swap, v[i], v[j])
        ids[i], ids[j] = jnp.where(swap, ids[j], ids[i]), jnp.where(swap, ids[i], ids[j])
    vals, picks = [], []
    for it in range(PEER_TOPK):
        m = jnp.max(v[0], axis=0, keepdims=True)
        am = jnp.min(jnp.where(v[0] == m, ids[0], N_KEYS), axis=0, keepdims=True)
        hit = ids[0] == am
        vals.append(m)
        picks.append(am)
        last = PEER_TOPK - 1 - it
        for j in range(last):
            v[j] = jnp.where(hit, v[j + 1], v[j])
            ids[j] = jnp.where(hit, ids[j + 1], ids[j])
        v[last] = jnp.where(hit, -jnp.inf, v[last])
    return jnp.concatenate(vals, axis=0), jnp.concatenate(picks, axis=0)


_CAND_ROWS = (
    [(a, 0) for a in range(16)]
    + [None] + [(0, b) for b in range(1, 16)]
    + [None] + [(a, 1) for a in range(1, 8)]
    + [None, None] + [(1, b) for b in range(2, 8)]
    + [(2, 2), (3, 2), (4, 2), (2, 3), (2, 4), (3, 3), None, None]
)
assert sorted(x for x in _CAND_ROWS if x) == sorted(
    (a, b) for a in range(16) for b in range(16) if (a + 1) * (b + 1) <= 16)


def _rows(x, picks):
    pieces, i = [], 0
    while i < len(picks):
        j = i
        while j + 1 < len(picks) and picks[j + 1] == picks[j] + 1:
            j += 1
        pieces.append(x[picks[i]:picks[j] + 1])
        i = j + 1
    return pieces[0] if len(pieces) == 1 else jnp.concatenate(pieces, axis=0)


def _cand_positions(L):
    row = lax.broadcasted_iota(jnp.int32, (len(_CAND_ROWS), L), 0)
    pos = jnp.full(row.shape, _PAD_POS, jnp.int32)
    for r, c in enumerate(_CAND_ROWS):
        if c is not None:
            pos = jnp.where(row == r, c[0] * PEER_TOPK + c[1], pos)
    return pos


_PAD_POS = 1 << 20


def _joint_top16(v1, i1, v2, i2, pos):
    a_of = [c[0] if c else 0 for c in _CAND_ROWS]
    b_of = [c[1] if c else 0 for c in _CAND_ROWS]
    cand = jnp.where(pos < _PAD_POS, _rows(v1, a_of) + _rows(v2, b_of), -jnp.inf)
    cidx = _rows(i1, a_of) * N_KEYS + _rows(i2, b_of)
    vals, picks = [], []
    for _ in range(PEER_TOPK):
        m = jnp.max(cand, axis=0, keepdims=True)
        pm = jnp.min(jnp.where(cand == m, pos, 1 << 21), axis=0, keepdims=True)
        hit = pos == pm
        vals.append(m)
        picks.append(jnp.max(jnp.where(hit, cidx, -1), axis=0, keepdims=True))
        cand = jnp.where(hit, -jnp.inf, cand)
    return jnp.concatenate(vals, axis=0), jnp.concatenate(picks, axis=0)


def _route_kernel(q_ref, keys_ref, idx_ref, gate_ref):
    nt = (((1,), (1,)), ((), ()))
    pos = _cand_positions(LANES)
    for j in range(q_ref.shape[0] // LANES):
        q = q_ref[j * LANES:(j + 1) * LANES, :]
        s1 = lax.dot_general(keys_ref[0, 0], q[:, 0:D_HALF], nt, preferred_element_type=jnp.float32)
        s2 = lax.dot_general(keys_ref[0, 1], q[:, D_HALF:2 * D_HALF], nt, preferred_element_type=jnp.float32)
        v1, i1 = _top16_of_keys(s1)
        v2, i2 = _top16_of_keys(s2)
        top_s, top_i = _joint_top16(v1, i1, v2, i2, pos)
        e = jnp.exp(top_s - top_s[0:1])
        gate_ref[:, j * LANES:(j + 1) * LANES] = e / jnp.sum(e, axis=0, keepdims=True)
        idx_ref[:, j * LANES:(j + 1) * LANES] = top_i * ROW_SUB


def peer_route(q, sub_keys, *, tb=1024):
    T = q.shape[0]
    tb = min(tb, T)
    return pl.pallas_call(
        _route_kernel,
        grid=(T // tb, PEER_HEADS),
        in_specs=[pl.BlockSpec((tb, 2 * D_HALF), lambda i, h: (i, h)),
                  pl.BlockSpec((1, 2, N_KEYS, D_HALF), lambda i, h: (h, 0, 0, 0))],
        out_specs=[pl.BlockSpec((PEER_TOPK, tb), lambda i, h: (h, i)),
                   pl.BlockSpec((PEER_TOPK, tb), lambda i, h: (h, i))],
        out_shape=[jax.ShapeDtypeStruct((N_SLOTS, T), jnp.int32),
                   jax.ShapeDtypeStruct((N_SLOTS, T), jnp.float32)],
        compiler_params=_cparams(("parallel", "arbitrary")),
        name="peer_route",
    )(q, sub_keys)


def _pack_kernel(x_ref, o_ref, *, split_rows):
    x = x_ref[...]
    half = x.shape[1] // 2

    def bf16_bits(v):
        return pltpu.bitcast(v.astype(jnp.bfloat16).astype(jnp.float32), jnp.uint32)

    words = (bf16_bits(x[:, :half]) >> 16) | (bf16_bits(x[:, half:]) & jnp.uint32(0xFFFF0000))
    if split_rows:
        n = x.shape[0]
        for c in range(ROW_SUB):
            o_ref[pl.ds(c, n, stride=ROW_SUB), :] = words[:, c * LANES:(c + 1) * LANES]
    else:
        o_ref[...] = words


def pack_table(tbls, layer, *, split_rows, te=512):
    _, E, D = tbls.shape
    out_block, out_shape = ((te * ROW_SUB, LANES), (E * ROW_SUB, LANES)) if split_rows else ((te, D // 2), (E, D // 2))
    return pl.pallas_call(
        functools.partial(_pack_kernel, split_rows=split_rows),
        grid=(E // te,),
        in_specs=[pl.BlockSpec((None, te, D), lambda i: (layer, i, 0))],
        out_specs=pl.BlockSpec(out_block, lambda i: (i, 0)),
        out_shape=jax.ShapeDtypeStruct(out_shape, jnp.uint32),
        compiler_params=_cparams(("parallel",)),
        name="pack_table",
    )(tbls)


def _unpack(words):
    lo = pltpu.bitcast(words << 16, jnp.float32)
    hi = pltpu.bitcast(words & jnp.uint32(0xFFFF0000), jnp.float32)
    return lo, hi


def _gather_rows(idx_ref, tbl_ref, t, planes_ref):
    for k in range(N_SLOTS):
        off = pl.multiple_of(idx_ref[t, k], ROW_SUB)
        planes_ref[pl.ds(k, ROW_SUB, stride=PLANE_STRIDE), :] = tbl_ref[pl.ds(off, ROW_SUB), :]


def _plane(planes_ref, c):
    return _unpack(planes_ref[c * PLANE_STRIDE:c * PLANE_STRIDE + N_SLOTS, :])


def _peer_down_kernel(idx_ref, x_ref, gate_ref, tbl_ref, o_ref, planes_a, planes_b, part_ref):
    tb = x_ref.shape[0]
    lane = lax.broadcasted_iota(jnp.int32, (N_SLOTS, tb), 1)

    def compute(t, planes_ref, slot):
        acc = jnp.zeros((N_SLOTS, LANES), jnp.float32)
        for c in range(ROW_SUB):
            lo, hi = _plane(planes_ref, c)
            acc = acc + lo * x_ref[t, c:c + 1, :] + hi * x_ref[t, ROW_SUB + c:ROW_SUB + c + 1, :]
        part_ref[slot] = acc

    def place_pair(t0):
        col_a = jnp.sum(part_ref[0], axis=1, keepdims=True)
        col_b = jnp.sum(part_ref[1], axis=1, keepdims=True)
        o_ref[...] = jnp.where(lane == t0 - 2, col_a, jnp.where(lane == t0 - 1, col_b, o_ref[...]))

    o_ref[...] = jnp.zeros(o_ref.shape, jnp.float32)
    part_ref[...] = jnp.zeros(part_ref.shape, jnp.float32)
    _gather_rows(idx_ref, tbl_ref, 0, planes_a)

    def pair(i, carry):
        t0 = 2 * i
        place_pair(t0)
        _gather_rows(idx_ref, tbl_ref, t0 + 1, planes_b)
        compute(t0, planes_a, 0)
        _gather_rows(idx_ref, tbl_ref, jnp.minimum(t0 + 2, tb - 1), planes_a)
        compute(t0 + 1, planes_b, 1)
        return carry

    lax.fori_loop(0, tb // 2, pair, 0)
    place_pair(tb)
    o_ref[...] = _gelu(o_ref[...]) * gate_ref[...]


def peer_down(idx, xn3, gate_t, tbl, *, tb=128):
    T = xn3.shape[0]
    tb = min(tb, T)
    assert tb % 2 == 0 and T % tb == 0
    planes = pltpu.VMEM((ROW_SUB * PLANE_STRIDE, LANES), jnp.uint32)
    return pl.pallas_call(
        _peer_down_kernel,
        grid=(T // tb,),
        in_specs=[pl.BlockSpec((tb, N_SLOTS), lambda i: (i, 0), memory_space=pltpu.SMEM),
                  pl.BlockSpec((tb, SUBLANES, LANES), lambda i: (i, 0, 0)),
                  pl.BlockSpec((N_SLOTS, tb), lambda i: (0, i)),
                  pl.BlockSpec(tbl.shape, lambda i: (0, 0), pipeline_mode=pl.Buffered(1))],
        out_specs=pl.BlockSpec((N_SLOTS, tb), lambda i: (0, i)),
        out_shape=jax.ShapeDtypeStruct((N_SLOTS, T), jnp.float32),
        scratch_shapes=[planes, planes, pltpu.VMEM((2, N_SLOTS, LANES), jnp.float32)],
        compiler_params=_cparams(("arbitrary",)),
        name="peer_down",
    )(idx, xn3, gate_t, tbl)


SC_CORES = 2
SC_SUBCORES = 16
SC_WORKERS = SC_CORES * SC_SUBCORES
SC_LANES = 16
SC_GATHER_ROWS = N_SLOTS // 2
SC_ROW_BLOCK = 8
SC_TOKEN_BLOCK = 32
SC_COL_GROUP = 8


def _sc_params():
    cp = pltpu.CompilerParams()
    if "needs_layout_passes" in pltpu.CompilerParams.__dataclass_fields__:
        cp = dataclasses.replace(cp, needs_layout_passes=False)
    return cp


def peer_up_sc(ids, w, tbl):
    T = ids.shape[0]
    L, CH, RB, TBK = SC_LANES, SC_GATHER_ROWS, SC_ROW_BLOCK, SC_TOKEN_BLOCK
    assert T % (SC_WORKERS * TBK) == 0
    tpw = T // SC_WORKERS
    mesh = plsc.VectorSubcoreMesh(core_axis_name="c", subcore_axis_name="s")
    rows_buf = pltpu.VMEM((CH, ROW_WORDS), jnp.uint32)
    y_buf = pltpu.VMEM((D_MODEL,), jnp.float32)

    @functools.partial(
        pl.kernel, mesh=mesh, compiler_params=_sc_params(),
        out_type=jax.ShapeDtypeStruct((T, D_MODEL), jnp.float32),
        scratch_types=[pltpu.VMEM((TBK, N_SLOTS), jnp.int32), pltpu.VMEM((TBK, N_SLOTS), jnp.float32),
                       rows_buf, rows_buf, y_buf, y_buf] + [pltpu.SemaphoreType.DMA] * 4,
    )
    def up_kernel(ids_hbm, w_hbm, tbl_hbm, out_hbm, ids_v, w_v, rows0, rows1, y0, y1, g0, g1, o0, o1):
        base = (lax.axis_index("s") * SC_CORES + lax.axis_index("c")) * tpw
        rows, gsem, ys, osem = (rows0, rows1), (g0, g1), (y0, y1), (o0, o1)

        def gather(tl, half):
            return pltpu.make_async_copy(tbl_hbm.at[ids_v.at[tl, pl.ds(half * CH, CH)]], rows[half], gsem[half])

        def out_copy(tok, slot):
            return pltpu.make_async_copy(ys[slot], out_hbm.at[tok], osem[slot])

        def accumulate(tl, half, y_v):
            rows_v = rows[half]

            def block(b, carry):
                r0 = b * RB
                tlv = jnp.full((L,), tl, jnp.int32)
                wks = [plsc.load_gather(w_v, [tlv, jnp.full((L,), half * CH + r0 + r, jnp.int32)])
                       for r in range(RB)]
                for q in range(ROW_WORDS // (SC_COL_GROUP * L)):
                    acc = [None] * (2 * SC_COL_GROUP)
                    for r in range(RB):
                        for i in range(SC_COL_GROUP):
                            v = rows_v[r0 + r, pl.ds((q * SC_COL_GROUP + i) * L, L)]
                            lo = plsc.bitcast(v << 16, jnp.float32) * wks[r]
                            hi = plsc.bitcast(v & jnp.uint32(0xFFFF0000), jnp.float32) * wks[r]
                            acc[2 * i] = lo if r == 0 else acc[2 * i] + lo
                            acc[2 * i + 1] = hi if r == 0 else acc[2 * i + 1] + hi
                    for i in range(SC_COL_GROUP):
                        col = (q * SC_COL_GROUP + i) * L
                        plsc.addupdate(y_v.at[pl.ds(col, L)], acc[2 * i])
                        plsc.addupdate(y_v.at[pl.ds(ROW_WORDS + col, L)], acc[2 * i + 1])
                return carry

            lax.fori_loop(0, CH // RB, block, 0)

        def token_block(bi, carry):
            tok0 = base + bi * TBK
            pltpu.sync_copy(ids_hbm.at[pl.ds(tok0, TBK)], ids_v)
            pltpu.sync_copy(w_hbm.at[pl.ds(tok0, TBK)], w_v)
            gather(0, 0).start()

            def token_pair(pi, carry2):
                for slot in range(2):
                    tl = 2 * pi + slot
                    y_v = ys[slot]

                    @pl.when(pi > 0)
                    def _():
                        out_copy(tok0 + tl - 2, slot).wait()

                    zero = jnp.zeros((L,), jnp.float32)
                    for j in range(D_MODEL // L):
                        y_v[pl.ds(j * L, L)] = zero
                    gather(tl, 1).start()
                    gather(tl, 0).wait()
                    accumulate(tl, 0, y_v)

                    @pl.when(tl + 1 < TBK)
                    def _():
                        gather(tl + 1, 0).start()

                    gather(tl, 1).wait()
                    accumulate(tl, 1, y_v)
                    out_copy(tok0 + tl, slot).start()
                return carry2

            lax.fori_loop(0, TBK // 2, token_pair, 0)
            out_copy(tok0 + TBK - 2, 0).wait()
            out_copy(tok0 + TBK - 1, 1).wait()
            return carry

        lax.fori_loop(0, tpw // TBK, token_block, 0)

    return up_kernel(ids, w, tbl)


def peer_layer(h, g, w_query, sub_keys, down_rows, up_rows):
    T = h.shape[0]
    q, xn = norm_matmul(h, g, w_query, out_dtype=jnp.bfloat16, emit_xn=True)
    idx_t, gate_t = peer_route(q, sub_keys.astype(jnp.bfloat16))
    idx = idx_t.T
    w_t = peer_down(idx, xn.reshape(T, SUBLANES, LANES), gate_t, down_rows)
    return peer_up_sc(idx // ROW_SUB, w_t.T, up_rows)


def _rms_kernel(x_ref, r_ref, g_ref, o_ref):
    x = x_ref[...] + r_ref[...]
    o_ref[...] = x * lax.rsqrt(jnp.mean(x * x, axis=-1, keepdims=True) + NORM_EPS) * g_ref[...]


def rms_norm_sum(x, r, g, *, tm=512):
    T, D = x.shape
    tm = min(tm, T)
    row = pl.BlockSpec((tm, D), lambda i: (i, 0))
    return pl.pallas_call(
        _rms_kernel,
        grid=(T // tm,),
        in_specs=[row, row, pl.BlockSpec((1, D), lambda i: (0, 0))],
        out_specs=row,
        out_shape=jax.ShapeDtypeStruct((T, D), jnp.float32),
        compiler_params=_cparams(("parallel",)),
        name="final_norm",
    )(x, r, g.reshape(1, D))


PIECE_SEQS = 2


def _piece_bounds(B):
    sizes = [PIECE_SEQS] * (B // PIECE_SEQS) + ([B % PIECE_SEQS] if B % PIECE_SEQS else [])
    if sizes[-1] > 1:
        sizes[-1:] = [sizes[-1] - sizes[-1] // 2, sizes[-1] // 2]
    starts = [sum(sizes[:i]) for i in range(len(sizes))]
    return list(zip(starts, sizes))


def kernel(x, rel_table, mix_norm_g, attn_w_in, attn_w_out, attn_sink, sg_w_in, sg_ln_g, sg_ln_b, sg_w_spatial, sg_b_spatial, sg_w_out, ffn_norm_g, peer_w_query, peer_sub_keys, peer_down, peer_up, final_norm_g):
    B, S, D = x.shape
    bf16 = jnp.bfloat16

    bias = attn_bias(rel_table)
    attn_in, attn_out = attn_w_in[0].astype(bf16), attn_w_out[0].astype(bf16)
    sg_in, sg_out = sg_w_in[0].astype(bf16), sg_w_out[0].astype(bf16)
    w_query = peer_w_query.astype(bf16)
    down_rows = [pack_table(peer_down, i, split_rows=True) for i in range(2)]
    up_rows = [pack_table(peer_up, i, split_rows=False) for i in range(2)]

    mid = []
    for b0, nb in _piece_bounds(B):
        h = x[b0:b0 + nb].reshape(nb * S, D)
        qkv, = norm_matmul(h, mix_norm_g[0], attn_in, out_dtype=bf16)
        att = window_attention(qkv.reshape(nb, S, QKV_DIM), bias, attn_sink[0])
        h = matmul_residual(att.reshape(nb * S, Q_DIM), attn_out, h)
        mid.append((h, peer_layer(h, ffn_norm_g[0], w_query[0], peer_sub_keys[0], down_rows[0], up_rows[0])))

    outs = []
    for h, y in mid:
        z, h = norm_matmul(h, mix_norm_g[1], sg_in, res=y, out_dtype=bf16, act=True, tn=1536)
        gated = spatial_gate(z, sg_ln_g[0], sg_ln_b[0], sg_w_spatial[0], sg_b_spatial[0])
        h = matmul_residual(gated, sg_out, h)
        y = peer_layer(h, ffn_norm_g[1], w_query[1], peer_sub_keys[1], down_rows[1], up_rows[1])
        outs.append(rms_norm_sum(h, y, final_norm_g).reshape(-1, S, D))
    return jnp.concatenate(outs, axis=0)
```

```python
import dataclasses
import functools
import math

import jax
import jax.numpy as jnp
from jax import lax
from jax.experimental import pallas as pl
from jax.experimental.pallas import tpu as pltpu
from jax.experimental.pallas import tpu_sc as plsc

D_MODEL = 1024
HEAD_DIM = 64
N_Q_HEADS = 16
N_KV_HEADS = 4
GQA_GROUP = 4
WINDOW = 128
BLOCK = 128
REL_BUCKETS = 32
REL_MAX_DIST = 128
Q_DIM = N_Q_HEADS * HEAD_DIM
KV_DIM = N_KV_HEADS * HEAD_DIM
QKV_DIM = Q_DIM + 2 * KV_DIM
CHUNK = 128
D_GATE = 3072
N_SG_GROUPS = 8
SG_GROUP_DIM = D_GATE // N_SG_GROUPS
N_KEYS = 128
PEER_HEADS = 8
PEER_TOPK = 16
D_HALF = 128
N_SLOTS = PEER_HEADS * PEER_TOPK
NORM_EPS = 1e-6
LN_EPS = 1e-5

LANES = 128
SUBLANES = 8
ROW_WORDS = D_MODEL // 2
ROW_SUB = ROW_WORDS // LANES
PLANE_STRIDE = 136
VMEM_LIMIT = 56 * 1024 * 1024

_GELU_C = math.sqrt(2.0 / math.pi)


def _gelu(x):
    return 0.5 * x * (1.0 + jnp.tanh(_GELU_C * (x + 0.044715 * (x * x * x))))


def _cparams(sem):
    return pltpu.CompilerParams(dimension_semantics=sem, vmem_limit_bytes=VMEM_LIMIT)


def _norm_matmul_kernel(*refs, act, emit_xn, add_res):
    refs = list(refs)
    x_ref = refs.pop(0)
    r_ref = refs.pop(0) if add_res else None
    g_ref, w_ref, o_ref = refs.pop(0), refs.pop(0), refs.pop(0)
    xn_ref = refs.pop(0) if emit_xn else None
    sum_ref = refs.pop(0) if add_res else None
    xs_ref = refs.pop(0)

    @pl.when(pl.program_id(1) == 0)
    def _():
        x = x_ref[...]
        if add_res:
            x = x + r_ref[...]
            sum_ref[...] = x
        y = x * lax.rsqrt(jnp.mean(x * x, axis=-1, keepdims=True) + NORM_EPS) * g_ref[...]
        xs_ref[...] = y.astype(xs_ref.dtype)
        if emit_xn:
            xn_ref[...] = y

    acc = jnp.dot(xs_ref[...], w_ref[...], preferred_element_type=jnp.float32)
    if act:
        acc = _gelu(acc)
    o_ref[...] = acc.astype(o_ref.dtype)


def norm_matmul(x, g, w, *, out_dtype, res=None, act=False, emit_xn=False, tm=512, tn=None):
    T, D = x.shape
    N = w.shape[1]
    tn = tn or N
    tm = min(tm, T)
    row = pl.BlockSpec((tm, D), lambda i, j: (i, 0))
    out_shape = [jax.ShapeDtypeStruct((T, N), out_dtype)]
    out_specs = [pl.BlockSpec((tm, tn), lambda i, j: (i, j))]
    for flag in (emit_xn, res is not None):
        if flag:
            out_shape.append(jax.ShapeDtypeStruct((T, D), jnp.float32))
            out_specs.append(row)
    args = [x] + ([res] if res is not None else []) + [g.reshape(1, D), w]
    in_specs = [row] * (len(args) - 2) + [pl.BlockSpec((1, D), lambda i, j: (0, 0)),
                                          pl.BlockSpec((D, tn), lambda i, j: (0, j))]
    return pl.pallas_call(
        functools.partial(_norm_matmul_kernel, act=act, emit_xn=emit_xn, add_res=res is not None),
        grid=(T // tm, N // tn),
        in_specs=in_specs,
        out_specs=out_specs,
        out_shape=out_shape,
        scratch_shapes=[pltpu.VMEM((tm, D), w.dtype)],
        compiler_params=_cparams(("parallel", "arbitrary")),
        name="norm_matmul",
    )(*args)


def _matmul_res_kernel(a_ref, w_ref, h_ref, o_ref):
    o_ref[...] = h_ref[...] + jnp.dot(a_ref[...], w_ref[...],
                                      preferred_element_type=jnp.float32)


def matmul_residual(a, w, h, *, tm=512):
    T, K = a.shape
    N = w.shape[1]
    tm = min(tm, T)
    return pl.pallas_call(
        _matmul_res_kernel,
        grid=(T // tm,),
        in_specs=[pl.BlockSpec((tm, K), lambda i: (i, 0)),
                  pl.BlockSpec((K, N), lambda i: (0, 0)),
                  pl.BlockSpec((tm, N), lambda i: (i, 0))],
        out_specs=pl.BlockSpec((tm, N), lambda i: (i, 0)),
        out_shape=jax.ShapeDtypeStruct((T, N), jnp.float32),
        compiler_params=_cparams(("parallel",)),
        name="matmul_residual",
    )(a, w, h)


def _t5_bucket(rel):
    nb = REL_BUCKETS // 2
    max_exact = nb // 2
    ret = jnp.where(rel > 0, nb, 0)
    n = jnp.abs(rel)
    nf = jnp.maximum(n, 1).astype(jnp.float32)
    large = max_exact + (jnp.log(nf / max_exact) / math.log(REL_MAX_DIST / max_exact)
                         * (nb - max_exact)).astype(jnp.int32)
    large = jnp.minimum(large, nb - 1)
    return (ret + jnp.where(n < max_exact, n, large)).astype(jnp.int32)


def _bias_kernel(bucket_ref, window_ref, table_ref, o_ref):
    bucket = bucket_ref[...]
    in_window = window_ref[...] > 0
    for hq in range(N_Q_HEADS):
        acc = jnp.zeros(bucket.shape, jnp.float32)
        for b in range(REL_BUCKETS):
            acc = jnp.where(bucket == b, table_ref[b, hq], acc)
        o_ref[hq] = jnp.where(in_window, acc, -jnp.inf)


def attn_bias(rel_table):
    qi = jnp.arange(BLOCK)[:, None]
    kj = jnp.arange(3 * BLOCK)[None, :]
    rel = kj - BLOCK - qi
    bucket = _t5_bucket(rel)
    window = (jnp.abs(rel) <= WINDOW).astype(jnp.int32)
    return pl.pallas_call(
        _bias_kernel,
        in_specs=[pl.BlockSpec(memory_space=pltpu.VMEM),
                  pl.BlockSpec(memory_space=pltpu.VMEM),
                  pl.BlockSpec(memory_space=pltpu.SMEM)],
        out_specs=pl.BlockSpec(memory_space=pltpu.VMEM),
        out_shape=jax.ShapeDtypeStruct((N_Q_HEADS, BLOCK, 3 * BLOCK), jnp.float32),
        name="attn_bias",
    )(bucket, window, rel_table)


def _attn_kernel(cur_ref, prev_ref, next_ref, bias_ref, sink_ref, o_ref):
    i = pl.program_id(1)
    nb = pl.num_programs(1)
    q = cur_ref[0, :, 0:Q_DIM] * (HEAD_DIM ** -0.5)
    kband = jnp.concatenate([prev_ref[0, :, 0:KV_DIM], cur_ref[0, :, Q_DIM:Q_DIM + KV_DIM],
                             next_ref[0, :, 0:KV_DIM]], axis=0)
    vband = jnp.concatenate([prev_ref[0, :, KV_DIM:2 * KV_DIM], cur_ref[0, :, Q_DIM + KV_DIM:QKV_DIM],
                             next_ref[0, :, KV_DIM:2 * KV_DIM]], axis=0)
    col = lax.broadcasted_iota(jnp.int32, (1, 3 * BLOCK), 1)
    valid = jnp.logical_and(jnp.logical_or(col >= BLOCK, i > 0),
                            jnp.logical_or(col < 2 * BLOCK, i < nb - 1))
    heads = range(N_Q_HEADS)
    kv = [(kband[:, hk * HEAD_DIM:(hk + 1) * HEAD_DIM], vband[:, hk * HEAD_DIM:(hk + 1) * HEAD_DIM])
          for hk in range(N_KV_HEADS)]
    s = []
    for hq in heads:
        qh = q[:, hq * HEAD_DIM:(hq + 1) * HEAD_DIM]
        sc = lax.dot_general(qh, kv[hq // GQA_GROUP][0], (((1,), (1,)), ((), ())),
                             preferred_element_type=jnp.float32)
        s.append(jnp.where(valid, sc + bias_ref[hq], -jnp.inf))
    mx = [jnp.maximum(jnp.max(s[hq], axis=-1, keepdims=True), sink_ref[hq]) for hq in heads]
    p = [jnp.exp(s[hq] - mx[hq]) for hq in heads]
    denom = [jnp.sum(p[hq], axis=-1, keepdims=True) + jnp.exp(sink_ref[hq] - mx[hq]) for hq in heads]
    outs = [jnp.dot((p[hq] / denom[hq]).astype(jnp.bfloat16), kv[hq // GQA_GROUP][1],
                    preferred_element_type=jnp.float32) for hq in heads]
    o_ref[0] = jnp.concatenate(outs, axis=-1).astype(o_ref.dtype)


def window_attention(qkv, bias, sink):
    B, S, _ = qkv.shape
    nb = S // BLOCK
    kv_col = Q_DIM // (2 * KV_DIM)
    return pl.pallas_call(
        _attn_kernel,
        grid=(B, nb),
        in_specs=[pl.BlockSpec((1, BLOCK, QKV_DIM), lambda b, i: (b, i, 0)),
                  pl.BlockSpec((1, BLOCK, 2 * KV_DIM),
                               lambda b, i: (b, jnp.maximum(i - 1, 0), kv_col)),
                  pl.BlockSpec((1, BLOCK, 2 * KV_DIM),
                               lambda b, i: (b, jnp.minimum(i + 1, nb - 1), kv_col)),
                  pl.BlockSpec((N_Q_HEADS, BLOCK, 3 * BLOCK), lambda b, i: (0, 0, 0)),
                  pl.BlockSpec(memory_space=pltpu.SMEM)],
        out_specs=pl.BlockSpec((1, BLOCK, Q_DIM), lambda b, i: (b, i, 0)),
        out_shape=jax.ShapeDtypeStruct((B, S, Q_DIM), jnp.bfloat16),
        compiler_params=_cparams(("parallel", "arbitrary")),
        name="window_attention",
    )(qkv, qkv, qkv, bias, sink)


def _spatial_gate_kernel(z_ref, g_ref, b_ref, wsp_ref, bsp_ref, o_ref):
    v = z_ref[:, D_GATE:2 * D_GATE].astype(jnp.float32)
    mu = jnp.mean(v, axis=-1, keepdims=True)
    vc = v - mu
    var = jnp.mean(vc * vc, axis=-1, keepdims=True)
    vn = (vc * lax.rsqrt(var + LN_EPS) * g_ref[...] + b_ref[...]).astype(jnp.bfloat16)
    for grp in range(N_SG_GROUPS):
        lo, hi = grp * SG_GROUP_DIM, (grp + 1) * SG_GROUP_DIM
        mixed = jnp.dot(wsp_ref[grp], vn[:, lo:hi], preferred_element_type=jnp.float32)
        mixed = mixed + bsp_ref[:, grp:grp + 1]
        u = z_ref[:, lo:hi].astype(jnp.float32)
        o_ref[:, lo:hi] = (u * mixed).astype(o_ref.dtype)


def spatial_gate(z, ln_g, ln_b, w_sp, b_sp):
    T = z.shape[0]
    return pl.pallas_call(
        _spatial_gate_kernel,
        grid=(T // CHUNK,),
        in_specs=[pl.BlockSpec((CHUNK, 2 * D_GATE), lambda i: (i, 0)),
                  pl.BlockSpec((1, D_GATE), lambda i: (0, 0)),
                  pl.BlockSpec((1, D_GATE), lambda i: (0, 0)),
                  pl.BlockSpec((N_SG_GROUPS, CHUNK, CHUNK), lambda i: (0, 0, 0)),
                  pl.BlockSpec((CHUNK, N_SG_GROUPS), lambda i: (0, 0))],
        out_specs=pl.BlockSpec((CHUNK, D_GATE), lambda i: (i, 0)),
        out_shape=jax.ShapeDtypeStruct((T, D_GATE), jnp.bfloat16),
        compiler_params=_cparams(("parallel",)),
        name="spatial_gate",
    )(z, ln_g.reshape(1, D_GATE), ln_b.reshape(1, D_GATE), w_sp.astype(jnp.bfloat16), b_sp.T)


def _oddeven_merge_sort_pairs(n):
    pairs = []
    p = 1
    while p < n:
        k = p
        while k >= 1:
            for j in range(k % p, n - k, 2 * k):
                for i in range(min(k, n - j - k)):
                    if (i + j) // (2 * p) == (i + j + k) // (2 * p):
                        pairs.append((i + j, i + j + k))
            k //= 2
        p *= 2
    return pairs


_SORT16 = _oddeven_merge_sort_pairs(N_KEYS // SUBLANES)


def _top16_of_keys(s):
    nv = N_KEYS // SUBLANES
    L = s.shape[1]
    sub = lax.broadcasted_iota(jnp.int32, (SUBLANES, L), 0)
    v = [s[j * SUBLANES:(j + 1) * SUBLANES] for j in range(nv)]
    ids = [sub + j * SUBLANES for j in range(nv)]
    for i, j in _SORT16:
        swap = jnp.logical_or(v[j] > v[i], jnp.logical_and(v[j] == v[i], ids[j] < ids[i]))
        v[i], v[j] = jnp.where(swap, v[j], v[i]), jnp.where(swap, v[i], v[j])
        ids[i], ids[j] = jnp.where(swap, ids[j], ids[i]), jnp.where(swap, ids[i], ids[j])
    vals, picks = [], []
    for it in range(PEER_TOPK):
        m = jnp.max(v[0], axis=0, keepdims=True)
        am = jnp.min(jnp.where(v[0] == m, ids[0], N_KEYS), axis=0, keepdims=True)
        hit = ids[0] == am
        vals.append(m)
        picks.append(am)
        last = PEER_TOPK - 1 - it
        for j in range(last):
            v[j] = jnp.where(hit, v[j + 1], v[j])
            ids[j] = jnp.where(hit, ids[j + 1], ids[j])
        v[last] = jnp.where(hit, -jnp.inf, v[last])
    return jnp.concatenate(vals, axis=0), jnp.concatenate(picks, axis=0)


_CAND_ROWS = (
    [(a, 0) for a in range(16)]
    + [None] + [(0, b) for b in range(1, 16)]
    + [None] + [(a, 1) for a in range(1, 8)]
    + [None, None] + [(1, b) for b in range(2, 8)]
    + [(2, 2), (3, 2), (4, 2), (2, 3), (2, 4), (3, 3), None, None]
)
assert sorted(x for x in _CAND_ROWS if x) == sorted(
    (a, b) for a in range(16) for b in range(16) if (a + 1) * (b + 1) <= 16)


def _rows(x, picks):
    pieces, i = [], 0
    while i < len(picks):
        j = i
        while j + 1 < len(picks) and picks[j + 1] == picks[j] + 1:
            j += 1
        pieces.append(x[picks[i]:picks[j] + 1])
        i = j + 1
    return pieces[0] if len(pieces) == 1 else jnp.concatenate(pieces, axis=0)


def _cand_positions(L):
    row = lax.broadcasted_iota(jnp.int32, (len(_CAND_ROWS), L), 0)
    pos = jnp.full(row.shape, _PAD_POS, jnp.int32)
    for r, c in enumerate(_CAND_ROWS):
        if c is not None:
            pos = jnp.where(row == r, c[0] * PEER_TOPK + c[1], pos)
    return pos


_PAD_POS = 1 << 20


def _joint_top16(v1, i1, v2, i2, pos):
    a_of = [c[0] if c else 0 for c in _CAND_ROWS]
    b_of = [c[1] if c else 0 for c in _CAND_ROWS]
    cand = jnp.where(pos < _PAD_POS, _rows(v1, a_of) + _rows(v2, b_of), -jnp.inf)
    cidx = _rows(i1, a_of) * N_KEYS + _rows(i2, b_of)
    vals, picks = [], []
    for _ in range(PEER_TOPK):
        m = jnp.max(cand, axis=0, keepdims=True)
        pm = jnp.min(jnp.where(cand == m, pos, 1 << 21), axis=0, keepdims=True)
        hit = pos == pm
        vals.append(m)
        picks.append(jnp.max(jnp.where(hit, cidx, -1), axis=0, keepdims=True))
        cand = jnp.where(hit, -jnp.inf, cand)
    return jnp.concatenate(vals, axis=0), jnp.concatenate(picks, axis=0)


def _route_kernel(q_ref, keys_ref, idx_ref, gate_ref):
    nt = (((1,), (1,)), ((), ()))
    pos = _cand_positions(LANES)
    for j in range(q_ref.shape[0] // LANES):
        q = q_ref[j * LANES:(j + 1) * LANES, :]
        s1 = lax.dot_general(keys_ref[0, 0], q[:, 0:D_HALF], nt, preferred_element_type=jnp.float32)
        s2 = lax.dot_general(keys_ref[0, 1], q[:, D_HALF:2 * D_HALF], nt, preferred_element_type=jnp.float32)
        v1, i1 = _top16_of_keys(s1)
        v2, i2 = _top16_of_keys(s2)
        top_s, top_i = _joint_top16(v1, i1, v2, i2, pos)
        e = jnp.exp(top_s - top_s[0:1])
        gate_ref[:, j * LANES:(j + 1) * LANES] = e / jnp.sum(e, axis=0, keepdims=True)
        idx_ref[:, j * LANES:(j + 1) * LANES] = top_i * ROW_SUB


def peer_route(q, sub_keys, *, tb=1024):
    T = q.shape[0]
    tb = min(tb, T)
    return pl.pallas_call(
        _route_kernel,
        grid=(T // tb, PEER_HEADS),
        in_specs=[pl.BlockSpec((tb, 2 * D_HALF), lambda i, h: (i, h)),
                  pl.BlockSpec((1, 2, N_KEYS, D_HALF), lambda i, h: (h, 0, 0, 0))],
        out_specs=[pl.BlockSpec((PEER_TOPK, tb), lambda i, h: (h, i)),
                   pl.BlockSpec((PEER_TOPK, tb), lambda i, h: (h, i))],
        out_shape=[jax.ShapeDtypeStruct((N_SLOTS, T), jnp.int32),
                   jax.ShapeDtypeStruct((N_SLOTS, T), jnp.float32)],
        compiler_params=_cparams(("parallel", "arbitrary")),
        name="peer_route",
    )(q, sub_keys)


def _pack_kernel(x_ref, o_ref, *, split_rows):
    x = x_ref[...]
    half = x.shape[1] // 2

    def bf16_bits(v):
        return pltpu.bitcast(v.astype(jnp.bfloat16).astype(jnp.float32), jnp.uint32)

    words = (bf16_bits(x[:, :half]) >> 16) | (bf16_bits(x[:, half:]) & jnp.uint32(0xFFFF0000))
    if split_rows:
        n = x.shape[0]
        for c in range(ROW_SUB):
            o_ref[pl.ds(c, n, stride=ROW_SUB), :] = words[:, c * LANES:(c + 1) * LANES]
    else:
        o_ref[...] = words


def pack_table(tbls, layer, *, split_rows, te=512):
    _, E, D = tbls.shape
    out_block, out_shape = ((te * ROW_SUB, LANES), (E * ROW_SUB, LANES)) if split_rows else ((te, D // 2), (E, D // 2))
    return pl.pallas_call(
        functools.partial(_pack_kernel, split_rows=split_rows),
        grid=(E // te,),
        in_specs=[pl.BlockSpec((None, te, D), lambda i: (layer, i, 0))],
        out_specs=pl.BlockSpec(out_block, lambda i: (i, 0)),
        out_shape=jax.ShapeDtypeStruct(out_shape, jnp.uint32),
        compiler_params=_cparams(("parallel",)),
        name="pack_table",
    )(tbls)


def _unpack(words):
    lo = pltpu.bitcast(words << 16, jnp.float32)
    hi = pltpu.bitcast(words & jnp.uint32(0xFFFF0000), jnp.float32)
    return lo, hi


def _gather_rows(idx_ref, tbl_ref, t, planes_ref):
    for k in range(N_SLOTS):
        off = pl.multiple_of(idx_ref[t, k], ROW_SUB)
        planes_ref[pl.ds(k, ROW_SUB, stride=PLANE_STRIDE), :] = tbl_ref[pl.ds(off, ROW_SUB), :]


def _plane(planes_ref, c):
    return _unpack(planes_ref[c * PLANE_STRIDE:c * PLANE_STRIDE + N_SLOTS, :])


def _peer_down_kernel(idx_ref, x_ref, gate_ref, tbl_ref, out_ref, planes_a, planes_b, part_ref, o_ref):
    tb = x_ref.shape[0]
    lane = lax.broadcasted_iota(jnp.int32, (N_SLOTS, tb), 1)

    def compute(t, planes_ref, slot):
        acc = jnp.zeros((N_SLOTS, LANES), jnp.float32)
        for c in range(ROW_SUB):
            lo, hi = _plane(planes_ref, c)
            acc = acc + lo * x_ref[t, c:c + 1, :] + hi * x_ref[t, ROW_SUB + c:ROW_SUB + c + 1, :]
        part_ref[slot] = acc

    def place_pair(t0):
        col_a = jnp.sum(part_ref[0], axis=1, keepdims=True)
        col_b = jnp.sum(part_ref[1], axis=1, keepdims=True)
        o_ref[...] = jnp.where(lane == t0 - 2, col_a, jnp.where(lane == t0 - 1, col_b, o_ref[...]))

    o_ref[...] = jnp.zeros(o_ref.shape, jnp.float32)
    part_ref[...] = jnp.zeros(part_ref.shape, jnp.float32)
    _gather_rows(idx_ref, tbl_ref, 0, planes_a)

    def pair(i, carry):
        t0 = 2 * i
        place_pair(t0)
        _gather_rows(idx_ref, tbl_ref, t0 + 1, planes_b)
        compute(t0, planes_a, 0)
        _gather_rows(idx_ref, tbl_ref, jnp.minimum(t0 + 2, tb - 1), planes_a)
        compute(t0 + 1, planes_b, 1)
        return carry

    lax.fori_loop(0, tb // 2, pair, 0)
    place_pair(tb)
    out_ref[...] = (_gelu(o_ref[...]) * gate_ref[...]).T


def peer_down(idx, xn3, gate_t, tbl, *, tb=128):
    T = xn3.shape[0]
    tb = min(tb, T)
    assert tb % 2 == 0 and T % tb == 0
    planes = pltpu.VMEM((ROW_SUB * PLANE_STRIDE, LANES), jnp.uint32)
    return pl.pallas_call(
        _peer_down_kernel,
        grid=(T // tb,),
        in_specs=[pl.BlockSpec((tb, N_SLOTS), lambda i: (i, 0), memory_space=pltpu.SMEM),
                  pl.BlockSpec((tb, SUBLANES, LANES), lambda i: (i, 0, 0)),
                  pl.BlockSpec((N_SLOTS, tb), lambda i: (0, i)),
                  pl.BlockSpec(tbl.shape, lambda i: (0, 0), pipeline_mode=pl.Buffered(1))],
        out_specs=pl.BlockSpec((tb, N_SLOTS), lambda i: (i, 0)),
        out_shape=jax.ShapeDtypeStruct((T, N_SLOTS), jnp.float32),
        scratch_shapes=[planes, planes, pltpu.VMEM((2, N_SLOTS, LANES), jnp.float32),
                        pltpu.VMEM((N_SLOTS, tb), jnp.float32)],
        compiler_params=_cparams(("arbitrary",)),
        name="peer_down",
    )(idx, xn3, gate_t, tbl)


SC_CORES = 2
SC_SUBCORES = 16
SC_WORKERS = SC_CORES * SC_SUBCORES
SC_LANES = 16
SC_GATHER_ROWS = N_SLOTS // 2
SC_ROW_BLOCK = 8
SC_TOKEN_BLOCK = 32
SC_COL_GROUP = 8


def _sc_params():
    cp = pltpu.CompilerParams()
    if "needs_layout_passes" in pltpu.CompilerParams.__dataclass_fields__:
        cp = dataclasses.replace(cp, needs_layout_passes=False)
    return cp


def peer_up_sc(ids, w, tbl):
    T = ids.shape[0]
    L, CH, RB, TBK = SC_LANES, SC_GATHER_ROWS, SC_ROW_BLOCK, SC_TOKEN_BLOCK
    assert T % (SC_WORKERS * TBK) == 0
    tpw = T // SC_WORKERS
    mesh = plsc.VectorSubcoreMesh(core_axis_name="c", subcore_axis_name="s")
    rows_buf = pltpu.VMEM((CH, ROW_WORDS), jnp.uint32)
    y_buf = pltpu.VMEM((D_MODEL,), jnp.float32)

    @functools.partial(
        pl.kernel, mesh=mesh, compiler_params=_sc_params(),
        out_type=jax.ShapeDtypeStruct((T, D_MODEL), jnp.float32),
        scratch_types=[pltpu.VMEM((TBK, N_SLOTS), jnp.int32), pltpu.VMEM((TBK, N_SLOTS), jnp.float32),
                       rows_buf, rows_buf, y_buf, y_buf] + [pltpu.SemaphoreType.DMA] * 4,
    )
    def up_kernel(ids_hbm, w_hbm, tbl_hbm, out_hbm, ids_v, w_v, rows0, rows1, y0, y1, g0, g1, o0, o1):
        base = (lax.axis_index("s") * SC_CORES + lax.axis_index("c")) * tpw
        rows, gsem, ys, osem = (rows0, rows1), (g0, g1), (y0, y1), (o0, o1)

        def gather(tl, half):
            return pltpu.make_async_copy(tbl_hbm.at[ids_v.at[tl, pl.ds(half * CH, CH)]], rows[half], gsem[half])

        def out_copy(tok, slot):
            return pltpu.make_async_copy(ys[slot], out_hbm.at[tok], osem[slot])

        def accumulate(tl, half, y_v):
            rows_v = rows[half]

            def block(b, carry):
                r0 = b * RB
                tlv = jnp.full((L,), tl, jnp.int32)
                wks = [plsc.load_gather(w_v, [tlv, jnp.full((L,), half * CH + r0 + r, jnp.int32)])
                       for r in range(RB)]
                for q in range(ROW_WORDS // (SC_COL_GROUP * L)):
                    acc = [None] * (2 * SC_COL_GROUP)
                    for r in range(RB):
                        for i in range(SC_COL_GROUP):
                            v = rows_v[r0 + r, pl.ds((q * SC_COL_GROUP + i) * L, L)]
                            lo = plsc.bitcast(v << 16, jnp.float32) * wks[r]
                            hi = plsc.bitcast(v & jnp.uint32(0xFFFF0000), jnp.float32) * wks[r]
                            acc[2 * i] = lo if r == 0 else acc[2 * i] + lo
                            acc[2 * i + 1] = hi if r == 0 else acc[2 * i + 1] + hi
                    for i in range(SC_COL_GROUP):
                        col = (q * SC_COL_GROUP + i) * L
                        plsc.addupdate(y_v.at[pl.ds(col, L)], acc[2 * i])
                        plsc.addupdate(y_v.at[pl.ds(ROW_WORDS + col, L)], acc[2 * i + 1])
                return carry

            lax.fori_loop(0, CH // RB, block, 0)

        def token_block(bi, carry):
            tok0 = base + bi * TBK
            pltpu.sync_copy(ids_hbm.at[pl.ds(tok0, TBK)], ids_v)
            pltpu.sync_copy(w_hbm.at[pl.ds(tok0, TBK)], w_v)
            gather(0, 0).start()

            def token_pair(pi, carry2):
                for slot in range(2):
                    tl = 2 * pi + slot
                    y_v = ys[slot]

                    @pl.when(pi > 0)
                    def _():
                        out_copy(tok0 + tl - 2, slot).wait()

                    zero = jnp.zeros((L,), jnp.float32)
                    for j in range(D_MODEL // L):
                        y_v[pl.ds(j * L, L)] = zero
                    gather(tl, 1).start()
                    gather(tl, 0).wait()
                    accumulate(tl, 0, y_v)

                    @pl.when(tl + 1 < TBK)
                    def _():
                        gather(tl + 1, 0).start()

                    gather(tl, 1).wait()
                    accumulate(tl, 1, y_v)
                    out_copy(tok0 + tl, slot).start()
                return carry2

            lax.fori_loop(0, TBK // 2, token_pair, 0)
            out_copy(tok0 + TBK - 2, 0).wait()
            out_copy(tok0 + TBK - 1, 1).wait()
            return carry

        lax.fori_loop(0, tpw // TBK, token_block, 0)

    return up_kernel(ids, w, tbl)


def peer_layer(h, g, w_query, sub_keys, down_rows, up_rows):
    T = h.shape[0]
    q, xn = norm_matmul(h, g, w_query, out_dtype=jnp.bfloat16, emit_xn=True)
    idx_t, gate_t = peer_route(q, sub_keys.astype(jnp.bfloat16))
    idx = idx_t.T
    w = peer_down(idx, xn.reshape(T, SUBLANES, LANES), gate_t, down_rows)
    return peer_up_sc(idx >> (ROW_SUB.bit_length() - 1), w, up_rows)


def _rms_kernel(x_ref, r_ref, g_ref, o_ref):
    x = x_ref[...] + r_ref[...]
    o_ref[...] = x * lax.rsqrt(jnp.mean(x * x, axis=-1, keepdims=True) + NORM_EPS) * g_ref[...]


def rms_norm_sum(x, r, g, *, tm=512):
    T, D = x.shape
    tm = min(tm, T)
    row = pl.BlockSpec((tm, D), lambda i: (i, 0))
    return pl.pallas_call(
        _rms_kernel,
        grid=(T // tm,),
        in_specs=[row, row, pl.BlockSpec((1, D), lambda i: (0, 0))],
        out_specs=row,
        out_shape=jax.ShapeDtypeStruct((T, D), jnp.float32),
        compiler_params=_cparams(("parallel",)),
        name="final_norm",
    )(x, r, g.reshape(1, D))


PIECE_SEQS = 2


def _piece_bounds(B):
    sizes = [PIECE_SEQS] * (B // PIECE_SEQS) + ([B % PIECE_SEQS] if B % PIECE_SEQS else [])
    if sizes[-1] > 1:
        sizes[-1:] = [sizes[-1] - sizes[-1] // 2, sizes[-1] // 2]
    starts = [sum(sizes[:i]) for i in range(len(sizes))]
    return list(zip(starts, sizes))


def kernel(x, rel_table, mix_norm_g, attn_w_in, attn_w_out, attn_sink, sg_w_in, sg_ln_g, sg_ln_b, sg_w_spatial, sg_b_spatial, sg_w_out, ffn_norm_g, peer_w_query, peer_sub_keys, peer_down, peer_up, final_norm_g):
    B, S, D = x.shape
    bf16 = jnp.bfloat16

    bias = attn_bias(rel_table)
    attn_in, attn_out = attn_w_in[0].astype(bf16), attn_w_out[0].astype(bf16)
    sg_in, sg_out = sg_w_in[0].astype(bf16), sg_w_out[0].astype(bf16)
    w_query = peer_w_query.astype(bf16)
    down_rows = [pack_table(peer_down, i, split_rows=True) for i in range(2)]
    up_rows = [pack_table(peer_up, i, split_rows=False) for i in range(2)]

    mid = []
    for b0, nb in _piece_bounds(B):
        h = x[b0:b0 + nb].reshape(nb * S, D)
        qkv, = norm_matmul(h, mix_norm_g[0], attn_in, out_dtype=bf16)
        att = window_attention(qkv.reshape(nb, S, QKV_DIM), bias, attn_sink[0])
        h = matmul_residual(att.reshape(nb * S, Q_DIM), attn_out, h)
        mid.append((h, peer_layer(h, ffn_norm_g[0], w_query[0], peer_sub_keys[0], down_rows[0], up_rows[0])))

    outs = []
    for h, y in mid:
        z, h = norm_matmul(h, mix_norm_g[1], sg_in, res=y, out_dtype=bf16, act=True, tn=1536)
        gated = spatial_gate(z, sg_ln_g[0], sg_ln_b[0], sg_w_spatial[0], sg_b_spatial[0])
        h = matmul_residual(gated, sg_out, h)
        y = peer_layer(h, ffn_norm_g[1], w_query[1], peer_sub_keys[1], down_rows[1], up_rows[1])
        outs.append(rms_norm_sum(h, y, final_norm_g).reshape(-1, S, D))
    return jnp.concatenate(outs, axis=0)
```
